```python
import jax, jax.numpy as jnp
from jax import lax
import numpy as np

D_MODEL = 1024
BATCH = 32
SEQ = 2048
DEPTH = 4

ATT_PATTERNS = ((128, 1), (512, 4), (2048, 16))
N_ATT_GROUPS = 3
ATT_HEADS_PER_GROUP = 4
ATT_HEAD_DIM = 128
ATT_QKV = N_ATT_GROUPS * ATT_HEADS_PER_GROUP * ATT_HEAD_DIM
ATT_OUT = ATT_HEADS_PER_GROUP * ATT_HEAD_DIM
ATT_BLOCK = 128
RET_HEADS = 4
RET_QK_DIM = D_MODEL // RET_HEADS
RET_V_DIM = D_MODEL // RET_HEADS
RET_QK = RET_HEADS * RET_QK_DIM
RET_WIDTH = RET_HEADS * RET_V_DIM
RET_CHUNK = 128
ROPE_BASE = 10000.0
EPS = 1e-6
NEG_INF = -1e30
SPLIT_SIZES = (ATT_QKV, ATT_QKV, ATT_QKV, ATT_OUT, RET_QK, RET_QK, RET_WIDTH, RET_WIDTH, D_MODEL, D_MODEL)
IN_WIDTH = sum(SPLIT_SIZES)
SPLIT_POINTS = tuple(int(v) for v in np.cumsum(SPLIT_SIZES)[:-1])

kernel_name = "hybrid_dilated_attn_retention_adaln"


def rmsnorm(x, w):
    xf = x.astype(jnp.float32)
    y = xf * lax.rsqrt(jnp.mean(xf * xf, axis=-1, keepdims=True) + EPS)
    return (y * w.astype(jnp.float32)).astype(x.dtype)


def dilated_window_attention(q, k, v, window, dil):
    B, S, G, Dh = q.shape
    L = S // dil
    span = window // dil
    blk = min(ATT_BLOCK, L)
    nb = -(-L // blk)
    Lp = nb * blk

    def to_sub(t):
        t = t.astype(jnp.float32).reshape(B, L, dil, G, Dh).transpose(0, 2, 3, 1, 4)
        t = jnp.pad(t, ((0, 0), (0, 0), (0, 0), (0, Lp - L), (0, 0)))
        return t.reshape(B, dil, G, nb, blk, Dh)

    def with_prev(t):
        prev = jnp.pad(t, ((0, 0), (0, 0), (0, 0), (1, 0), (0, 0), (0, 0)))[:, :, :, :-1]
        return jnp.concatenate([prev, t], axis=-2)

    qs = to_sub(q)
    kk = with_prev(to_sub(k))
    vv = with_prev(to_sub(v))
    s = jnp.einsum('bcgnqd,bcgnkd->bcgnqk', qs, kk) * (Dh ** -0.5)
    qi = jnp.arange(nb)[:, None, None] * blk + jnp.arange(blk)[None, :, None]
    ki = jnp.arange(nb)[:, None, None] * blk - blk + jnp.arange(2 * blk)[None, None, :]
    dist = qi - ki
    valid = (ki >= 0) & (dist >= 0) & (dist <= span)
    s = jnp.where(valid, s, NEG_INF)
    m = jnp.max(s, axis=-1, keepdims=True)
    p = jnp.exp(s - m)
    l = jnp.sum(p, axis=-1, keepdims=True)
    o = jnp.einsum('bcgnqk,bcgnkd->bcgnqd', p, vv) / l
    lse = (m + jnp.log(l))[..., 0]
    o = o.reshape(B, dil, G, Lp, Dh)[:, :, :, :L].transpose(0, 3, 1, 2, 4).reshape(B, S, G, Dh)
    lse = lse.reshape(B, dil, G, Lp)[..., :L].transpose(0, 3, 1, 2).reshape(B, S, G)
    return o, lse


def rotary(t, positions):
    d = t.shape[-1]
    half = d // 2
    theta = ROPE_BASE ** (-jnp.arange(half, dtype=jnp.float32) / half)
    ang = positions.astype(jnp.float32)[..., None] * theta
    cos, sin = jnp.cos(ang)[:, :, None, :], jnp.sin(ang)[:, :, None, :]
    tf = t.astype(jnp.float32)
    t1, t2 = tf[..., :half], tf[..., half:]
    return jnp.concatenate([t1 * cos - t2 * sin, t2 * cos + t1 * sin], axis=-1)


def retention(q, k, v, positions, gn_w):
    B, S, H, dk = q.shape
    dv = v.shape[-1]
    q = rotary(q, positions)
    k = rotary(k, positions) * (dk ** -0.5)
    C = RET_CHUNK
    N = S // C

    def chunks(t):
        return t.astype(jnp.float32).reshape(B, N, C, H, t.shape[-1]).transpose(1, 0, 3, 2, 4)

    log_g = jnp.log1p(-jnp.exp2(-5.0 - jnp.arange(H, dtype=jnp.float32)))
    idx = jnp.arange(C, dtype=jnp.float32)
    diff = idx[:, None] - idx[None, :]
    inner_decay = jnp.where(diff >= 0, jnp.exp(log_g[:, None, None] * jnp.maximum(diff, 0.0)), 0.0)
    q_decay = jnp.exp(log_g[:, None] * (idx + 1.0))
    k_decay = jnp.exp(log_g[:, None] * (C - 1.0 - idx))
    chunk_decay = jnp.exp(log_g * C)

    def step(state, qkv):
        qc, kc, vc = qkv
        att = jnp.einsum('bhnd,bhmd->bhnm', qc, kc) * inner_decay
        out = (jnp.einsum('bhnm,bhme->bhne', att, vc)
               + jnp.einsum('bhnd,bhde->bhne', qc, state) * q_decay[:, :, None])
        state = (state * chunk_decay[:, None, None]
                 + jnp.einsum('bhmd,bhme->bhde', kc * k_decay[:, :, None], vc))
        return state, out

    state0 = jnp.zeros((B, H, dk, dv), jnp.float32)
    _, out = lax.scan(step, state0, (chunks(q), chunks(k), chunks(v)))
    out = out.transpose(1, 0, 3, 2, 4).reshape(B, S, H, dv)
    mu = jnp.mean(out, axis=-1, keepdims=True)
    var = jnp.mean(jnp.square(out - mu), axis=-1, keepdims=True)
    out = (out - mu) * lax.rsqrt(var + EPS)
    return out.reshape(B, S, H * dv) * gn_w.astype(jnp.float32)


def hybrid_layer(x, c_act, positions, norm_w, w_ada, b_ada, w_in, ret_gn_w, w_proj_attn, w_proj_ret, w_out):
    B, S, D = x.shape
    mod = c_act @ w_ada + b_ada
    shift, scale, gate = jnp.split(mod, 3, axis=-1)
    h = rmsnorm(x, norm_w) * (1.0 + scale[:, None, :]) + shift[:, None, :]
    proj = h @ w_in
    qa, ka, va, za, qr, kr, vr, zr, ga, gr = jnp.split(proj, SPLIT_POINTS, axis=-1)

    att_shape = (B, S, N_ATT_GROUPS, ATT_HEADS_PER_GROUP, ATT_HEAD_DIM)
    qa, ka, va = qa.reshape(att_shape), ka.reshape(att_shape), va.reshape(att_shape)
    outs, lses = [], []
    for g, (window, dil) in enumerate(ATT_PATTERNS):
        o, lse = dilated_window_attention(qa[:, :, g], ka[:, :, g], va[:, :, g], window, dil)
        outs.append(o)
        lses.append(lse)
    wts = jax.nn.softmax(jnp.stack(lses, axis=0), axis=0)
    o_att = jnp.sum(wts[..., None] * jnp.stack(outs, axis=0), axis=0)
    y_att = o_att.reshape(B, S, ATT_OUT).astype(x.dtype) * jax.nn.silu(za)

    ret_qk = (B, S, RET_HEADS, RET_QK_DIM)
    o_ret = retention(qr.reshape(ret_qk), kr.reshape(ret_qk), vr.reshape(B, S, RET_HEADS, RET_V_DIM),
                      positions, ret_gn_w)
    y_ret = o_ret.astype(x.dtype) * jax.nn.silu(zr)

    merged = jax.nn.sigmoid(ga) * (y_att @ w_proj_attn) + jax.nn.sigmoid(gr) * (y_ret @ w_proj_ret)
    return x + gate[:, None, :] * (merged @ w_out)


def setup_inputs(seed: int = 0) -> dict:
    key = jax.random.key(seed)
    ks = jax.random.split(key, 12)
    f32 = jnp.float32
    D = D_MODEL
    x = jax.random.normal(ks[0], (BATCH, SEQ, D), f32)
    c = jax.random.normal(ks[1], (BATCH, D), f32)
    positions = jnp.tile(jnp.arange(SEQ, dtype=jnp.int32)[None, :], (BATCH, 1))
    norm_w = 1.0 + 0.01 * jax.random.normal(ks[2], (DEPTH, D), f32)
    w_ada = 0.5 * jax.random.normal(ks[3], (DEPTH, D, 3 * D), f32) * D ** -0.5
    b_ada = 0.01 * jax.random.normal(ks[4], (DEPTH, 3 * D), f32)
    w_in = jax.random.normal(ks[5], (DEPTH, D, IN_WIDTH), f32) * D ** -0.5
    ret_gn_w = 1.0 + 0.01 * jax.random.normal(ks[6], (DEPTH, RET_WIDTH), f32)
    w_proj_attn = jax.random.normal(ks[7], (DEPTH, ATT_OUT, D), f32) * ATT_OUT ** -0.5
    w_proj_ret = jax.random.normal(ks[8], (DEPTH, RET_WIDTH, D), f32) * RET_WIDTH ** -0.5
    w_out = jax.random.normal(ks[9], (DEPTH, D, D), f32) * D ** -0.5
    final_norm_w = 1.0 + 0.01 * jax.random.normal(ks[10], (D,), f32)
    return {"x": x, "c": c, "positions": positions, "norm_w": norm_w, "w_ada": w_ada, "b_ada": b_ada,
            "w_in": w_in, "ret_gn_w": ret_gn_w, "w_proj_attn": w_proj_attn, "w_proj_ret": w_proj_ret,
            "w_out": w_out, "final_norm_w": final_norm_w}


def reference(x, c, positions, norm_w, w_ada, b_ada, w_in, ret_gn_w, w_proj_attn, w_proj_ret, w_out, final_norm_w):
    c_act = jax.nn.silu(c)
    for l in range(DEPTH):
        x = hybrid_layer(x, c_act, positions, norm_w[l], w_ada[l], b_ada[l], w_in[l], ret_gn_w[l],
                         w_proj_attn[l], w_proj_ret[l], w_out[l])
    return rmsnorm(x, final_norm_w)
```

```python
import functools

import numpy as np
import jax
import jax.numpy as jnp
from jax import lax
from jax.experimental import pallas as pl
from jax.experimental.pallas import tpu as pltpu

D_MODEL = 1024
SEQ = 2048
DEPTH = 4
ATT_DILATIONS = (1, 4, 16)
N_ATT_GROUPS = 3
ATT_SLOTS = 4
ATT_HEAD_DIM = 128
ATT_BLOCK = 128
RET_HEADS = 4
RET_DIM = 256
RET_CHUNK = 128
ROPE_BASE = 10000.0
EPS = 1e-6
NEG_INF = -1e30

LANES = 128
COL_BLOCK = 512
BLOCKS_PER_STEP = COL_BLOCK // LANES
ATT_QKV = N_ATT_GROUPS * ATT_SLOTS * ATT_HEAD_DIM
IN_WIDTH = 3 * ATT_QKV + ATT_SLOTS * ATT_HEAD_DIM + 4 * RET_HEADS * RET_DIM + 2 * D_MODEL
N_COL_STEPS = IN_WIDTH // COL_BLOCK
N_ATT_COL_STEPS = 3 * N_ATT_GROUPS
BLK_Q = 0
BLK_K = ATT_QKV // LANES
BLK_V = 2 * ATT_QKV // LANES
BLK_ZA = 3 * ATT_QKV // LANES
BLK_QR = BLK_ZA + ATT_SLOTS
BLK_KR = BLK_QR + RET_HEADS * RET_DIM // LANES
BLK_VR = BLK_KR + RET_HEADS * RET_DIM // LANES
BLK_ZR = BLK_VR + RET_HEADS * RET_DIM // LANES
BLK_GA = BLK_ZR + RET_HEADS * RET_DIM // LANES
BLK_GR = BLK_GA + D_MODEL // LANES
N_BLOCKS = IN_WIDTH // LANES

VMEM_LIMIT = 56 * 1024 * 1024
F32 = jnp.float32
BF16 = jnp.bfloat16


def _params(*sem):
    return pltpu.CompilerParams(dimension_semantics=sem, vmem_limit_bytes=VMEM_LIMIT)


def _ada_kernel(c_ref, w_ref, b_ref, o_ref):
    c = c_ref[...]
    c_act = (c * jax.nn.sigmoid(c)).astype(BF16)
    acc = jnp.dot(c_act, w_ref[0].astype(BF16), preferred_element_type=F32)
    o_ref[0] = acc + b_ref[0]


def _ada(c, w_ada, b_ada):
    B = c.shape[0]
    n_col = 3 * D_MODEL // D_MODEL
    return pl.pallas_call(
        _ada_kernel,
        grid=(DEPTH, n_col),
        in_specs=[
            pl.BlockSpec((B, D_MODEL), lambda l, j: (0, 0)),
            pl.BlockSpec((1, D_MODEL, D_MODEL), lambda l, j: (l, 0, j)),
            pl.BlockSpec((1, 1, D_MODEL), lambda l, j: (l, 0, j)),
        ],
        out_specs=pl.BlockSpec((1, B, D_MODEL), lambda l, j: (l, 0, j)),
        out_shape=jax.ShapeDtypeStruct((DEPTH, B, 3 * D_MODEL), F32),
        compiler_params=_params("arbitrary", "arbitrary"),
    )(c, w_ada, b_ada.reshape(DEPTH, 1, 3 * D_MODEL))


def _rope_kernel(pos_ref, theta_ref, cos_ref, sin_ref):
    ang = pos_ref[0].astype(F32) * theta_ref[...]
    cos_ref[0] = jnp.cos(ang)
    sin_ref[0] = jnp.sin(ang)


def _rope_tables(positions):
    B, S = positions.shape
    half = RET_DIM // 2
    theta = ROPE_BASE ** (-jnp.arange(half, dtype=F32) / half)
    spec = pl.BlockSpec((1, S, half), lambda b: (b, 0, 0))
    return pl.pallas_call(
        _rope_kernel,
        grid=(B,),
        in_specs=[pl.BlockSpec((1, S, 1), lambda b: (b, 0, 0)),
                  pl.BlockSpec((1, half), lambda b: (0, 0))],
        out_specs=[spec, spec],
        out_shape=[jax.ShapeDtypeStruct((B, S, half), F32)] * 2,
        compiler_params=_params("arbitrary"),
    )(positions.reshape(B, S, 1), theta.reshape(1, half))


def _inproj_kernel(x_ref, mod_ref, nw_ref, w_ref, o_ref, h_ref, hn_ref):
    j = pl.program_id(1)
    S = x_ref.shape[1]
    rows = 128
    n_lane_blocks = D_MODEL // LANES

    @pl.when(j == 0)
    def _():
        shift = mod_ref[0, :, 0:D_MODEL]
        scale = mod_ref[0, :, D_MODEL:2 * D_MODEL]
        wmul = nw_ref[...] * (1.0 + scale)

        def natural(c, carry):
            idx = pl.ds(pl.multiple_of(c * rows, rows), rows)
            xs = x_ref[0, idx, :]
            ms = jnp.mean(xs * xs, axis=-1, keepdims=True)
            hv = xs * lax.rsqrt(ms + EPS) * wmul + shift
            h_ref[0, idx, :] = hv.astype(BF16)
            for cb in range(n_lane_blocks):
                hn_ref[cb, idx, :] = hv[:, cb * LANES:(cb + 1) * LANES]
            return carry

        lax.fori_loop(0, S // rows, natural, 0)

        def permuted(c, carry):
            p0 = pl.multiple_of(c * rows, rows)
            for v, d in enumerate(ATT_DILATIONS):
                if d == 1:
                    continue
                sub_len = S // d
                start = (p0 % sub_len) * d + p0 // sub_len
                hv = jnp.concatenate(
                    [hn_ref[cb, pl.ds(start, rows, stride=d), :] for cb in range(n_lane_blocks)],
                    axis=-1)
                h_ref[v, pl.ds(p0, rows), :] = hv.astype(BF16)
            return carry

        lax.fori_loop(0, S // rows, permuted, 0)

    sel = jnp.where(j < N_ATT_COL_STEPS, lax.rem(j, N_ATT_GROUPS), 0)
    res = jnp.dot(h_ref[sel], w_ref[...], preferred_element_type=F32)
    for cblk in range(BLOCKS_PER_STEP):
        o_ref[0, cblk] = res[:, cblk * LANES:(cblk + 1) * LANES].astype(BF16)


def _inproj(x, mod_l, norm_w_l, w_in_l):
    B, S, D = x.shape
    return pl.pallas_call(
        _inproj_kernel,
        grid=(B, N_COL_STEPS),
        in_specs=[
            pl.BlockSpec((1, S, D), lambda b, j: (b, 0, 0)),
            pl.BlockSpec((1, 1, 3 * D), lambda b, j: (b, 0, 0)),
            pl.BlockSpec((1, D), lambda b, j: (0, 0)),
            pl.BlockSpec((D, COL_BLOCK), lambda b, j: (0, j)),
        ],
        out_specs=pl.BlockSpec((1, BLOCKS_PER_STEP, S, LANES), lambda b, j: (b, j, 0, 0)),
        out_shape=jax.ShapeDtypeStruct((B, N_BLOCKS, S, LANES), BF16),
        scratch_shapes=[pltpu.VMEM((N_ATT_GROUPS, S, D), BF16),
                        pltpu.VMEM((D // LANES, S, LANES), F32)],
        compiler_params=_params("arbitrary", "arbitrary"),
    )(x, mod_l, norm_w_l.reshape(1, D), w_in_l)


def _attn_kernel(q0, k0, v0, q1, k1, v1, q2, k2, v2, z_ref, y_ref, o_sc, l_sc):
    S = y_ref.shape[2]
    T = ATT_BLOCK
    scale = ATT_HEAD_DIM ** -0.5
    row = lax.broadcasted_iota(jnp.int32, (T, 2 * T), 0)
    col = lax.broadcasted_iota(jnp.int32, (T, 2 * T), 1)
    mask_band = (col >= row) & (col <= row + T)
    mask_first = (lax.broadcasted_iota(jnp.int32, (T, T), 1)
                  <= lax.broadcasted_iota(jnp.int32, (T, T), 0))

    def block(q, k, v, mask):
        s = lax.dot_general(q, k, (((1,), (1,)), ((), ())), preferred_element_type=F32) * scale
        s = jnp.where(mask, s, NEG_INF)
        m = jnp.max(s, axis=-1, keepdims=True)
        p = jnp.exp(s - m)
        l = jnp.sum(p, axis=-1, keepdims=True)
        o = jnp.dot(p.astype(BF16), v, preferred_element_type=F32) / l
        return o, m + jnp.log(l)

    def run_block(g, refs, q_start, first, out_idx):
        q_ref, k_ref, v_ref = refs
        q = q_ref[0, 0, pl.ds(q_start, T), :]
        if first:
            k = k_ref[0, 0, pl.ds(q_start, T), :]
            v = v_ref[0, 0, pl.ds(q_start, T), :]
            o, lse = block(q, k, v, mask_first)
        else:
            k = k_ref[0, 0, pl.ds(q_start - T, 2 * T), :]
            v = v_ref[0, 0, pl.ds(q_start - T, 2 * T), :]
            o, lse = block(q, k, v, mask_band)
        o_sc[g, out_idx, :] = o
        l_sc[g, out_idx, :] = jnp.broadcast_to(lse, (T, LANES))

    refs0 = (q0, k0, v0)
    run_block(0, refs0, 0, True, pl.ds(0, T))

    def g0_body(c, carry):
        for i in range(5):
            start = pl.multiple_of((1 + c * 5 + i) * T, T)
            run_block(0, refs0, start, False, pl.ds(start, T))
        return carry

    lax.fori_loop(0, 3, g0_body, 0)

    d1 = ATT_DILATIONS[1]
    len1 = S // d1
    nb1 = len1 // T

    def g1_body(r, carry):
        base = pl.multiple_of(r * len1, len1)
        for n in range(nb1):
            out_idx = pl.ds(n * T * d1 + r, T, stride=d1)
            run_block(1, (q1, k1, v1), base + n * T, n == 0, out_idx)
        return carry

    lax.fori_loop(0, d1, g1_body, 0)

    d2 = ATT_DILATIONS[2]

    def g2_body(c, carry):
        for i in range(4):
            r = c * 4 + i
            start = pl.multiple_of(r * T, T)
            run_block(2, (q2, k2, v2), start, True, pl.ds(r, T, stride=d2))
        return carry

    lax.fori_loop(0, d2 // 4, g2_body, 0)

    rows = 256

    def comb(c, carry):
        idx = pl.ds(pl.multiple_of(c * rows, rows), rows)
        l0, l1, l2 = l_sc[0, idx, :], l_sc[1, idx, :], l_sc[2, idx, :]
        mx = jnp.maximum(jnp.maximum(l0, l1), l2)
        w0, w1, w2 = jnp.exp(l0 - mx), jnp.exp(l1 - mx), jnp.exp(l2 - mx)
        o = (w0 * o_sc[0, idx, :] + w1 * o_sc[1, idx, :] + w2 * o_sc[2, idx, :]) / (w0 + w1 + w2)
        z = z_ref[0, 0, idx, :].astype(F32)
        y_ref[0, 0, idx, :] = (o * (z * jax.nn.sigmoid(z))).astype(BF16)
        return carry

    lax.fori_loop(0, S // rows, comb, 0)


def _attn(proj):
    B, _, S, _ = proj.shape

    def spec(base, g):
        return pl.BlockSpec((1, 1, S, LANES), lambda b, s: (b, base + g * ATT_SLOTS + s, 0, 0))

    in_specs = []
    for g in range(N_ATT_GROUPS):
        in_specs += [spec(BLK_Q, g), spec(BLK_K, g), spec(BLK_V, g)]
    in_specs.append(spec(BLK_ZA, 0))
    return pl.pallas_call(
        _attn_kernel,
        grid=(B, ATT_SLOTS),
        in_specs=in_specs,
        out_specs=pl.BlockSpec((1, 1, S, LANES), lambda b, s: (b, s, 0, 0)),
        out_shape=jax.ShapeDtypeStruct((B, ATT_SLOTS, S, LANES), BF16),
        scratch_shapes=[pltpu.VMEM((N_ATT_GROUPS, S, LANES), F32),
                        pltpu.VMEM((N_ATT_GROUPS, S, LANES), F32)],
        compiler_params=_params("arbitrary", "arbitrary"),
    )(*([proj] * 10))


def _ret_decay_tables():
    H, C = RET_HEADS, RET_CHUNK
    log_g = np.log1p(-np.exp2(-5.0 - np.arange(H, dtype=np.float64)))
    idx = np.arange(C, dtype=np.float64)
    diff = idx[:, None] - idx[None, :]
    inner = np.where(diff >= 0, np.exp(log_g[:, None, None] * np.maximum(diff, 0.0)), 0.0)
    q_decay = np.exp(log_g[:, None] * (idx + 1.0))
    k_decay = np.exp(log_g[:, None] * (C - 1.0 - idx))
    chunk_decay = np.exp(log_g * C)
    f = lambda a: jnp.asarray(a, dtype=F32)
    return f(inner), f(q_decay[:, :, None]), f(k_decay[:, :, None]), f(chunk_decay)


def _ret_kernel(cd_ref, q_ref, k_ref, v_ref, z_ref, cos_ref, sin_ref, inner_ref, qd_ref, kd_ref,
                gnw_ref, y_ref, st_ref):
    S = y_ref.shape[2]
    C = RET_CHUNK
    half = RET_DIM // 2
    cd = cd_ref[pl.program_id(1)]
    k_scale = RET_DIM ** -0.5
    st_ref[...] = jnp.zeros_like(st_ref)

    def rot(t_ref, idx, cos, sin):
        t1 = t_ref[0, 0, idx, :].astype(F32)
        t2 = t_ref[0, 1, idx, :].astype(F32)
        return jnp.concatenate([t1 * cos - t2 * sin, t2 * cos + t1 * sin], axis=-1)

    def chunk(n, carry):
        idx = pl.ds(pl.multiple_of(n * C, C), C)
        cos, sin = cos_ref[0, idx, :], sin_ref[0, idx, :]
        qc = rot(q_ref, idx, cos, sin).astype(BF16)
        kf = rot(k_ref, idx, cos, sin) * k_scale
        kc = kf.astype(BF16)
        vc = jnp.concatenate([v_ref[0, 0, idx, :], v_ref[0, 1, idx, :]], axis=-1)
        state = st_ref[...]
        att = lax.dot_general(qc, kc, (((1,), (1,)), ((), ())), preferred_element_type=F32)
        att = (att * inner_ref[0]).astype(BF16)
        out = (jnp.dot(att, vc, preferred_element_type=F32)
               + jnp.dot(qc, state.astype(BF16), preferred_element_type=F32) * qd_ref[0])
        kdt = (kf * kd_ref[0]).T.astype(BF16)
        st_ref[...] = state * cd + jnp.dot(kdt, vc, preferred_element_type=F32)
        mu = jnp.mean(out, axis=-1, keepdims=True)
        cen = out - mu
        var = jnp.mean(cen * cen, axis=-1, keepdims=True)
        o = cen * lax.rsqrt(var + EPS) * gnw_ref[...]
        z = jnp.concatenate([z_ref[0, 0, idx, :], z_ref[0, 1, idx, :]], axis=-1).astype(F32)
        y = (o * (z * jax.nn.sigmoid(z))).astype(BF16)
        y_ref[0, 0, idx, :] = y[:, :half]
        y_ref[0, 1, idx, :] = y[:, half:]
        return carry

    lax.fori_loop(0, S // C, chunk, 0)


def _ret(proj, cos, sin, gn_w_l):
    B, _, S, _ = proj.shape
    inner, q_decay, k_decay, chunk_decay = _ret_decay_tables()
    per_head = RET_DIM // LANES

    def spec(base):
        return pl.BlockSpec((1, per_head, S, LANES), lambda b, h: (b, base // per_head + h, 0, 0))

    tab = pl.BlockSpec((1, S, LANES), lambda b, h: (b, 0, 0))
    return pl.pallas_call(
        _ret_kernel,
        grid=(B, RET_HEADS),
        in_specs=[
            pl.BlockSpec(memory_space=pltpu.SMEM),
            spec(BLK_QR), spec(BLK_KR), spec(BLK_VR), spec(BLK_ZR), tab, tab,
            pl.BlockSpec((1, RET_CHUNK, RET_CHUNK), lambda b, h: (h, 0, 0)),
            pl.BlockSpec((1, RET_CHUNK, 1), lambda b, h: (h, 0, 0)),
            pl.BlockSpec((1, RET_CHUNK, 1), lambda b, h: (h, 0, 0)),
            pl.BlockSpec((1, RET_DIM), lambda b, h: (0, h)),
        ],
        out_specs=pl.BlockSpec((1, per_head, S, LANES), lambda b, h: (b, h, 0, 0)),
        out_shape=jax.ShapeDtypeStruct((B, RET_HEADS * per_head, S, LANES), BF16),
        scratch_shapes=[pltpu.VMEM((RET_DIM, RET_DIM), F32)],
        compiler_params=_params("arbitrary", "arbitrary"),
    )(chunk_decay, proj, proj, proj, proj, cos, sin, inner, q_decay, k_decay,
      gn_w_l.reshape(1, RET_HEADS * RET_DIM))


def _outproj_kernel(ya_ref, yr_ref, ga_ref, gr_ref, x_ref, gate_ref, wpa_ref, wpr_ref, wo_ref,
                    fnw_ref, o_ref, *, final_norm):
    def cat(ref):
        return jnp.concatenate([ref[0, i] for i in range(ref.shape[1])], axis=-1)

    a = jnp.dot(cat(ya_ref), wpa_ref[...], preferred_element_type=F32)
    r = jnp.dot(cat(yr_ref), wpr_ref[...], preferred_element_type=F32)
    merged = (jax.nn.sigmoid(cat(ga_ref).astype(F32)) * a
              + jax.nn.sigmoid(cat(gr_ref).astype(F32)) * r)
    out = x_ref[0] + gate_ref[0] * jnp.dot(merged.astype(BF16), wo_ref[...],
                                           preferred_element_type=F32)
    if final_norm:
        ms = jnp.mean(out * out, axis=-1, keepdims=True)
        out = out * lax.rsqrt(ms + EPS) * fnw_ref[...]
    o_ref[0] = out


def _outproj(x, ya, yr, proj, mod_l, wpa, wpr, wo, final_norm_w, final_norm):
    B, S, D = x.shape
    tm = 512
    n_g = D // LANES
    return pl.pallas_call(
        functools.partial(_outproj_kernel, final_norm=final_norm),
        grid=(B, S // tm),
        in_specs=[
            pl.BlockSpec((1, ya.shape[1], tm, LANES), lambda b, i: (b, 0, i, 0)),
            pl.BlockSpec((1, yr.shape[1], tm, LANES), lambda b, i: (b, 0, i, 0)),
            pl.BlockSpec((1, n_g, tm, LANES), lambda b, i: (b, BLK_GA // n_g, i, 0)),
            pl.BlockSpec((1, n_g, tm, LANES), lambda b, i: (b, BLK_GR // n_g, i, 0)),
            pl.BlockSpec((1, tm, D), lambda b, i: (b, i, 0)),
            pl.BlockSpec((1, 1, D), lambda b, i: (b, 0, 2)),
            pl.BlockSpec(wpa.shape, lambda b, i: (0, 0)),
            pl.BlockSpec(wpr.shape, lambda b, i: (0, 0)),
            pl.BlockSpec(wo.shape, lambda b, i: (0, 0)),
            pl.BlockSpec((1, D), lambda b, i: (0, 0)),
        ],
        out_specs=pl.BlockSpec((1, tm, D), lambda b, i: (b, i, 0)),
        out_shape=jax.ShapeDtypeStruct((B, S, D), F32),
        compiler_params=_params("arbitrary", "arbitrary"),
    )(ya, yr, proj, proj, x, mod_l, wpa, wpr, wo, final_norm_w.reshape(1, D))


def kernel(x, c, positions, norm_w, w_ada, b_ada, w_in, ret_gn_w, w_proj_attn, w_proj_ret, w_out,
           final_norm_w):
    B = x.shape[0]
    mod = _ada(c, w_ada, b_ada).reshape(DEPTH, B, 1, 3 * D_MODEL)
    cos, sin = _rope_tables(positions)
    w_in_b = w_in.astype(BF16)
    wpa_b = w_proj_attn.astype(BF16)
    wpr_b = w_proj_ret.astype(BF16)
    wo_b = w_out.astype(BF16)
    for l in range(DEPTH):
        proj = _inproj(x, mod[l], norm_w[l], w_in_b[l])
        ya = _attn(proj)
        yr = _ret(proj, cos, sin, ret_gn_w[l])
        x = _outproj(x, ya, yr, proj, mod[l], wpa_b[l], wpr_b[l], wo_b[l], final_norm_w,
                     final_norm=(l == DEPTH - 1))
    return x
```

```python
import functools

import numpy as np
import jax
import jax.numpy as jnp
from jax import lax
from jax.experimental import pallas as pl
from jax.experimental.pallas import tpu as pltpu

D_MODEL = 1024
SEQ = 2048
DEPTH = 4
ATT_DILATIONS = (1, 4, 16)
N_ATT_GROUPS = 3
ATT_SLOTS = 4
ATT_HEAD_DIM = 128
ATT_BLOCK = 128
RET_HEADS = 4
RET_DIM = 256
RET_CHUNK = 128
ROPE_BASE = 10000.0
EPS = 1e-6
NEG_INF = -1e30

LANES = 128
COL_BLOCK = 512
BLOCKS_PER_STEP = COL_BLOCK // LANES
ATT_QKV = N_ATT_GROUPS * ATT_SLOTS * ATT_HEAD_DIM
IN_WIDTH = 3 * ATT_QKV + ATT_SLOTS * ATT_HEAD_DIM + 4 * RET_HEADS * RET_DIM + 2 * D_MODEL
N_COL_STEPS = IN_WIDTH // COL_BLOCK
N_ATT_COL_STEPS = 3 * N_ATT_GROUPS
BLK_Q = 0
BLK_K = ATT_QKV // LANES
BLK_V = 2 * ATT_QKV // LANES
BLK_ZA = 3 * ATT_QKV // LANES
BLK_QR = BLK_ZA + ATT_SLOTS
BLK_KR = BLK_QR + RET_HEADS * RET_DIM // LANES
BLK_VR = BLK_KR + RET_HEADS * RET_DIM // LANES
BLK_ZR = BLK_VR + RET_HEADS * RET_DIM // LANES
BLK_GA = BLK_ZR + RET_HEADS * RET_DIM // LANES
BLK_GR = BLK_GA + D_MODEL // LANES
N_BLOCKS = IN_WIDTH // LANES

VMEM_LIMIT = 56 * 1024 * 1024
F32 = jnp.float32
BF16 = jnp.bfloat16


def _params(*sem):
    return pltpu.CompilerParams(dimension_semantics=sem, vmem_limit_bytes=VMEM_LIMIT)


def _ada_kernel(c_ref, w_ref, b_ref, o_ref):
    c = c_ref[...]
    c_act = (c * jax.nn.sigmoid(c)).astype(BF16)
    acc = jnp.dot(c_act, w_ref[0].astype(BF16), preferred_element_type=F32)
    o_ref[0] = acc + b_ref[0]


def _ada(c, w_ada, b_ada):
    B = c.shape[0]
    n_col = 3 * D_MODEL // D_MODEL
    return pl.pallas_call(
        _ada_kernel,
        grid=(DEPTH, n_col),
        in_specs=[
            pl.BlockSpec((B, D_MODEL), lambda l, j: (0, 0)),
            pl.BlockSpec((1, D_MODEL, D_MODEL), lambda l, j: (l, 0, j)),
            pl.BlockSpec((1, 1, D_MODEL), lambda l, j: (l, 0, j)),
        ],
        out_specs=pl.BlockSpec((1, B, D_MODEL), lambda l, j: (l, 0, j)),
        out_shape=jax.ShapeDtypeStruct((DEPTH, B, 3 * D_MODEL), F32),
        compiler_params=_params("arbitrary", "arbitrary"),
        name="ada",
    )(c, w_ada, b_ada.reshape(DEPTH, 1, 3 * D_MODEL))


def _rope_kernel(pos_ref, theta_ref, cos_ref, sin_ref):
    ang = pos_ref[0].astype(F32) * theta_ref[...]
    cos_ref[0] = jnp.cos(ang)
    sin_ref[0] = jnp.sin(ang)


def _rope_tables(positions):
    B, S = positions.shape
    half = RET_DIM // 2
    theta = ROPE_BASE ** (-jnp.arange(half, dtype=F32) / half)
    spec = pl.BlockSpec((1, S, half), lambda b: (b, 0, 0))
    return pl.pallas_call(
        _rope_kernel,
        grid=(B,),
        in_specs=[pl.BlockSpec((1, S, 1), lambda b: (b, 0, 0)),
                  pl.BlockSpec((1, half), lambda b: (0, 0))],
        out_specs=[spec, spec],
        out_shape=[jax.ShapeDtypeStruct((B, S, half), F32)] * 2,
        compiler_params=_params("arbitrary"),
        name="rope",
    )(positions.reshape(B, S, 1), theta.reshape(1, half))


def _inproj_kernel(x_ref, mod_ref, nw_ref, w_ref, o_ref, h_ref, hn_ref):
    j = pl.program_id(1)
    S = x_ref.shape[1]
    rows = 128
    n_lane_blocks = D_MODEL // LANES

    @pl.when(j == 0)
    def _():
        shift = mod_ref[0, :, 0:D_MODEL]
        scale = mod_ref[0, :, D_MODEL:2 * D_MODEL]
        wmul = nw_ref[...] * (1.0 + scale)

        def natural(c, carry):
            idx = pl.ds(pl.multiple_of(c * rows, rows), rows)
            xs = x_ref[0, idx, :]
            ms = jnp.mean(xs * xs, axis=-1, keepdims=True)
            hv = xs * lax.rsqrt(ms + EPS) * wmul + shift
            h_ref[0, idx, :] = hv.astype(BF16)
            for cb in range(n_lane_blocks):
                hn_ref[cb, idx, :] = hv[:, cb * LANES:(cb + 1) * LANES]
            return carry

        lax.fori_loop(0, S // rows, natural, 0)

        def permuted(c, carry):
            p0 = pl.multiple_of(c * rows, rows)
            for v, d in enumerate(ATT_DILATIONS):
                if d == 1:
                    continue
                sub_len = S // d
                start = (p0 % sub_len) * d + p0 // sub_len
                hv = jnp.concatenate(
                    [hn_ref[cb, pl.ds(start, rows, stride=d), :] for cb in range(n_lane_blocks)],
                    axis=-1)
                h_ref[v, pl.ds(p0, rows), :] = hv.astype(BF16)
            return carry

        lax.fori_loop(0, S // rows, permuted, 0)

    sel = jnp.where(j < N_ATT_COL_STEPS, lax.rem(j, N_ATT_GROUPS), 0)
    res = jnp.dot(h_ref[sel], w_ref[...], preferred_element_type=F32)
    for cblk in range(BLOCKS_PER_STEP):
        o_ref[0, cblk] = res[:, cblk * LANES:(cblk + 1) * LANES].astype(BF16)


def _inproj(x, mod_l, norm_w_l, w_in_l):
    B, S, D = x.shape
    return pl.pallas_call(
        _inproj_kernel,
        grid=(B, N_COL_STEPS),
        in_specs=[
            pl.BlockSpec((1, S, D), lambda b, j: (b, 0, 0)),
            pl.BlockSpec((1, 1, 3 * D), lambda b, j: (b, 0, 0)),
            pl.BlockSpec((1, D), lambda b, j: (0, 0)),
            pl.BlockSpec((D, COL_BLOCK), lambda b, j: (0, j)),
        ],
        out_specs=pl.BlockSpec((1, BLOCKS_PER_STEP, S, LANES), lambda b, j: (b, j, 0, 0)),
        out_shape=jax.ShapeDtypeStruct((B, N_BLOCKS, S, LANES), BF16),
        scratch_shapes=[pltpu.VMEM((N_ATT_GROUPS, S, D), BF16),
                        pltpu.VMEM((D // LANES, S, LANES), F32)],
        compiler_params=_params("arbitrary", "arbitrary"),
        name="inproj",
    )(x, mod_l, norm_w_l.reshape(1, D), w_in_l)


def _attn_kernel(q0, k0, v0, q1, k1, v1, q2, k2, v2, z_ref, y_ref,
                 sb_sc, sd_sc, m_sc, acc_sc, l_sc):
    S = y_ref.shape[2]
    T = ATT_BLOCK
    c_exp2 = ATT_HEAD_DIM ** -0.5 * np.log2(np.e)
    row = lax.broadcasted_iota(jnp.int32, (T, 2 * T), 0)
    col = lax.broadcasted_iota(jnp.int32, (T, 2 * T), 1)
    mask_band = (col >= row) & (col <= row + T)
    mask_diag = (lax.broadcasted_iota(jnp.int32, (T, T), 1)
                 <= lax.broadcasted_iota(jnp.int32, (T, T), 0))
    groups = ((q0, k0, v0), (q1, k1, v1), (q2, k2, v2))
    d1, d2 = ATT_DILATIONS[1], ATT_DILATIONS[2]
    len1 = S // d1
    nb1 = len1 // T
    n_band0 = S // T - 1

    def scores(g, q_start, band, blk, out_idx):
        q_ref, k_ref, _ = groups[g]
        q = q_ref[0, 0, pl.ds(q_start, T), :]
        if band:
            k = k_ref[0, 0, pl.ds(q_start - T, 2 * T), :]
        else:
            k = k_ref[0, 0, pl.ds(q_start, T), :]
        s = lax.dot_general(q, k, (((1,), (1,)), ((), ())), preferred_element_type=F32) * c_exp2
        s = jnp.where(mask_band if band else mask_diag, s, NEG_INF)
        (sb_sc if band else sd_sc)[blk] = s
        m_sc[g, out_idx, :] = jnp.broadcast_to(jnp.max(s, axis=-1, keepdims=True), (T, LANES))

    def weighted_values(g, q_start, band, blk, out_idx):
        v_ref = groups[g][2]
        m = m_sc[0, out_idx, :]
        if band:
            p = jnp.exp2(sb_sc[blk] - jnp.concatenate([m, m], axis=-1))
            v = v_ref[0, 0, pl.ds(q_start - T, 2 * T), :]
        else:
            p = jnp.exp2(sd_sc[blk] - m)
            v = v_ref[0, 0, pl.ds(q_start, T), :]
        v1 = jnp.concatenate([v, jnp.ones_like(v)], axis=-1)
        r = jnp.dot(p.astype(BF16), v1, preferred_element_type=F32)
        acc_sc[g, out_idx, :] = r[:, :LANES]
        l_sc[g, out_idx, :] = r[:, LANES:]

    def all_blocks(fn):
        fn(0, 0, False, 0, pl.ds(0, T))
        for i in range(n_band0):
            fn(0, (i + 1) * T, True, i, pl.ds((i + 1) * T, T))
        for r in range(d1):
            fn(1, r * len1, False, 1 + r, pl.ds(r, T, stride=d1))
            for n in range(1, nb1):
                fn(1, r * len1 + n * T, True, n_band0 + r * (nb1 - 1) + n - 1,
                   pl.ds(n * T * d1 + r, T, stride=d1))
        for r in range(d2):
            fn(2, r * T, False, 1 + d1 + r, pl.ds(r, T, stride=d2))

    all_blocks(scores)

    rows = 256

    def shared_max(c, carry):
        idx = pl.ds(pl.multiple_of(c * rows, rows), rows)
        m_sc[0, idx, :] = jnp.maximum(jnp.maximum(m_sc[0, idx, :], m_sc[1, idx, :]), m_sc[2, idx, :])
        return carry

    lax.fori_loop(0, S // rows, shared_max, 0)

    all_blocks(weighted_values)

    def finish(c, carry):
        idx = pl.ds(pl.multiple_of(c * rows, rows), rows)
        z = z_ref[0, 0, idx, :].astype(F32)
        o = ((acc_sc[0, idx, :] + acc_sc[1, idx, :] + acc_sc[2, idx, :])
             / (l_sc[0, idx, :] + l_sc[1, idx, :] + l_sc[2, idx, :]))
        y_ref[0, 0, idx, :] = (o * (z * jax.nn.sigmoid(z))).astype(BF16)
        return carry

    lax.fori_loop(0, S // rows, finish, 0)


def _attn(proj):
    B, _, S, _ = proj.shape

    def spec(base, g):
        return pl.BlockSpec((1, 1, S, LANES), lambda b, s: (b, base + g * ATT_SLOTS + s, 0, 0))

    in_specs = []
    for g in range(N_ATT_GROUPS):
        in_specs += [spec(BLK_Q, g), spec(BLK_K, g), spec(BLK_V, g)]
    in_specs.append(spec(BLK_ZA, 0))
    n_diag = sum(ATT_DILATIONS)
    n_band = N_ATT_GROUPS * S // ATT_BLOCK - n_diag
    return pl.pallas_call(
        _attn_kernel,
        grid=(B, ATT_SLOTS),
        in_specs=in_specs,
        out_specs=pl.BlockSpec((1, 1, S, LANES), lambda b, s: (b, s, 0, 0)),
        out_shape=jax.ShapeDtypeStruct((B, ATT_SLOTS, S, LANES), BF16),
        scratch_shapes=[pltpu.VMEM((n_band, ATT_BLOCK, 2 * ATT_BLOCK), F32),
                        pltpu.VMEM((n_diag, ATT_BLOCK, ATT_BLOCK), F32),
                        pltpu.VMEM((N_ATT_GROUPS, S, LANES), F32),
                        pltpu.VMEM((N_ATT_GROUPS, S, LANES), F32),
                        pltpu.VMEM((N_ATT_GROUPS, S, LANES), F32)],
        compiler_params=_params("arbitrary", "arbitrary"),
        name="attn",
    )(*([proj] * 10))


def _ret_decay_tables():
    H, C = RET_HEADS, RET_CHUNK
    log_g = np.log1p(-np.exp2(-5.0 - np.arange(H, dtype=np.float64)))
    idx = np.arange(C, dtype=np.float64)
    diff = idx[:, None] - idx[None, :]
    inner = np.where(diff >= 0, np.exp(log_g[:, None, None] * np.maximum(diff, 0.0)), 0.0)
    q_decay = np.exp(log_g[:, None] * (idx + 1.0))
    k_decay = np.exp(log_g[:, None] * (C - 1.0 - idx))
    chunk_decay = np.exp(log_g * C)
    f = lambda a: jnp.asarray(a, dtype=F32)
    return f(inner), f(q_decay[:, :, None]), f(k_decay[:, :, None]), f(chunk_decay)


def _ret_kernel(cd_ref, q_ref, k_ref, v_ref, z_ref, cos_ref, sin_ref, inner_ref, qd_ref, kd_ref,
                gnw_ref, y_ref, st_ref):
    S = y_ref.shape[2]
    C = RET_CHUNK
    half = RET_DIM // 2
    cd = cd_ref[pl.program_id(1)]
    k_scale = RET_DIM ** -0.5
    st_ref[...] = jnp.zeros_like(st_ref)

    def rot(t_ref, idx, cos, sin):
        t1 = t_ref[0, 0, idx, :].astype(F32)
        t2 = t_ref[0, 1, idx, :].astype(F32)
        return jnp.concatenate([t1 * cos - t2 * sin, t2 * cos + t1 * sin], axis=-1)

    def chunk(n, carry):
        idx = pl.ds(pl.multiple_of(n * C, C), C)
        cos, sin = cos_ref[0, idx, :], sin_ref[0, idx, :]
        qc = rot(q_ref, idx, cos, sin).astype(BF16)
        kf = rot(k_ref, idx, cos, sin) * k_scale
        kc = kf.astype(BF16)
        vc = jnp.concatenate([v_ref[0, 0, idx, :], v_ref[0, 1, idx, :]], axis=-1)
        state = st_ref[...]
        att = lax.dot_general(qc, kc, (((1,), (1,)), ((), ())), preferred_element_type=F32)
        att = (att * inner_ref[0]).astype(BF16)
        out = (jnp.dot(att, vc, preferred_element_type=F32)
               + jnp.dot(qc, state.astype(BF16), preferred_element_type=F32) * qd_ref[0])
        kdt = (kf * kd_ref[0]).T.astype(BF16)
        st_ref[...] = state * cd + jnp.dot(kdt, vc, preferred_element_type=F32)
        mu = jnp.mean(out, axis=-1, keepdims=True)
        cen = out - mu
        var = jnp.mean(cen * cen, axis=-1, keepdims=True)
        o = cen * lax.rsqrt(var + EPS) * gnw_ref[...]
        z = jnp.concatenate([z_ref[0, 0, idx, :], z_ref[0, 1, idx, :]], axis=-1).astype(F32)
        y = (o * (z * jax.nn.sigmoid(z))).astype(BF16)
        y_ref[0, 0, idx, :] = y[:, :half]
        y_ref[0, 1, idx, :] = y[:, half:]
        return carry

    lax.fori_loop(0, S // C, chunk, 0)


def _ret(proj, cos, sin, gn_w_l):
    B, _, S, _ = proj.shape
    inner, q_decay, k_decay, chunk_decay = _ret_decay_tables()
    per_head = RET_DIM // LANES

    def spec(base):
        return pl.BlockSpec((1, per_head, S, LANES), lambda b, h: (b, base // per_head + h, 0, 0))

    tab = pl.BlockSpec((1, S, LANES), lambda b, h: (b, 0, 0))
    return pl.pallas_call(
        _ret_kernel,
        grid=(B, RET_HEADS),
        in_specs=[
            pl.BlockSpec(memory_space=pltpu.SMEM),
            spec(BLK_QR), spec(BLK_KR), spec(BLK_VR), spec(BLK_ZR), tab, tab,
            pl.BlockSpec((1, RET_CHUNK, RET_CHUNK), lambda b, h: (h, 0, 0)),
            pl.BlockSpec((1, RET_CHUNK, 1), lambda b, h: (h, 0, 0)),
            pl.BlockSpec((1, RET_CHUNK, 1), lambda b, h: (h, 0, 0)),
            pl.BlockSpec((1, RET_DIM), lambda b, h: (0, h)),
        ],
        out_specs=pl.BlockSpec((1, per_head, S, LANES), lambda b, h: (b, h, 0, 0)),
        out_shape=jax.ShapeDtypeStruct((B, RET_HEADS * per_head, S, LANES), BF16),
        scratch_shapes=[pltpu.VMEM((RET_DIM, RET_DIM), F32)],
        compiler_params=_params("arbitrary", "arbitrary"),
        name="retention",
    )(chunk_decay, proj, proj, proj, proj, cos, sin, inner, q_decay, k_decay,
      gn_w_l.reshape(1, RET_HEADS * RET_DIM))


def _outproj_kernel(ya_ref, yr_ref, ga_ref, gr_ref, x_ref, gate_ref, wpa_ref, wpr_ref, wo_ref,
                    fnw_ref, o_ref, *, final_norm):
    def cat(ref):
        return jnp.concatenate([ref[0, i] for i in range(ref.shape[1])], axis=-1)

    a = jnp.dot(cat(ya_ref), wpa_ref[...], preferred_element_type=F32)
    r = jnp.dot(cat(yr_ref), wpr_ref[...], preferred_element_type=F32)
    merged = (jax.nn.sigmoid(cat(ga_ref).astype(F32)) * a
              + jax.nn.sigmoid(cat(gr_ref).astype(F32)) * r)
    out = x_ref[0] + gate_ref[0] * jnp.dot(merged.astype(BF16), wo_ref[...],
                                           preferred_element_type=F32)
    if final_norm:
        ms = jnp.mean(out * out, axis=-1, keepdims=True)
        out = out * lax.rsqrt(ms + EPS) * fnw_ref[...]
    o_ref[0] = out


def _outproj(x, ya, yr, proj, mod_l, wpa, wpr, wo, final_norm_w, final_norm):
    B, S, D = x.shape
    tm = 512
    n_g = D // LANES
    return pl.pallas_call(
        functools.partial(_outproj_kernel, final_norm=final_norm),
        grid=(B, S // tm),
        in_specs=[
            pl.BlockSpec((1, ya.shape[1], tm, LANES), lambda b, i: (b, 0, i, 0)),
            pl.BlockSpec((1, yr.shape[1], tm, LANES), lambda b, i: (b, 0, i, 0)),
            pl.BlockSpec((1, n_g, tm, LANES), lambda b, i: (b, BLK_GA // n_g, i, 0)),
            pl.BlockSpec((1, n_g, tm, LANES), lambda b, i: (b, BLK_GR // n_g, i, 0)),
            pl.BlockSpec((1, tm, D), lambda b, i: (b, i, 0)),
            pl.BlockSpec((1, 1, D), lambda b, i: (b, 0, 2)),
            pl.BlockSpec(wpa.shape, lambda b, i: (0, 0)),
            pl.BlockSpec(wpr.shape, lambda b, i: (0, 0)),
            pl.BlockSpec(wo.shape, lambda b, i: (0, 0)),
            pl.BlockSpec((1, D), lambda b, i: (0, 0)),
        ],
        out_specs=pl.BlockSpec((1, tm, D), lambda b, i: (b, i, 0)),
        out_shape=jax.ShapeDtypeStruct((B, S, D), F32),
        compiler_params=_params("arbitrary", "arbitrary"),
        name="outproj",
    )(ya, yr, proj, proj, x, mod_l, wpa, wpr, wo, final_norm_w.reshape(1, D))


def kernel(x, c, positions, norm_w, w_ada, b_ada, w_in, ret_gn_w, w_proj_attn, w_proj_ret, w_out,
           final_norm_w):
    B = x.shape[0]
    mod = _ada(c, w_ada, b_ada).reshape(DEPTH, B, 1, 3 * D_MODEL)
    cos, sin = _rope_tables(positions)
    w_in_b = w_in.astype(BF16)
    wpa_b = w_proj_attn.astype(BF16)
    wpr_b = w_proj_ret.astype(BF16)
    wo_b = w_out.astype(BF16)
    for l in range(DEPTH):
        proj = _inproj(x, mod[l], norm_w[l], w_in_b[l])
        ya = _attn(proj)
        yr = _ret(proj, cos, sin, ret_gn_w[l])
        x = _outproj(x, ya, yr, proj, mod[l], wpa_b[l], wpr_b[l], wo_b[l], final_norm_w,
                     final_norm=(l == DEPTH - 1))
    return x
```

```python
import functools

import numpy as np
import jax
import jax.numpy as jnp
from jax import lax
from jax.experimental import pallas as pl
from jax.experimental.pallas import tpu as pltpu

D_MODEL = 1024
SEQ = 2048
DEPTH = 4
ATT_DILATIONS = (1, 4, 16)
N_ATT_GROUPS = 3
ATT_SLOTS = 4
ATT_HEAD_DIM = 128
ATT_BLOCK = 128
RET_HEADS = 4
RET_DIM = 256
RET_CHUNK = 128
ROPE_BASE = 10000.0
EPS = 1e-6
NEG_INF = -1e30

LANES = 128
COL_BLOCK = 512
BLOCKS_PER_STEP = COL_BLOCK // LANES
DOTS_PER_STEP = 2
ATT_QKV = N_ATT_GROUPS * ATT_SLOTS * ATT_HEAD_DIM
IN_WIDTH = 3 * ATT_QKV + ATT_SLOTS * ATT_HEAD_DIM + 4 * RET_HEADS * RET_DIM + 2 * D_MODEL
N_COL_STEPS = IN_WIDTH // COL_BLOCK
N_BLOCKS = IN_WIDTH // LANES


def _column_segments():
    grp = ATT_SLOTS * ATT_HEAD_DIM
    qkv = lambda t, g: (t * ATT_QKV + g * grp, grp)
    seg = [qkv(0, 0), qkv(1, 0), qkv(2, 0), (3 * ATT_QKV, IN_WIDTH - 3 * ATT_QKV)]
    for g in range(1, N_ATT_GROUPS):
        seg += [qkv(0, g), qkv(1, g), qkv(2, g)]
    return seg


_GRP_BLOCKS = ATT_SLOTS * ATT_HEAD_DIM // LANES
_WIDE_BLOCKS = RET_HEADS * RET_DIM // LANES
BLK_ZA = 3 * _GRP_BLOCKS
BLK_QR = BLK_ZA + _GRP_BLOCKS
BLK_KR = BLK_QR + _WIDE_BLOCKS
BLK_VR = BLK_KR + _WIDE_BLOCKS
BLK_ZR = BLK_VR + _WIDE_BLOCKS
BLK_GA = BLK_ZR + _WIDE_BLOCKS
BLK_GR = BLK_GA + D_MODEL // LANES
_BLK_G1 = BLK_GR + D_MODEL // LANES
BLK_Q = (0,) + tuple(_BLK_G1 + (g - 1) * 3 * _GRP_BLOCKS for g in range(1, N_ATT_GROUPS))
BLK_K = tuple(b + _GRP_BLOCKS for b in BLK_Q)
BLK_V = tuple(b + 2 * _GRP_BLOCKS for b in BLK_Q)
N_NATURAL_STEPS = _BLK_G1 // BLOCKS_PER_STEP
STEPS_PER_GROUP = 3 * _GRP_BLOCKS // BLOCKS_PER_STEP

VMEM_LIMIT = 60 * 1024 * 1024
F32 = jnp.float32
BF16 = jnp.bfloat16


def _params(*sem):
    return pltpu.CompilerParams(dimension_semantics=sem, vmem_limit_bytes=VMEM_LIMIT)


def _ada_kernel(c_ref, w_ref, b_ref, o_ref):
    c = c_ref[...]
    c_act = (c * jax.nn.sigmoid(c)).astype(BF16)
    acc = jnp.dot(c_act, w_ref[0].astype(BF16), preferred_element_type=F32)
    o_ref[0] = acc + b_ref[0]


def _ada(c, w_ada, b_ada):
    B = c.shape[0]
    n_col = 3 * D_MODEL // D_MODEL
    return pl.pallas_call(
        _ada_kernel,
        grid=(DEPTH, n_col),
        in_specs=[
            pl.BlockSpec((B, D_MODEL), lambda l, j: (0, 0)),
            pl.BlockSpec((1, D_MODEL, D_MODEL), lambda l, j: (l, 0, j)),
            pl.BlockSpec((1, 1, D_MODEL), lambda l, j: (l, 0, j)),
        ],
        out_specs=pl.BlockSpec((1, B, D_MODEL), lambda l, j: (l, 0, j)),
        out_shape=jax.ShapeDtypeStruct((DEPTH, B, 3 * D_MODEL), F32),
        compiler_params=_params("arbitrary", "arbitrary"),
        name="ada",
    )(c, w_ada, b_ada.reshape(DEPTH, 1, 3 * D_MODEL))


def _rope_kernel(pos_ref, theta_ref, cos_ref, sin_ref):
    ang = pos_ref[0].astype(F32) * theta_ref[...]
    cos_ref[0] = jnp.cos(ang)
    sin_ref[0] = jnp.sin(ang)


def _rope_tables(positions):
    B, S = positions.shape
    half = RET_DIM // 2
    theta = ROPE_BASE ** (-jnp.arange(half, dtype=F32) / half)
    spec = pl.BlockSpec((1, S, half), lambda b: (b, 0, 0))
    return pl.pallas_call(
        _rope_kernel,
        grid=(B,),
        in_specs=[pl.BlockSpec((1, S, 1), lambda b: (b, 0, 0)),
                  pl.BlockSpec((1, half), lambda b: (0, 0))],
        out_specs=[spec, spec],
        out_shape=[jax.ShapeDtypeStruct((B, S, half), F32)] * 2,
        compiler_params=_params("arbitrary"),
        name="rope",
    )(positions.reshape(B, S, 1), theta.reshape(1, half))


def _row_order_of_column_block(cb):
    return 0 if cb < N_NATURAL_STEPS else 1 + (cb - N_NATURAL_STEPS) // STEPS_PER_GROUP


def _inproj_kernel(x_ref, mod_ref, nw_ref, w_ref, o_ref, h0_ref, h1_ref, h2_ref, hn_ref):
    j = pl.program_id(1)
    S = x_ref.shape[1]
    rows = 128
    n_lane_blocks = D_MODEL // LANES
    h_refs = (h0_ref, h1_ref, h2_ref)

    def project(h_ref, i):
        res = jnp.dot(h_ref[...], w_ref[:, i * COL_BLOCK:(i + 1) * COL_BLOCK],
                      preferred_element_type=F32)
        for cblk in range(BLOCKS_PER_STEP):
            o_ref[0, i * BLOCKS_PER_STEP + cblk] = (
                res[:, cblk * LANES:(cblk + 1) * LANES].astype(BF16))

    @pl.when(j == 0)
    def _():
        shift = mod_ref[0, :, 0:D_MODEL]
        scale = mod_ref[0, :, D_MODEL:2 * D_MODEL]
        wmul = nw_ref[...] * (1.0 + scale)

        def natural(c, carry):
            idx = pl.ds(pl.multiple_of(c * rows, rows), rows)
            xs = x_ref[0, idx, :]
            ms = jnp.mean(xs * xs, axis=-1, keepdims=True)
            hv = xs * lax.rsqrt(ms + EPS) * wmul + shift
            h0_ref[idx, :] = hv.astype(BF16)
            for cb in range(n_lane_blocks):
                hn_ref[cb, idx, :] = hv[:, cb * LANES:(cb + 1) * LANES]
            return carry

        lax.fori_loop(0, S // rows, natural, 0, unroll=2)

        def permuted(c, carry):
            p0 = pl.multiple_of(c * rows, rows)
            for h_ref, d in zip(h_refs[1:], ATT_DILATIONS[1:]):
                sub_len = S // d
                start = (p0 % sub_len) * d + p0 // sub_len
                hv = jnp.concatenate(
                    [hn_ref[cb, pl.ds(start, rows, stride=d), :] for cb in range(n_lane_blocks)],
                    axis=-1)
                h_ref[pl.ds(p0, rows), :] = hv.astype(BF16)
            return carry

        lax.fori_loop(0, S // rows, permuted, 0, unroll=2)

    steps_by_orders = {}
    for s in range(N_COL_STEPS // DOTS_PER_STEP):
        orders = tuple(_row_order_of_column_block(s * DOTS_PER_STEP + i)
                       for i in range(DOTS_PER_STEP))
        steps_by_orders.setdefault(orders, []).append(s)
    for orders, steps in steps_by_orders.items():
        @pl.when((j >= steps[0]) & (j <= steps[-1]))
        def _(orders=orders):
            for i, order in enumerate(orders):
                project(h_refs[order], i)


def _inproj(x, mod_l, norm_w_l, w_in_l):
    B, S, D = x.shape
    return pl.pallas_call(
        _inproj_kernel,
        grid=(B, N_COL_STEPS // DOTS_PER_STEP),
        in_specs=[
            pl.BlockSpec((1, S, D), lambda b, j: (b, 0, 0)),
            pl.BlockSpec((1, 1, 3 * D), lambda b, j: (b, 0, 0)),
            pl.BlockSpec((1, D), lambda b, j: (0, 0)),
            pl.BlockSpec((D, DOTS_PER_STEP * COL_BLOCK), lambda b, j: (0, j)),
        ],
        out_specs=pl.BlockSpec((1, DOTS_PER_STEP * BLOCKS_PER_STEP, S, LANES),
                               lambda b, j: (b, j, 0, 0)),
        out_shape=jax.ShapeDtypeStruct((B, N_BLOCKS, S, LANES), BF16),
        scratch_shapes=[pltpu.VMEM((S, D), BF16)] * N_ATT_GROUPS
                       + [pltpu.VMEM((D // LANES, S, LANES), F32)],
        compiler_params=_params("arbitrary", "arbitrary"),
        name="inproj",
    )(x, mod_l, norm_w_l.reshape(1, D), w_in_l)


def _attn_kernel(q0, k0, v0, q1, k1, v1, q2, k2, v2, z_ref, y_ref,
                 sb_sc, sd_sc, m_sc, acc_sc, l_sc):
    S = y_ref.shape[2]
    T = ATT_BLOCK
    c_exp2 = ATT_HEAD_DIM ** -0.5 * np.log2(np.e)
    row = lax.broadcasted_iota(jnp.int32, (T, 2 * T), 0)
    col = lax.broadcasted_iota(jnp.int32, (T, 2 * T), 1)
    mask_band = (col >= row) & (col <= row + T)
    mask_diag = (lax.broadcasted_iota(jnp.int32, (T, T), 1)
                 <= lax.broadcasted_iota(jnp.int32, (T, T), 0))
    groups = ((q0, k0, v0), (q1, k1, v1), (q2, k2, v2))
    d1, d2 = ATT_DILATIONS[1], ATT_DILATIONS[2]
    len1 = S // d1
    nb1 = len1 // T
    n_band0 = S // T - 1

    def scores(g, q_start, band, blk, out_idx):
        q_ref, k_ref, _ = groups[g]
        q = q_ref[0, 0, pl.ds(q_start, T), :]
        if band:
            k = k_ref[0, 0, pl.ds(q_start - T, 2 * T), :]
        else:
            k = k_ref[0, 0, pl.ds(q_start, T), :]
        s = lax.dot_general(q, k, (((1,), (1,)), ((), ())), preferred_element_type=F32) * c_exp2
        s = jnp.where(mask_band if band else mask_diag, s, NEG_INF)
        (sb_sc if band else sd_sc)[blk] = s
        m_sc[g, out_idx, :] = jnp.broadcast_to(jnp.max(s, axis=-1, keepdims=True), (T, LANES))

    def weighted_values(g, q_start, band, blk, out_idx):
        v_ref = groups[g][2]
        m = m_sc[0, out_idx, :]
        if band:
            p = jnp.exp2(sb_sc[blk] - jnp.concatenate([m, m], axis=-1))
            v = v_ref[0, 0, pl.ds(q_start - T, 2 * T), :]
        else:
            p = jnp.exp2(sd_sc[blk] - m)
            v = v_ref[0, 0, pl.ds(q_start, T), :]
        v1 = jnp.concatenate([v, jnp.ones_like(v)], axis=-1)
        r = jnp.dot(p.astype(BF16), v1, preferred_element_type=F32)
        acc_sc[g, out_idx, :] = r[:, :LANES]
        l_sc[g, out_idx, :] = r[:, LANES:]

    def all_blocks(fn):
        fn(0, 0, False, 0, pl.ds(0, T))
        for i in range(n_band0):
            fn(0, (i + 1) * T, True, i, pl.ds((i + 1) * T, T))
        for r in range(d1):
            fn(1, r * len1, False, 1 + r, pl.ds(r, T, stride=d1))
            for n in range(1, nb1):
                fn(1, r * len1 + n * T, True, n_band0 + r * (nb1 - 1) + n - 1,
                   pl.ds(n * T * d1 + r, T, stride=d1))
        for r in range(d2):
            fn(2, r * T, False, 1 + d1 + r, pl.ds(r, T, stride=d2))

    all_blocks(scores)

    rows = 256

    def shared_max(c, carry):
        idx = pl.ds(pl.multiple_of(c * rows, rows), rows)
        m_sc[0, idx, :] = jnp.maximum(jnp.maximum(m_sc[0, idx, :], m_sc[1, idx, :]), m_sc[2, idx, :])
        return carry

    lax.fori_loop(0, S // rows, shared_max, 0)

    all_blocks(weighted_values)

    def finish(c, carry):
        idx = pl.ds(pl.multiple_of(c * rows, rows), rows)
        z = z_ref[0, 0, idx, :].astype(F32)
        o = ((acc_sc[0, idx, :] + acc_sc[1, idx, :] + acc_sc[2, idx, :])
             / (l_sc[0, idx, :] + l_sc[1, idx, :] + l_sc[2, idx, :]))
        y_ref[0, 0, idx, :] = (o * (z * jax.nn.sigmoid(z))).astype(BF16)
        return carry

    lax.fori_loop(0, S // rows, finish, 0)


def _attn(proj):
    B, _, S, _ = proj.shape

    def spec(base):
        return pl.BlockSpec((1, 1, S, LANES), lambda b, s: (b, base + s, 0, 0))

    in_specs = []
    for g in range(N_ATT_GROUPS):
        in_specs += [spec(BLK_Q[g]), spec(BLK_K[g]), spec(BLK_V[g])]
    in_specs.append(spec(BLK_ZA))
    n_diag = sum(ATT_DILATIONS)
    n_band = N_ATT_GROUPS * S // ATT_BLOCK - n_diag
    return pl.pallas_call(
        _attn_kernel,
        grid=(B, ATT_SLOTS),
        in_specs=in_specs,
        out_specs=pl.BlockSpec((1, 1, S, LANES), lambda b, s: (b, s, 0, 0)),
        out_shape=jax.ShapeDtypeStruct((B, ATT_SLOTS, S, LANES), BF16),
        scratch_shapes=[pltpu.VMEM((n_band, ATT_BLOCK, 2 * ATT_BLOCK), F32),
                        pltpu.VMEM((n_diag, ATT_BLOCK, ATT_BLOCK), F32),
                        pltpu.VMEM((N_ATT_GROUPS, S, LANES), F32),
                        pltpu.VMEM((N_ATT_GROUPS, S, LANES), F32),
                        pltpu.VMEM((N_ATT_GROUPS, S, LANES), F32)],
        compiler_params=_params("arbitrary", "arbitrary"),
        name="attn",
    )(*([proj] * 10))


def _ret_decay_tables():
    H, C = RET_HEADS, RET_CHUNK
    log_g = np.log1p(-np.exp2(-5.0 - np.arange(H, dtype=np.float64)))
    idx = np.arange(C, dtype=np.float64)
    diff = idx[:, None] - idx[None, :]
    inner = np.where(diff >= 0, np.exp(log_g[:, None, None] * np.maximum(diff, 0.0)), 0.0)
    q_decay = np.exp(log_g[:, None] * (idx + 1.0))
    k_decay = np.exp(log_g[:, None] * (C - 1.0 - idx))
    chunk_decay = np.exp(log_g * C)
    f = lambda a: jnp.asarray(a, dtype=F32)
    return f(inner), f(q_decay[:, :, None]), f(k_decay[:, :, None]), f(chunk_decay)


def _ret_kernel(cd_ref, q_ref, k_ref, v_ref, z_ref, cos_ref, sin_ref, inner_ref, qd_ref, kd_ref,
                gnw_ref, y_ref, qr_sc, intra_sc, u_sc, st_sc):
    S = y_ref.shape[2]
    C = RET_CHUNK
    n_chunks = S // C
    half = RET_DIM // 2
    cd = cd_ref[pl.program_id(1)]
    k_scale = RET_DIM ** -0.5

    def rot(t_ref, idx, cos, sin):
        t1 = t_ref[0, 0, idx, :].astype(F32)
        t2 = t_ref[0, 1, idx, :].astype(F32)
        return jnp.concatenate([t1 * cos - t2 * sin, t2 * cos + t1 * sin], axis=-1)

    for n in range(n_chunks):
        idx = pl.ds(n * C, C)
        cos, sin = cos_ref[0, idx, :], sin_ref[0, idx, :]
        qc = rot(q_ref, idx, cos, sin).astype(BF16)
        kf = rot(k_ref, idx, cos, sin) * k_scale
        vc = jnp.concatenate([v_ref[0, 0, idx, :], v_ref[0, 1, idx, :]], axis=-1)
        att = lax.dot_general(qc, kf.astype(BF16), (((1,), (1,)), ((), ())),
                              preferred_element_type=F32)
        att = (att * inner_ref[0]).astype(BF16)
        qr_sc[idx, :] = qc
        intra_sc[idx, :] = jnp.dot(att, vc, preferred_element_type=F32)
        kdt = (kf * kd_ref[0]).T.astype(BF16)
        u_sc[n] = jnp.dot(kdt, vc, preferred_element_type=F32)

    slab = 64
    for r0 in range(0, RET_DIM, slab):
        state = jnp.zeros((slab, RET_DIM), F32)
        for n in range(n_chunks):
            st_sc[n, r0:r0 + slab, :] = state.astype(BF16)
            state = state * cd + u_sc[n, r0:r0 + slab, :]

    for n in range(n_chunks):
        idx = pl.ds(n * C, C)
        cross = jnp.dot(qr_sc[idx, :], st_sc[n], preferred_element_type=F32) * qd_ref[0]
        out = intra_sc[idx, :] + cross
        mu = jnp.mean(out, axis=-1, keepdims=True)
        cen = out - mu
        var = jnp.mean(cen * cen, axis=-1, keepdims=True)
        o = cen * lax.rsqrt(var + EPS) * gnw_ref[...]
        z = jnp.concatenate([z_ref[0, 0, idx, :], z_ref[0, 1, idx, :]], axis=-1).astype(F32)
        y = (o * (z * jax.nn.sigmoid(z))).astype(BF16)
        y_ref[0, 0, idx, :] = y[:, :half]
        y_ref[0, 1, idx, :] = y[:, half:]


def _ret(proj, cos, sin, gn_w_l):
    B, _, S, _ = proj.shape
    inner, q_decay, k_decay, chunk_decay = _ret_decay_tables()
    per_head = RET_DIM // LANES

    def spec(base):
        return pl.BlockSpec((1, per_head, S, LANES), lambda b, h: (b, base // per_head + h, 0, 0))

    tab = pl.BlockSpec((1, S, LANES), lambda b, h: (b, 0, 0))
    return pl.pallas_call(
        _ret_kernel,
        grid=(B, RET_HEADS),
        in_specs=[
            pl.BlockSpec(memory_space=pltpu.SMEM),
            spec(BLK_QR), spec(BLK_KR), spec(BLK_VR), spec(BLK_ZR), tab, tab,
            pl.BlockSpec((1, RET_CHUNK, RET_CHUNK), lambda b, h: (h, 0, 0)),
            pl.BlockSpec((1, RET_CHUNK, 1), lambda b, h: (h, 0, 0)),
            pl.BlockSpec((1, RET_CHUNK, 1), lambda b, h: (h, 0, 0)),
            pl.BlockSpec((1, RET_DIM), lambda b, h: (0, h)),
        ],
        out_specs=pl.BlockSpec((1, per_head, S, LANES), lambda b, h: (b, h, 0, 0)),
        out_shape=jax.ShapeDtypeStruct((B, RET_HEADS * per_head, S, LANES), BF16),
        scratch_shapes=[pltpu.VMEM((S, RET_DIM), BF16),
                        pltpu.VMEM((S, RET_DIM), F32),
                        pltpu.VMEM((S // RET_CHUNK, RET_DIM, RET_DIM), F32),
                        pltpu.VMEM((S // RET_CHUNK, RET_DIM, RET_DIM), BF16)],
        compiler_params=_params("arbitrary", "arbitrary"),
        name="retention",
    )(chunk_decay, proj, proj, proj, proj, cos, sin, inner, q_decay, k_decay,
      gn_w_l.reshape(1, RET_HEADS * RET_DIM))


def _outproj_kernel(ya_ref, yr_ref, ga_ref, gr_ref, x_ref, gate_ref, wpa_ref, wpr_ref, wo_ref,
                    fnw_ref, o_ref, *, final_norm):
    def cat(ref):
        return jnp.concatenate([ref[0, i] for i in range(ref.shape[1])], axis=-1)

    a = jnp.dot(cat(ya_ref), wpa_ref[...], preferred_element_type=F32)
    r = jnp.dot(cat(yr_ref), wpr_ref[...], preferred_element_type=F32)
    merged = (jax.nn.sigmoid(cat(ga_ref).astype(F32)) * a
              + jax.nn.sigmoid(cat(gr_ref).astype(F32)) * r)
    out = x_ref[0] + gate_ref[0] * jnp.dot(merged.astype(BF16), wo_ref[...],
                                           preferred_element_type=F32)
    if final_norm:
        ms = jnp.mean(out * out, axis=-1, keepdims=True)
        out = out * lax.rsqrt(ms + EPS) * fnw_ref[...]
    o_ref[0] = out


def _outproj(x, ya, yr, proj, mod_l, wpa, wpr, wo, final_norm_w, final_norm):
    B, S, D = x.shape
    tm = 512
    n_g = D // LANES
    return pl.pallas_call(
        functools.partial(_outproj_kernel, final_norm=final_norm),
        grid=(B, S // tm),
        in_specs=[
            pl.BlockSpec((1, ya.shape[1], tm, LANES), lambda b, i: (b, 0, i, 0)),
            pl.BlockSpec((1, yr.shape[1], tm, LANES), lambda b, i: (b, 0, i, 0)),
            pl.BlockSpec((1, n_g, tm, LANES), lambda b, i: (b, BLK_GA // n_g, i, 0)),
            pl.BlockSpec((1, n_g, tm, LANES), lambda b, i: (b, BLK_GR // n_g, i, 0)),
            pl.BlockSpec((1, tm, D), lambda b, i: (b, i, 0)),
            pl.BlockSpec((1, 1, D), lambda b, i: (b, 0, 2)),
            pl.BlockSpec(wpa.shape, lambda b, i: (0, 0)),
            pl.BlockSpec(wpr.shape, lambda b, i: (0, 0)),
            pl.BlockSpec(wo.shape, lambda b, i: (0, 0)),
            pl.BlockSpec((1, D), lambda b, i: (0, 0)),
        ],
        out_specs=pl.BlockSpec((1, tm, D), lambda b, i: (b, i, 0)),
        out_shape=jax.ShapeDtypeStruct((B, S, D), F32),
        compiler_params=_params("arbitrary", "arbitrary"),
        name="outproj",
    )(ya, yr, proj, proj, x, mod_l, wpa, wpr, wo, final_norm_w.reshape(1, D))


def kernel(x, c, positions, norm_w, w_ada, b_ada, w_in, ret_gn_w, w_proj_attn, w_proj_ret, w_out,
           final_norm_w):
    B = x.shape[0]
    mod = _ada(c, w_ada, b_ada).reshape(DEPTH, B, 1, 3 * D_MODEL)
    cos, sin = _rope_tables(positions)
    w_in_b = jnp.concatenate([w_in[:, :, s:s + n] for s, n in _column_segments()],
                             axis=-1).astype(BF16)
    wpa_b = w_proj_attn.astype(BF16)
    wpr_b = w_proj_ret.astype(BF16)
    wo_b = w_out.astype(BF16)
    for l in range(DEPTH):
        proj = _inproj(x, mod[l], norm_w[l], w_in_b[l])
        ya = _attn(proj)
        yr = _ret(proj, cos, sin, ret_gn_w[l])
        x = _outproj(x, ya, yr, proj, mod[l], wpa_b[l], wpr_b[l], wo_b[l], final_norm_w,
                     final_norm=(l == DEPTH - 1))
    return x
```

```python
import functools

import numpy as np
import jax
import jax.numpy as jnp
from jax import lax
from jax.experimental import pallas as pl
from jax.experimental.pallas import tpu as pltpu

D_MODEL = 1024
SEQ = 2048
DEPTH = 4
ATT_DILATIONS = (1, 4, 16)
N_ATT_GROUPS = 3
ATT_SLOTS = 4
ATT_HEAD_DIM = 128
ATT_BLOCK = 128
RET_HEADS = 4
RET_DIM = 256
RET_CHUNK = 128
RET_HEADS_PER_STEP = 2
ROPE_BASE = 10000.0
EPS = 1e-6
NEG_INF = -1e30

LANES = 128
COL_BLOCK = 512
BLOCKS_PER_STEP = COL_BLOCK // LANES
DOTS_PER_STEP = 2
ATT_QKV = N_ATT_GROUPS * ATT_SLOTS * ATT_HEAD_DIM
IN_WIDTH = 3 * ATT_QKV + ATT_SLOTS * ATT_HEAD_DIM + 4 * RET_HEADS * RET_DIM + 2 * D_MODEL
N_COL_STEPS = IN_WIDTH // COL_BLOCK
N_BLOCKS = IN_WIDTH // LANES


def _column_segments():
    grp = ATT_SLOTS * ATT_HEAD_DIM
    qkv = lambda t, g: (t * ATT_QKV + g * grp, grp)
    seg = [qkv(0, 0), qkv(1, 0), qkv(2, 0), (3 * ATT_QKV, IN_WIDTH - 3 * ATT_QKV)]
    for g in range(1, N_ATT_GROUPS):
        seg += [qkv(0, g), qkv(1, g), qkv(2, g)]
    return seg


_GRP_BLOCKS = ATT_SLOTS * ATT_HEAD_DIM // LANES
_WIDE_BLOCKS = RET_HEADS * RET_DIM // LANES
BLK_ZA = 3 * _GRP_BLOCKS
BLK_QR = BLK_ZA + _GRP_BLOCKS
BLK_KR = BLK_QR + _WIDE_BLOCKS
BLK_VR = BLK_KR + _WIDE_BLOCKS
BLK_ZR = BLK_VR + _WIDE_BLOCKS
BLK_GA = BLK_ZR + _WIDE_BLOCKS
BLK_GR = BLK_GA + D_MODEL // LANES
_BLK_G1 = BLK_GR + D_MODEL // LANES
BLK_Q = (0,) + tuple(_BLK_G1 + (g - 1) * 3 * _GRP_BLOCKS for g in range(1, N_ATT_GROUPS))
BLK_K = tuple(b + _GRP_BLOCKS for b in BLK_Q)
BLK_V = tuple(b + 2 * _GRP_BLOCKS for b in BLK_Q)
N_NATURAL_STEPS = _BLK_G1 // BLOCKS_PER_STEP
STEPS_PER_GROUP = 3 * _GRP_BLOCKS // BLOCKS_PER_STEP
N_NATURAL_GRID_STEPS = N_NATURAL_STEPS // DOTS_PER_STEP
PERMUTED_ROWS_PER_STEP = SEQ // N_NATURAL_GRID_STEPS

VMEM_LIMIT = 60 * 1024 * 1024
F32 = jnp.float32
BF16 = jnp.bfloat16


def _params(*sem):
    return pltpu.CompilerParams(dimension_semantics=sem, vmem_limit_bytes=VMEM_LIMIT)


def _ada_kernel(c_ref, w_ref, b_ref, o_ref):
    c = c_ref[...]
    c_act = (c * jax.nn.sigmoid(c)).astype(BF16)
    acc = jnp.dot(c_act, w_ref[0].astype(BF16), preferred_element_type=F32)
    o_ref[0] = acc + b_ref[0]


def _ada(c, w_ada, b_ada):
    B = c.shape[0]
    n_col = 3 * D_MODEL // D_MODEL
    return pl.pallas_call(
        _ada_kernel,
        grid=(DEPTH, n_col),
        in_specs=[
            pl.BlockSpec((B, D_MODEL), lambda l, j: (0, 0)),
            pl.BlockSpec((1, D_MODEL, D_MODEL), lambda l, j: (l, 0, j)),
            pl.BlockSpec((1, 1, D_MODEL), lambda l, j: (l, 0, j)),
        ],
        out_specs=pl.BlockSpec((1, B, D_MODEL), lambda l, j: (l, 0, j)),
        out_shape=jax.ShapeDtypeStruct((DEPTH, B, 3 * D_MODEL), F32),
        compiler_params=_params("arbitrary", "arbitrary"),
        name="ada",
    )(c, w_ada, b_ada.reshape(DEPTH, 1, 3 * D_MODEL))


def _rope_kernel(pos_ref, theta_ref, cos_ref, sin_ref):
    ang = pos_ref[0].astype(F32) * theta_ref[...]
    cos_ref[0] = jnp.cos(ang)
    sin_ref[0] = jnp.sin(ang)


def _rope_tables(positions):
    B, S = positions.shape
    half = RET_DIM // 2
    theta = ROPE_BASE ** (-jnp.arange(half, dtype=F32) / half)
    spec = pl.BlockSpec((1, S, half), lambda b: (b, 0, 0))
    return pl.pallas_call(
        _rope_kernel,
        grid=(B,),
        in_specs=[pl.BlockSpec((1, S, 1), lambda b: (b, 0, 0)),
                  pl.BlockSpec((1, half), lambda b: (0, 0))],
        out_specs=[spec, spec],
        out_shape=[jax.ShapeDtypeStruct((B, S, half), F32)] * 2,
        compiler_params=_params("arbitrary"),
        name="rope",
    )(positions.reshape(B, S, 1), theta.reshape(1, half))


def _row_order_of_column_block(cb):
    return 0 if cb < N_NATURAL_STEPS else 1 + (cb - N_NATURAL_STEPS) // STEPS_PER_GROUP


def _rotary_scale_of_column_block(cb):
    blk = cb * BLOCKS_PER_STEP
    if BLK_QR <= blk < BLK_KR:
        return 1.0
    if BLK_KR <= blk < BLK_VR:
        return RET_DIM ** -0.5
    return None


def _inproj_kernel(x_ref, x1_ref, x2_ref, mod_ref, nw_ref, w_ref, cos_ref, sin_ref, o_ref,
                   h0_ref, h1_ref, h2_ref, rstd_ref):
    j = pl.program_id(1)
    S = x_ref.shape[1]
    rows = 128
    n_lane_blocks = D_MODEL // LANES
    h_refs = (h0_ref, h1_ref, h2_ref)
    half = RET_DIM // 2
    shift = mod_ref[0, :, 0:D_MODEL]
    wmul = nw_ref[...] * (1.0 + mod_ref[0, :, D_MODEL:2 * D_MODEL])

    def project(h_ref, i, rotary_scale):
        res = jnp.dot(h_ref[...], w_ref[:, i * COL_BLOCK:(i + 1) * COL_BLOCK],
                      preferred_element_type=F32)
        blocks = [res[:, c * LANES:(c + 1) * LANES] for c in range(BLOCKS_PER_STEP)]
        if rotary_scale is not None:
            cos, sin = cos_ref[0], sin_ref[0]
            if rotary_scale != 1.0:
                cos, sin = cos * rotary_scale, sin * rotary_scale
            for c in range(0, BLOCKS_PER_STEP, RET_DIM // LANES):
                t1, t2 = blocks[c], blocks[c + half // LANES]
                blocks[c] = t1 * cos - t2 * sin
                blocks[c + half // LANES] = t2 * cos + t1 * sin
        for c, blk in enumerate(blocks):
            o_ref[0, i * BLOCKS_PER_STEP + c] = blk.astype(BF16)

    @pl.when(j == 0)
    def _():
        def natural(c, carry):
            idx = pl.ds(pl.multiple_of(c * rows, rows), rows)
            xs = x_ref[0, idx, :]
            rstd = lax.rsqrt(jnp.mean(xs * xs, axis=-1, keepdims=True) + EPS)
            rstd_ref[idx, :] = jnp.broadcast_to(rstd, (rows, LANES))
            h0_ref[idx, :] = (xs * rstd * wmul + shift).astype(BF16)
            return carry

        lax.fori_loop(0, S // rows, natural, 0, unroll=2)

    @pl.when(j < N_NATURAL_GRID_STEPS)
    def _():
        p0 = j * PERMUTED_ROWS_PER_STEP
        for h_ref, xp_ref, d in zip(h_refs[1:], (x1_ref, x2_ref), ATT_DILATIONS[1:]):
            sub_len = S // d
            run = min(sub_len, PERMUTED_ROWS_PER_STEP)
            for i in range(PERMUTED_ROWS_PER_STEP // run):
                q0 = p0 + i * run
                start = (q0 % sub_len) * d + q0 // sub_len
                rstd = rstd_ref[pl.ds(start, run, stride=d), :]
                xs = xp_ref[0, :, i * D_MODEL:(i + 1) * D_MODEL]
                hv = xs * jnp.concatenate([rstd] * n_lane_blocks, axis=-1) * wmul + shift
                h_ref[pl.ds(pl.multiple_of(q0, run), run), :] = hv.astype(BF16)

    steps_by_kind = {}
    for s in range(N_COL_STEPS // DOTS_PER_STEP):
        cbs = [s * DOTS_PER_STEP + i for i in range(DOTS_PER_STEP)]
        kind = tuple((_row_order_of_column_block(cb), _rotary_scale_of_column_block(cb))
                     for cb in cbs)
        steps_by_kind.setdefault(kind, []).append(s)
    for kind, steps in steps_by_kind.items():
        @pl.when(functools.reduce(jnp.logical_or, [j == s for s in steps]))
        def _(kind=kind):
            for i, (order, rotary_scale) in enumerate(kind):
                project(h_refs[order], i, rotary_scale)


def _inproj(x, mod_l, norm_w_l, w_in_l, cos, sin):
    B, S, D = x.shape
    tab = pl.BlockSpec((1, S, LANES), lambda b, j: (b, 0, 0))

    def permuted(d):
        sub_len = S // d
        run = min(sub_len, PERMUTED_ROWS_PER_STEP)
        per = PERMUTED_ROWS_PER_STEP // run

        def index(b, j):
            q0 = jnp.minimum(j, N_NATURAL_GRID_STEPS - 1) * PERMUTED_ROWS_PER_STEP
            return (b, (q0 % sub_len) // run, q0 // (sub_len * per))

        return x.reshape(B, sub_len, d * D), pl.BlockSpec((1, run, per * D), index)

    (x1, x1_spec), (x2, x2_spec) = (permuted(d) for d in ATT_DILATIONS[1:])
    return pl.pallas_call(
        _inproj_kernel,
        grid=(B, N_COL_STEPS // DOTS_PER_STEP),
        in_specs=[
            pl.BlockSpec((1, S, D), lambda b, j: (b, 0, 0)),
            x1_spec, x2_spec,
            pl.BlockSpec((1, 1, 3 * D), lambda b, j: (b, 0, 0)),
            pl.BlockSpec((1, D), lambda b, j: (0, 0)),
            pl.BlockSpec((D, DOTS_PER_STEP * COL_BLOCK), lambda b, j: (0, j)),
            tab, tab,
        ],
        out_specs=pl.BlockSpec((1, DOTS_PER_STEP * BLOCKS_PER_STEP, S, LANES),
                               lambda b, j: (b, j, 0, 0)),
        out_shape=jax.ShapeDtypeStruct((B, N_BLOCKS, S, LANES), BF16),
        scratch_shapes=[pltpu.VMEM((S, D), BF16)] * N_ATT_GROUPS
                       + [pltpu.VMEM((S, LANES), F32)],
        compiler_params=_params("arbitrary", "arbitrary"),
        name="inproj",
    )(x, x1, x2, mod_l, norm_w_l.reshape(1, D), w_in_l, cos, sin)


def _attn_kernel(q0, k0, v0, q1, k1, v1, q2, k2, v2, z_ref, y_ref,
                 sb_sc, sd_sc, m_sc, acc_sc, l_sc):
    S = y_ref.shape[2]
    T = ATT_BLOCK
    c_exp2 = ATT_HEAD_DIM ** -0.5 * np.log2(np.e)
    row = lax.broadcasted_iota(jnp.int32, (T, 2 * T), 0)
    col = lax.broadcasted_iota(jnp.int32, (T, 2 * T), 1)
    mask_band = (col >= row) & (col <= row + T)
    mask_diag = (lax.broadcasted_iota(jnp.int32, (T, T), 1)
                 <= lax.broadcasted_iota(jnp.int32, (T, T), 0))
    groups = ((q0, k0, v0), (q1, k1, v1), (q2, k2, v2))
    d1, d2 = ATT_DILATIONS[1], ATT_DILATIONS[2]
    len1 = S // d1
    nb1 = len1 // T
    n_band0 = S // T - 1

    def scores(g, q_start, band, blk, out_idx):
        q_ref, k_ref, _ = groups[g]
        q = q_ref[0, 0, pl.ds(q_start, T), :]
        if band:
            k = k_ref[0, 0, pl.ds(q_start - T, 2 * T), :]
        else:
            k = k_ref[0, 0, pl.ds(q_start, T), :]
        s = lax.dot_general(q, k, (((1,), (1,)), ((), ())), preferred_element_type=F32) * c_exp2
        s = jnp.where(mask_band if band else mask_diag, s, NEG_INF)
        (sb_sc if band else sd_sc)[blk] = s
        m_sc[g, out_idx, :] = jnp.broadcast_to(jnp.max(s, axis=-1, keepdims=True), (T, LANES))

    def weighted_values(g, q_start, band, blk, out_idx):
        v_ref = groups[g][2]
        m = m_sc[0, out_idx, :]
        if band:
            p = jnp.exp2(sb_sc[blk] - jnp.concatenate([m, m], axis=-1))
            v = v_ref[0, 0, pl.ds(q_start - T, 2 * T), :]
        else:
            p = jnp.exp2(sd_sc[blk] - m)
            v = v_ref[0, 0, pl.ds(q_start, T), :]
        v1 = jnp.concatenate([v, jnp.ones_like(v)], axis=-1)
        r = jnp.dot(p.astype(BF16), v1, preferred_element_type=F32)
        acc_sc[g, out_idx, :] = r[:, :LANES]
        l_sc[g, out_idx, :] = r[:, LANES:]

    def all_blocks(fn):
        fn(0, 0, False, 0, pl.ds(0, T))
        for i in range(n_band0):
            fn(0, (i + 1) * T, True, i, pl.ds((i + 1) * T, T))
        for r in range(d1):
            fn(1, r * len1, False, 1 + r, pl.ds(r, T, stride=d1))
            for n in range(1, nb1):
                fn(1, r * len1 + n * T, True, n_band0 + r * (nb1 - 1) + n - 1,
                   pl.ds(n * T * d1 + r, T, stride=d1))
        for r in range(d2):
            fn(2, r * T, False, 1 + d1 + r, pl.ds(r, T, stride=d2))

    all_blocks(scores)

    rows = 256

    def shared_max(c, carry):
        idx = pl.ds(pl.multiple_of(c * rows, rows), rows)
        m_sc[0, idx, :] = jnp.maximum(jnp.maximum(m_sc[0, idx, :], m_sc[1, idx, :]), m_sc[2, idx, :])
        return carry

    lax.fori_loop(0, S // rows, shared_max, 0)

    all_blocks(weighted_values)

    def finish(c, carry):
        idx = pl.ds(pl.multiple_of(c * rows, rows), rows)
        z = z_ref[0, 0, idx, :].astype(F32)
        o = ((acc_sc[0, idx, :] + acc_sc[1, idx, :] + acc_sc[2, idx, :])
             / (l_sc[0, idx, :] + l_sc[1, idx, :] + l_sc[2, idx, :]))
        y_ref[0, 0, idx, :] = (o * (z * jax.nn.sigmoid(z))).astype(BF16)
        return carry

    lax.fori_loop(0, S // rows, finish, 0)


def _attn(proj):
    B, _, S, _ = proj.shape

    def spec(base):
        return pl.BlockSpec((1, 1, S, LANES), lambda b, s: (b, base + s, 0, 0))

    in_specs = []
    for g in range(N_ATT_GROUPS):
        in_specs += [spec(BLK_Q[g]), spec(BLK_K[g]), spec(BLK_V[g])]
    in_specs.append(spec(BLK_ZA))
    n_diag = sum(ATT_DILATIONS)
    n_band = N_ATT_GROUPS * S // ATT_BLOCK - n_diag
    return pl.pallas_call(
        _attn_kernel,
        grid=(B, ATT_SLOTS),
        in_specs=in_specs,
        out_specs=pl.BlockSpec((1, 1, S, LANES), lambda b, s: (b, s, 0, 0)),
        out_shape=jax.ShapeDtypeStruct((B, ATT_SLOTS, S, LANES), BF16),
        scratch_shapes=[pltpu.VMEM((n_band, ATT_BLOCK, 2 * ATT_BLOCK), F32),
                        pltpu.VMEM((n_diag, ATT_BLOCK, ATT_BLOCK), F32),
                        pltpu.VMEM((N_ATT_GROUPS, S, LANES), F32),
                        pltpu.VMEM((N_ATT_GROUPS, S, LANES), F32),
                        pltpu.VMEM((N_ATT_GROUPS, S, LANES), F32)],
        compiler_params=_params("arbitrary", "arbitrary"),
        name="attn",
    )(*([proj] * 10))


def _ret_decay_tables():
    H, C = RET_HEADS, RET_CHUNK
    log_g = np.log1p(-np.exp2(-5.0 - np.arange(H, dtype=np.float64)))
    idx = np.arange(C, dtype=np.float64)
    diff = idx[:, None] - idx[None, :]
    inner = np.where(diff >= 0, np.exp(log_g[:, None, None] * np.maximum(diff, 0.0)), 0.0)
    q_decay = np.exp(log_g[:, None] * (idx + 1.0))
    k_decay = np.exp(log_g[:, None] * (C - 1.0 - idx))
    chunk_decay = np.exp(log_g * C)
    f = lambda a: jnp.asarray(a, dtype=F32)
    return f(inner), f(q_decay[:, :, None]), f(k_decay[:, :, None]), f(chunk_decay)


def _ret_kernel(cd_ref, q_ref, k_ref, v_ref, z_ref, inner_ref, qd_ref, kd_ref,
                gnw_ref, y_ref, intra_sc, u_sc, st_sc):
    S = y_ref.shape[2]
    C = RET_CHUNK
    n_chunks = S // C
    half = RET_DIM // 2
    per_head = RET_DIM // LANES

    def head(ref, hh, idx):
        return jnp.concatenate([ref[0, hh * per_head + i, idx, :] for i in range(per_head)],
                               axis=-1)

    def state_free(hh):
        for n in range(n_chunks):
            idx = pl.ds(n * C, C)
            qc, kc, vc = head(q_ref, hh, idx), head(k_ref, hh, idx), head(v_ref, hh, idx)
            att = lax.dot_general(qc, kc, (((1,), (1,)), ((), ())), preferred_element_type=F32)
            att = (att * inner_ref[hh]).astype(BF16)
            intra_sc[hh, idx, :] = jnp.dot(att, vc, preferred_element_type=F32)
            if n + 1 < n_chunks:
                kdt = (kc.astype(F32) * kd_ref[hh]).T.astype(BF16)
                u_sc[hh, n] = jnp.dot(kdt, vc, preferred_element_type=F32)

    def recurrence(hh):
        cd = cd_ref[pl.program_id(1) * RET_HEADS_PER_STEP + hh]
        slab = 64
        for r0 in range(0, RET_DIM, slab):
            state = jnp.zeros((slab, RET_DIM), F32)
            for n in range(1, n_chunks):
                state = state * cd + u_sc[hh, n - 1, r0:r0 + slab, :]
                st_sc[hh, n, r0:r0 + slab, :] = state.astype(BF16)

    def finish(hh):
        gnw = gnw_ref[:, hh * RET_DIM:(hh + 1) * RET_DIM]
        for n in range(n_chunks):
            idx = pl.ds(n * C, C)
            out = intra_sc[hh, idx, :]
            if n > 0:
                out = out + jnp.dot(head(q_ref, hh, idx), st_sc[hh, n],
                                    preferred_element_type=F32) * qd_ref[hh]
            mu = jnp.mean(out, axis=-1, keepdims=True)
            cen = out - mu
            var = jnp.mean(cen * cen, axis=-1, keepdims=True)
            o = cen * lax.rsqrt(var + EPS) * gnw
            z = head(z_ref, hh, idx).astype(F32)
            y = (o * (z * jax.nn.sigmoid(z))).astype(BF16)
            y_ref[0, hh * per_head, idx, :] = y[:, :half]
            y_ref[0, hh * per_head + 1, idx, :] = y[:, half:]

    for hh in range(RET_HEADS_PER_STEP):
        state_free(hh)
        if hh > 0:
            finish(hh - 1)
        recurrence(hh)
    finish(RET_HEADS_PER_STEP - 1)


def _ret(proj, gn_w_l):
    B, _, S, _ = proj.shape
    inner, q_decay, k_decay, chunk_decay = _ret_decay_tables()
    blocks = RET_HEADS_PER_STEP * RET_DIM // LANES
    hps = RET_HEADS_PER_STEP

    def spec(base):
        return pl.BlockSpec((1, blocks, S, LANES), lambda b, h: (b, base // blocks + h, 0, 0))

    return pl.pallas_call(
        _ret_kernel,
        grid=(B, RET_HEADS // hps),
        in_specs=[
            pl.BlockSpec(memory_space=pltpu.SMEM),
            spec(BLK_QR), spec(BLK_KR), spec(BLK_VR), spec(BLK_ZR),
            pl.BlockSpec((hps, RET_CHUNK, RET_CHUNK), lambda b, h: (h, 0, 0)),
            pl.BlockSpec((hps, RET_CHUNK, 1), lambda b, h: (h, 0, 0)),
            pl.BlockSpec((hps, RET_CHUNK, 1), lambda b, h: (h, 0, 0)),
            pl.BlockSpec((1, hps * RET_DIM), lambda b, h: (0, h)),
        ],
        out_specs=pl.BlockSpec((1, blocks, S, LANES), lambda b, h: (b, h, 0, 0)),
        out_shape=jax.ShapeDtypeStruct((B, RET_HEADS * RET_DIM // LANES, S, LANES), BF16),
        scratch_shapes=[pltpu.VMEM((hps, S, RET_DIM), F32),
                        pltpu.VMEM((hps, S // RET_CHUNK, RET_DIM, RET_DIM), F32),
                        pltpu.VMEM((hps, S // RET_CHUNK, RET_DIM, RET_DIM), BF16)],
        compiler_params=_params("arbitrary", "arbitrary"),
        name="retention",
    )(chunk_decay, proj, proj, proj, proj, inner, q_decay, k_decay,
      gn_w_l.reshape(1, RET_HEADS * RET_DIM))


def _outproj_kernel(ya_ref, yr_ref, ga_ref, gr_ref, x_ref, gate_ref, wpa_ref, wpr_ref, wo_ref,
                    fnw_ref, o_ref, *, final_norm):
    def cat(ref):
        return jnp.concatenate([ref[0, i] for i in range(ref.shape[1])], axis=-1)

    a = jnp.dot(cat(ya_ref), wpa_ref[...], preferred_element_type=F32)
    r = jnp.dot(cat(yr_ref), wpr_ref[...], preferred_element_type=F32)
    merged = (jax.nn.sigmoid(cat(ga_ref).astype(F32)) * a
              + jax.nn.sigmoid(cat(gr_ref).astype(F32)) * r)
    out = x_ref[0] + gate_ref[0] * jnp.dot(merged.astype(BF16), wo_ref[...],
                                           preferred_element_type=F32)
    if final_norm:
        ms = jnp.mean(out * out, axis=-1, keepdims=True)
        out = out * lax.rsqrt(ms + EPS) * fnw_ref[...]
    o_ref[0] = out


def _outproj(x, ya, yr, proj, mod_l, wpa, wpr, wo, final_norm_w, final_norm):
    B, S, D = x.shape
    tm = 512
    n_g = D // LANES
    return pl.pallas_call(
        functools.partial(_outproj_kernel, final_norm=final_norm),
        grid=(B, S // tm),
        in_specs=[
            pl.BlockSpec((1, ya.shape[1], tm, LANES), lambda b, i: (b, 0, i, 0)),
            pl.BlockSpec((1, yr.shape[1], tm, LANES), lambda b, i: (b, 0, i, 0)),
            pl.BlockSpec((1, n_g, tm, LANES), lambda b, i: (b, BLK_GA // n_g, i, 0)),
            pl.BlockSpec((1, n_g, tm, LANES), lambda b, i: (b, BLK_GR // n_g, i, 0)),
            pl.BlockSpec((1, tm, D), lambda b, i: (b, i, 0)),
            pl.BlockSpec((1, 1, D), lambda b, i: (b, 0, 2)),
            pl.BlockSpec(wpa.shape, lambda b, i: (0, 0)),
            pl.BlockSpec(wpr.shape, lambda b, i: (0, 0)),
            pl.BlockSpec(wo.shape, lambda b, i: (0, 0)),
            pl.BlockSpec((1, D), lambda b, i: (0, 0)),
        ],
        out_specs=pl.BlockSpec((1, tm, D), lambda b, i: (b, i, 0)),
        out_shape=jax.ShapeDtypeStruct((B, S, D), F32),
        compiler_params=_params("arbitrary", "arbitrary"),
        name="outproj",
    )(ya, yr, proj, proj, x, mod_l, wpa, wpr, wo, final_norm_w.reshape(1, D))


def kernel(x, c, positions, norm_w, w_ada, b_ada, w_in, ret_gn_w, w_proj_attn, w_proj_ret, w_out,
           final_norm_w):
    B = x.shape[0]
    mod = _ada(c, w_ada, b_ada).reshape(DEPTH, B, 1, 3 * D_MODEL)
    cos, sin = _rope_tables(positions)
    w_in_b = jnp.concatenate([w_in[:, :, s:s + n] for s, n in _column_segments()],
                             axis=-1).astype(BF16)
    wpa_b = w_proj_attn.astype(BF16)
    wpr_b = w_proj_ret.astype(BF16)
    wo_b = w_out.astype(BF16)
    for l in range(DEPTH):
        proj = _inproj(x, mod[l], norm_w[l], w_in_b[l], cos, sin)
        ya = _attn(proj)
        yr = _ret(proj, ret_gn_w[l])
        x = _outproj(x, ya, yr, proj, mod[l], wpa_b[l], wpr_b[l], wo_b[l], final_norm_w,
                     final_norm=(l == DEPTH - 1))
    return x
```

```python
import functools

import numpy as np
import jax
import jax.numpy as jnp
from jax import lax
from jax.experimental import pallas as pl
from jax.experimental.pallas import tpu as pltpu

D_MODEL = 1024
SEQ = 2048
DEPTH = 4
ATT_DILATIONS = (1, 4, 16)
N_ATT_GROUPS = 3
ATT_SLOTS = 4
ATT_HEAD_DIM = 128
ATT_BLOCK = 128
ATT_SLOTS_PER_STEP = 2
RET_HEADS = 4
RET_DIM = 256
RET_CHUNK = 128
RET_HEADS_PER_STEP = 2
ROPE_BASE = 10000.0
EPS = 1e-6
NEG_INF = -1e30

LANES = 128
COL_BLOCK = 512
BLOCKS_PER_STEP = COL_BLOCK // LANES
DOTS_PER_STEP = 2
ATT_QKV = N_ATT_GROUPS * ATT_SLOTS * ATT_HEAD_DIM
IN_WIDTH = 3 * ATT_QKV + ATT_SLOTS * ATT_HEAD_DIM + 4 * RET_HEADS * RET_DIM + 2 * D_MODEL
N_COL_STEPS = IN_WIDTH // COL_BLOCK
N_BLOCKS = IN_WIDTH // LANES


def _column_segments():
    grp = ATT_SLOTS * ATT_HEAD_DIM
    qkv = lambda t, g: (t * ATT_QKV + g * grp, grp)
    seg = [qkv(0, 0), qkv(1, 0), qkv(2, 0), (3 * ATT_QKV, IN_WIDTH - 3 * ATT_QKV)]
    for g in range(1, N_ATT_GROUPS):
        seg += [qkv(0, g), qkv(1, g), qkv(2, g)]
    return seg


_GRP_BLOCKS = ATT_SLOTS * ATT_HEAD_DIM // LANES
_WIDE_BLOCKS = RET_HEADS * RET_DIM // LANES
BLK_ZA = 3 * _GRP_BLOCKS
BLK_QR = BLK_ZA + _GRP_BLOCKS
BLK_KR = BLK_QR + _WIDE_BLOCKS
BLK_VR = BLK_KR + _WIDE_BLOCKS
BLK_ZR = BLK_VR + _WIDE_BLOCKS
BLK_GA = BLK_ZR + _WIDE_BLOCKS
BLK_GR = BLK_GA + D_MODEL // LANES
_BLK_G1 = BLK_GR + D_MODEL // LANES
BLK_Q = (0,) + tuple(_BLK_G1 + (g - 1) * 3 * _GRP_BLOCKS for g in range(1, N_ATT_GROUPS))
BLK_K = tuple(b + _GRP_BLOCKS for b in BLK_Q)
BLK_V = tuple(b + 2 * _GRP_BLOCKS for b in BLK_Q)
N_NATURAL_STEPS = _BLK_G1 // BLOCKS_PER_STEP
STEPS_PER_GROUP = 3 * _GRP_BLOCKS // BLOCKS_PER_STEP

VMEM_LIMIT = 60 * 1024 * 1024
F32 = jnp.float32
BF16 = jnp.bfloat16


def _params(*sem):
    return pltpu.CompilerParams(dimension_semantics=sem, vmem_limit_bytes=VMEM_LIMIT)


def _ada_kernel(c_ref, w_ref, b_ref, o_ref):
    c = c_ref[...]
    c_act = (c * jax.nn.sigmoid(c)).astype(BF16)
    acc = jnp.dot(c_act, w_ref[0].astype(BF16), preferred_element_type=F32)
    o_ref[0] = acc + b_ref[0]


def _ada(c, w_ada, b_ada):
    B = c.shape[0]
    n_col = 3 * D_MODEL // D_MODEL
    return pl.pallas_call(
        _ada_kernel,
        grid=(DEPTH, n_col),
        in_specs=[
            pl.BlockSpec((B, D_MODEL), lambda l, j: (0, 0)),
            pl.BlockSpec((1, D_MODEL, D_MODEL), lambda l, j: (l, 0, j)),
            pl.BlockSpec((1, 1, D_MODEL), lambda l, j: (l, 0, j)),
        ],
        out_specs=pl.BlockSpec((1, B, D_MODEL), lambda l, j: (l, 0, j)),
        out_shape=jax.ShapeDtypeStruct((DEPTH, B, 3 * D_MODEL), F32),
        compiler_params=_params("arbitrary", "arbitrary"),
        name="ada",
    )(c, w_ada, b_ada.reshape(DEPTH, 1, 3 * D_MODEL))


def _rope_kernel(pos_ref, theta_ref, cos_ref, sin_ref):
    ang = pos_ref[0].astype(F32) * theta_ref[...]
    cos_ref[0] = jnp.cos(ang)
    sin_ref[0] = jnp.sin(ang)


def _rope_tables(positions):
    B, S = positions.shape
    half = RET_DIM // 2
    theta = ROPE_BASE ** (-jnp.arange(half, dtype=F32) / half)
    spec = pl.BlockSpec((1, S, half), lambda b: (b, 0, 0))
    return pl.pallas_call(
        _rope_kernel,
        grid=(B,),
        in_specs=[pl.BlockSpec((1, S, 1), lambda b: (b, 0, 0)),
                  pl.BlockSpec((1, half), lambda b: (0, 0))],
        out_specs=[spec, spec],
        out_shape=[jax.ShapeDtypeStruct((B, S, half), F32)] * 2,
        compiler_params=_params("arbitrary"),
        name="rope",
    )(positions.reshape(B, S, 1), theta.reshape(1, half))


def _row_order_of_column_block(cb):
    return 0 if cb < N_NATURAL_STEPS else 1 + (cb - N_NATURAL_STEPS) // STEPS_PER_GROUP


def _rotary_scale_of_column_block(cb):
    blk = cb * BLOCKS_PER_STEP
    if BLK_QR <= blk < BLK_KR:
        return 1.0
    if BLK_KR <= blk < BLK_VR:
        return RET_DIM ** -0.5
    return None


def _inproj_kernel(x_ref, mod_ref, nw_ref, w_ref, cos_ref, sin_ref, o_ref,
                   h0_ref, h1_ref, h2_ref, hn_ref):
    j = pl.program_id(1)
    S = x_ref.shape[1]
    rows = 128
    n_lane_blocks = D_MODEL // LANES
    h_refs = (h0_ref, h1_ref, h2_ref)
    half = RET_DIM // 2

    def project(h_ref, i, rotary_scale):
        res = jnp.dot(h_ref[...], w_ref[0, :, i * COL_BLOCK:(i + 1) * COL_BLOCK],
                      preferred_element_type=F32)
        blocks = [res[:, c * LANES:(c + 1) * LANES] for c in range(BLOCKS_PER_STEP)]
        if rotary_scale is not None:
            cos, sin = cos_ref[0], sin_ref[0]
            if rotary_scale != 1.0:
                cos, sin = cos * rotary_scale, sin * rotary_scale
            for c in range(0, BLOCKS_PER_STEP, RET_DIM // LANES):
                t1, t2 = blocks[c], blocks[c + half // LANES]
                blocks[c] = t1 * cos - t2 * sin
                blocks[c + half // LANES] = t2 * cos + t1 * sin
        for c, blk in enumerate(blocks):
            o_ref[0, i * BLOCKS_PER_STEP + c] = blk.astype(BF16)

    @pl.when(j == 0)
    def _():
        shift = mod_ref[0, :, 0:D_MODEL]
        wmul = nw_ref[0] * (1.0 + mod_ref[0, :, D_MODEL:2 * D_MODEL])

        def natural(c, carry):
            idx = pl.ds(pl.multiple_of(c * rows, rows), rows)
            xs = x_ref[0, idx, :]
            ms = jnp.mean(xs * xs, axis=-1, keepdims=True)
            hv = xs * lax.rsqrt(ms + EPS) * wmul + shift
            h0_ref[idx, :] = hv.astype(BF16)
            for cb in range(n_lane_blocks):
                hn_ref[cb, idx, :] = hv[:, cb * LANES:(cb + 1) * LANES]
            return carry

        lax.fori_loop(0, S // rows, natural, 0, unroll=2)

        def permuted(c, carry):
            p0 = pl.multiple_of(c * rows, rows)
            for h_ref, d in zip(h_refs[1:], ATT_DILATIONS[1:]):
                sub_len = S // d
                start = (p0 % sub_len) * d + p0 // sub_len
                hv = jnp.concatenate(
                    [hn_ref[cb, pl.ds(start, rows, stride=d), :] for cb in range(n_lane_blocks)],
                    axis=-1)
                h_ref[pl.ds(p0, rows), :] = hv.astype(BF16)
            return carry

        lax.fori_loop(0, S // rows, permuted, 0, unroll=2)

    steps_by_kind = {}
    for s in range(N_COL_STEPS // DOTS_PER_STEP):
        cbs = [s * DOTS_PER_STEP + i for i in range(DOTS_PER_STEP)]
        kind = tuple((_row_order_of_column_block(cb), _rotary_scale_of_column_block(cb))
                     for cb in cbs)
        steps_by_kind.setdefault(kind, []).append(s)
    for kind, steps in steps_by_kind.items():
        @pl.when(functools.reduce(jnp.logical_or, [j == s for s in steps]))
        def _(kind=kind):
            for i, (order, rotary_scale) in enumerate(kind):
                project(h_refs[order], i, rotary_scale)


def _inproj(x, mod_l, norm_w, w_in, cos, sin, layer):
    B, S, D = x.shape
    tab = pl.BlockSpec((1, S, LANES), lambda b, j: (b, 0, 0))
    return pl.pallas_call(
        _inproj_kernel,
        grid=(B, N_COL_STEPS // DOTS_PER_STEP),
        in_specs=[
            pl.BlockSpec((1, S, D), lambda b, j: (b, 0, 0)),
            pl.BlockSpec((1, 1, 3 * D), lambda b, j: (b, 0, 0)),
            pl.BlockSpec((1, 1, D), lambda b, j: (layer, 0, 0)),
            pl.BlockSpec((1, D, DOTS_PER_STEP * COL_BLOCK), lambda b, j: (layer, 0, j)),
            tab, tab,
        ],
        out_specs=pl.BlockSpec((1, DOTS_PER_STEP * BLOCKS_PER_STEP, S, LANES),
                               lambda b, j: (b, j, 0, 0)),
        out_shape=jax.ShapeDtypeStruct((B, N_BLOCKS, S, LANES), BF16),
        scratch_shapes=[pltpu.VMEM((S, D), BF16)] * N_ATT_GROUPS
                       + [pltpu.VMEM((D // LANES, S, LANES), F32)],
        compiler_params=_params("arbitrary", "arbitrary"),
        name="inproj",
    )(x, mod_l, norm_w, w_in, cos, sin)


def _attn_kernel(q0, k0, v0, q1, k1, v1, q2, k2, v2, z_ref, y_ref,
                 sb_sc, sd_sc, m_sc, acc_sc, l_sc):
    S = y_ref.shape[2]
    T = ATT_BLOCK
    c_exp2 = ATT_HEAD_DIM ** -0.5 * np.log2(np.e)
    row = lax.broadcasted_iota(jnp.int32, (T, 2 * T), 0)
    col = lax.broadcasted_iota(jnp.int32, (T, 2 * T), 1)
    mask_band = (col >= row) & (col <= row + T)
    mask_diag = (lax.broadcasted_iota(jnp.int32, (T, T), 1)
                 <= lax.broadcasted_iota(jnp.int32, (T, T), 0))
    groups = ((q0, k0, v0), (q1, k1, v1), (q2, k2, v2))
    d1, d2 = ATT_DILATIONS[1], ATT_DILATIONS[2]
    len1 = S // d1
    nb1 = len1 // T
    n_band0 = S // T - 1

    def scores(sl, g, q_start, band, blk, out_idx):
        q_ref, k_ref, _ = groups[g]
        q = q_ref[0, sl, pl.ds(q_start, T), :]
        if band:
            k = k_ref[0, sl, pl.ds(q_start - T, 2 * T), :]
        else:
            k = k_ref[0, sl, pl.ds(q_start, T), :]
        s = lax.dot_general(q, k, (((1,), (1,)), ((), ())), preferred_element_type=F32) * c_exp2
        s = jnp.where(mask_band if band else mask_diag, s, NEG_INF)
        (sb_sc if band else sd_sc)[blk] = s
        m_sc[g, out_idx, :] = jnp.broadcast_to(jnp.max(s, axis=-1, keepdims=True), (T, LANES))

    def weighted_values(sl, g, q_start, band, blk, out_idx):
        v_ref = groups[g][2]
        m = m_sc[0, out_idx, :]
        if band:
            p = jnp.exp2(sb_sc[blk] - jnp.concatenate([m, m], axis=-1))
            v = v_ref[0, sl, pl.ds(q_start - T, 2 * T), :]
        else:
            p = jnp.exp2(sd_sc[blk] - m)
            v = v_ref[0, sl, pl.ds(q_start, T), :]
        v1 = jnp.concatenate([v, jnp.ones_like(v)], axis=-1)
        r = jnp.dot(p.astype(BF16), v1, preferred_element_type=F32)
        acc_sc[g, out_idx, :] = r[:, :LANES]
        l_sc[g, out_idx, :] = r[:, LANES:]

    def all_blocks(fn):
        fn(0, 0, False, 0, pl.ds(0, T))
        for i in range(n_band0):
            fn(0, (i + 1) * T, True, i, pl.ds((i + 1) * T, T))
        for r in range(d1):
            fn(1, r * len1, False, 1 + r, pl.ds(r, T, stride=d1))
            for n in range(1, nb1):
                fn(1, r * len1 + n * T, True, n_band0 + r * (nb1 - 1) + n - 1,
                   pl.ds(n * T * d1 + r, T, stride=d1))
        for r in range(d2):
            fn(2, r * T, False, 1 + d1 + r, pl.ds(r, T, stride=d2))

    rows = 256

    def shared_max(c, carry):
        idx = pl.ds(pl.multiple_of(c * rows, rows), rows)
        m_sc[0, idx, :] = jnp.maximum(jnp.maximum(m_sc[0, idx, :], m_sc[1, idx, :]), m_sc[2, idx, :])
        return carry

    def finish(sl, c, carry):
        idx = pl.ds(pl.multiple_of(c * rows, rows), rows)
        z = z_ref[0, sl, idx, :].astype(F32)
        o = ((acc_sc[0, idx, :] + acc_sc[1, idx, :] + acc_sc[2, idx, :])
             / (l_sc[0, idx, :] + l_sc[1, idx, :] + l_sc[2, idx, :]))
        y_ref[0, sl, idx, :] = (o * (z * jax.nn.sigmoid(z))).astype(BF16)
        return carry

    for sl in range(ATT_SLOTS_PER_STEP):
        all_blocks(functools.partial(scores, sl))
        lax.fori_loop(0, S // rows, shared_max, 0)
        all_blocks(functools.partial(weighted_values, sl))
        lax.fori_loop(0, S // rows, functools.partial(finish, sl), 0)


def _attn(proj):
    B, _, S, _ = proj.shape
    sps = ATT_SLOTS_PER_STEP

    def spec(base):
        return pl.BlockSpec((1, sps, S, LANES), lambda b, s: (b, base // sps + s, 0, 0))

    in_specs = []
    for g in range(N_ATT_GROUPS):
        in_specs += [spec(BLK_Q[g]), spec(BLK_K[g]), spec(BLK_V[g])]
    in_specs.append(spec(BLK_ZA))
    n_diag = sum(ATT_DILATIONS)
    n_band = N_ATT_GROUPS * S // ATT_BLOCK - n_diag
    return pl.pallas_call(
        _attn_kernel,
        grid=(B, ATT_SLOTS // sps),
        in_specs=in_specs,
        out_specs=pl.BlockSpec((1, sps, S, LANES), lambda b, s: (b, s, 0, 0)),
        out_shape=jax.ShapeDtypeStruct((B, ATT_SLOTS, S, LANES), BF16),
        scratch_shapes=[pltpu.VMEM((n_band, ATT_BLOCK, 2 * ATT_BLOCK), F32),
                        pltpu.VMEM((n_diag, ATT_BLOCK, ATT_BLOCK), F32),
                        pltpu.VMEM((N_ATT_GROUPS, S, LANES), F32),
                        pltpu.VMEM((N_ATT_GROUPS, S, LANES), F32),
                        pltpu.VMEM((N_ATT_GROUPS, S, LANES), F32)],
        compiler_params=_params("arbitrary", "arbitrary"),
        name="attn",
    )(*([proj] * 10))


def _ret_decay_tables():
    H, C = RET_HEADS, RET_CHUNK
    log_g = np.log1p(-np.exp2(-5.0 - np.arange(H, dtype=np.float64)))
    idx = np.arange(C, dtype=np.float64)
    diff = idx[:, None] - idx[None, :]
    inner = np.where(diff >= 0, np.exp(log_g[:, None, None] * np.maximum(diff, 0.0)), 0.0)
    q_decay = np.exp(log_g[:, None] * (idx + 1.0))
    k_decay = np.exp(log_g[:, None] * (C - 1.0 - idx))
    chunk_decay = np.exp(log_g * C)
    f = lambda a: jnp.asarray(a, dtype=F32)
    return f(inner), f(q_decay[:, :, None]), f(k_decay[:, :, None]), f(chunk_decay)


def _ret_kernel(cd_ref, q_ref, k_ref, v_ref, z_ref, inner_ref, qd_ref, kd_ref,
                gnw_ref, y_ref, intra_sc, u_sc, st_sc):
    S = y_ref.shape[2]
    C = RET_CHUNK
    n_chunks = S // C
    half = RET_DIM // 2
    per_head = RET_DIM // LANES

    def head(ref, hh, idx):
        return jnp.concatenate([ref[0, hh * per_head + i, idx, :] for i in range(per_head)],
                               axis=-1)

    def state_free(hh):
        for n in range(n_chunks):
            idx = pl.ds(n * C, C)
            qc, kc, vc = head(q_ref, hh, idx), head(k_ref, hh, idx), head(v_ref, hh, idx)
            att = lax.dot_general(qc, kc, (((1,), (1,)), ((), ())), preferred_element_type=F32)
            att = (att * inner_ref[hh]).astype(BF16)
            intra_sc[hh, idx, :] = jnp.dot(att, vc, preferred_element_type=F32)
            if n + 1 < n_chunks:
                kdt = (kc.astype(F32) * kd_ref[hh]).T.astype(BF16)
                u_sc[hh, n] = jnp.dot(kdt, vc, preferred_element_type=F32)

    def recurrence(hh):
        cd = cd_ref[pl.program_id(1) * RET_HEADS_PER_STEP + hh]
        slab = 64
        for r0 in range(0, RET_DIM, slab):
            state = jnp.zeros((slab, RET_DIM), F32)
            for n in range(1, n_chunks):
                state = state * cd + u_sc[hh, n - 1, r0:r0 + slab, :]
                st_sc[hh, n, r0:r0 + slab, :] = state.astype(BF16)

    def finish(hh):
        gnw = gnw_ref[:, hh * RET_DIM:(hh + 1) * RET_DIM]
        for n in range(n_chunks):
            idx = pl.ds(n * C, C)
            out = intra_sc[hh, idx, :]
            if n > 0:
                out = out + jnp.dot(head(q_ref, hh, idx), st_sc[hh, n],
                                    preferred_element_type=F32) * qd_ref[hh]
            mu = jnp.mean(out, axis=-1, keepdims=True)
            cen = out - mu
            var = jnp.mean(cen * cen, axis=-1, keepdims=True)
            o = cen * lax.rsqrt(var + EPS) * gnw
            z = head(z_ref, hh, idx).astype(F32)
            y = (o * (z * jax.nn.sigmoid(z))).astype(BF16)
            y_ref[0, hh * per_head, idx, :] = y[:, :half]
            y_ref[0, hh * per_head + 1, idx, :] = y[:, half:]

    for hh in range(RET_HEADS_PER_STEP):
        state_free(hh)
        if hh > 0:
            finish(hh - 1)
        recurrence(hh)
    finish(RET_HEADS_PER_STEP - 1)


def _ret(proj, gn_w_l):
    B, _, S, _ = proj.shape
    inner, q_decay, k_decay, chunk_decay = _ret_decay_tables()
    blocks = RET_HEADS_PER_STEP * RET_DIM // LANES
    hps = RET_HEADS_PER_STEP

    def spec(base):
        return pl.BlockSpec((1, blocks, S, LANES), lambda b, h: (b, base // blocks + h, 0, 0))

    return pl.pallas_call(
        _ret_kernel,
        grid=(B, RET_HEADS // hps),
        in_specs=[
            pl.BlockSpec(memory_space=pltpu.SMEM),
            spec(BLK_QR), spec(BLK_KR), spec(BLK_VR), spec(BLK_ZR),
            pl.BlockSpec((hps, RET_CHUNK, RET_CHUNK), lambda b, h: (h, 0, 0)),
            pl.BlockSpec((hps, RET_CHUNK, 1), lambda b, h: (h, 0, 0)),
            pl.BlockSpec((hps, RET_CHUNK, 1), lambda b, h: (h, 0, 0)),
            pl.BlockSpec((1, hps * RET_DIM), lambda b, h: (0, h)),
        ],
        out_specs=pl.BlockSpec((1, blocks, S, LANES), lambda b, h: (b, h, 0, 0)),
        out_shape=jax.ShapeDtypeStruct((B, RET_HEADS * RET_DIM // LANES, S, LANES), BF16),
        scratch_shapes=[pltpu.VMEM((hps, S, RET_DIM), F32),
                        pltpu.VMEM((hps, S // RET_CHUNK, RET_DIM, RET_DIM), F32),
                        pltpu.VMEM((hps, S // RET_CHUNK, RET_DIM, RET_DIM), BF16)],
        compiler_params=_params("arbitrary", "arbitrary"),
        name="retention",
    )(chunk_decay, proj, proj, proj, proj, inner, q_decay, k_decay,
      gn_w_l.reshape(1, RET_HEADS * RET_DIM))


def _outproj_kernel(ya_ref, yr_ref, ga_ref, gr_ref, x_ref, gate_ref, wpa_ref, wpr_ref, wo_ref,
                    fnw_ref, o_ref, *, final_norm):
    def cat(ref):
        return jnp.concatenate([ref[0, i] for i in range(ref.shape[1])], axis=-1)

    a = jnp.dot(cat(ya_ref), wpa_ref[0], preferred_element_type=F32)
    r = jnp.dot(cat(yr_ref), wpr_ref[0], preferred_element_type=F32)
    merged = (jax.nn.sigmoid(cat(ga_ref).astype(F32)) * a
              + jax.nn.sigmoid(cat(gr_ref).astype(F32)) * r)
    out = x_ref[0] + gate_ref[0] * jnp.dot(merged.astype(BF16), wo_ref[0],
                                           preferred_element_type=F32)
    if final_norm:
        ms = jnp.mean(out * out, axis=-1, keepdims=True)
        out = out * lax.rsqrt(ms + EPS) * fnw_ref[...]
    o_ref[0] = out


def _outproj(x, ya, yr, proj, mod_l, wpa, wpr, wo, final_norm_w, layer, final_norm):
    B, S, D = x.shape
    tm = 512
    n_g = D // LANES
    weight = lambda w: pl.BlockSpec((1,) + w.shape[1:], lambda b, i: (layer, 0, 0))
    return pl.pallas_call(
        functools.partial(_outproj_kernel, final_norm=final_norm),
        grid=(B, S // tm),
        in_specs=[
            pl.BlockSpec((1, ya.shape[1], tm, LANES), lambda b, i: (b, 0, i, 0)),
            pl.BlockSpec((1, yr.shape[1], tm, LANES), lambda b, i: (b, 0, i, 0)),
            pl.BlockSpec((1, n_g, tm, LANES), lambda b, i: (b, BLK_GA // n_g, i, 0)),
            pl.BlockSpec((1, n_g, tm, LANES), lambda b, i: (b, BLK_GR // n_g, i, 0)),
            pl.BlockSpec((1, tm, D), lambda b, i: (b, i, 0)),
            pl.BlockSpec((1, 1, D), lambda b, i: (b, 0, 2)),
            weight(wpa), weight(wpr), weight(wo),
            pl.BlockSpec((1, D), lambda b, i: (0, 0)),
        ],
        out_specs=pl.BlockSpec((1, tm, D), lambda b, i: (b, i, 0)),
        out_shape=jax.ShapeDtypeStruct((B, S, D), F32),
        compiler_params=_params("arbitrary", "arbitrary"),
        name="outproj",
    )(ya, yr, proj, proj, x, mod_l, wpa, wpr, wo, final_norm_w.reshape(1, D))


def kernel(x, c, positions, norm_w, w_ada, b_ada, w_in, ret_gn_w, w_proj_attn, w_proj_ret, w_out,
           final_norm_w):
    B = x.shape[0]
    mod = _ada(c, w_ada, b_ada).reshape(DEPTH, B, 1, 3 * D_MODEL)
    cos, sin = _rope_tables(positions)
    w_in_b = jnp.concatenate([w_in[:, :, s:s + n] for s, n in _column_segments()],
                             axis=-1).astype(BF16)
    wpa_b = w_proj_attn.astype(BF16)
    wpr_b = w_proj_ret.astype(BF16)
    wo_b = w_out.astype(BF16)
    norm_w3 = norm_w.reshape(DEPTH, 1, D_MODEL)
    for l in range(DEPTH):
        proj = _inproj(x, mod[l], norm_w3, w_in_b, cos, sin, layer=l)
        ya = _attn(proj)
        yr = _ret(proj, ret_gn_w[l])
        x = _outproj(x, ya, yr, proj, mod[l], wpa_b, wpr_b, wo_b, final_norm_w, layer=l,
                     final_norm=(l == DEPTH - 1))
    return x
```

```python
import functools

import numpy as np
import jax
import jax.numpy as jnp
from jax import lax
from jax.experimental import pallas as pl
from jax.experimental.pallas import tpu as pltpu

D_MODEL = 1024
SEQ = 2048
DEPTH = 4
ATT_DILATIONS = (1, 4, 16)
N_ATT_GROUPS = 3
ATT_SLOTS = 4
ATT_HEAD_DIM = 128
ATT_BLOCK = 128
ATT_SLOTS_PER_STEP = 2
RET_HEADS = 4
RET_DIM = 256
RET_CHUNK = 256
RET_HEADS_PER_STEP = 2
ROPE_BASE = 10000.0
EPS = 1e-6
NEG_INF = -1e30

LANES = 128
COL_BLOCK = 512
BLOCKS_PER_STEP = COL_BLOCK // LANES
DOTS_PER_STEP = 2
ATT_QKV = N_ATT_GROUPS * ATT_SLOTS * ATT_HEAD_DIM
IN_WIDTH = 3 * ATT_QKV + ATT_SLOTS * ATT_HEAD_DIM + 4 * RET_HEADS * RET_DIM + 2 * D_MODEL
N_COL_STEPS = IN_WIDTH // COL_BLOCK
N_BLOCKS = IN_WIDTH // LANES


def _column_segments():
    grp = ATT_SLOTS * ATT_HEAD_DIM
    qkv = lambda t, g: (t * ATT_QKV + g * grp, grp)
    seg = [qkv(0, 0), qkv(1, 0), qkv(2, 0), (3 * ATT_QKV, IN_WIDTH - 3 * ATT_QKV)]
    for g in range(1, N_ATT_GROUPS):
        seg += [qkv(0, g), qkv(1, g), qkv(2, g)]
    return seg


_GRP_BLOCKS = ATT_SLOTS * ATT_HEAD_DIM // LANES
_WIDE_BLOCKS = RET_HEADS * RET_DIM // LANES
BLK_ZA = 3 * _GRP_BLOCKS
BLK_QR = BLK_ZA + _GRP_BLOCKS
BLK_KR = BLK_QR + _WIDE_BLOCKS
BLK_VR = BLK_KR + _WIDE_BLOCKS
BLK_ZR = BLK_VR + _WIDE_BLOCKS
BLK_GA = BLK_ZR + _WIDE_BLOCKS
BLK_GR = BLK_GA + D_MODEL // LANES
_BLK_G1 = BLK_GR + D_MODEL // LANES
BLK_Q = (0,) + tuple(_BLK_G1 + (g - 1) * 3 * _GRP_BLOCKS for g in range(1, N_ATT_GROUPS))
BLK_K = tuple(b + _GRP_BLOCKS for b in BLK_Q)
BLK_V = tuple(b + 2 * _GRP_BLOCKS for b in BLK_Q)
N_NATURAL_STEPS = _BLK_G1 // BLOCKS_PER_STEP
STEPS_PER_GROUP = 3 * _GRP_BLOCKS // BLOCKS_PER_STEP

VMEM_LIMIT = 60 * 1024 * 1024
F32 = jnp.float32
BF16 = jnp.bfloat16


def _params(*sem):
    return pltpu.CompilerParams(dimension_semantics=sem, vmem_limit_bytes=VMEM_LIMIT)


def _ada_kernel(c_ref, w_ref, b_ref, o_ref):
    c = c_ref[...]
    c_act = (c * jax.nn.sigmoid(c)).astype(BF16)
    acc = jnp.dot(c_act, w_ref[0].astype(BF16), preferred_element_type=F32)
    o_ref[0] = acc + b_ref[0]


def _ada(c, w_ada, b_ada):
    B = c.shape[0]
    n_col = 3 * D_MODEL // D_MODEL
    return pl.pallas_call(
        _ada_kernel,
        grid=(DEPTH, n_col),
        in_specs=[
            pl.BlockSpec((B, D_MODEL), lambda l, j: (0, 0)),
            pl.BlockSpec((1, D_MODEL, D_MODEL), lambda l, j: (l, 0, j)),
            pl.BlockSpec((1, 1, D_MODEL), lambda l, j: (l, 0, j)),
        ],
        out_specs=pl.BlockSpec((1, B, D_MODEL), lambda l, j: (l, 0, j)),
        out_shape=jax.ShapeDtypeStruct((DEPTH, B, 3 * D_MODEL), F32),
        compiler_params=_params("arbitrary", "arbitrary"),
        name="ada",
    )(c, w_ada, b_ada.reshape(DEPTH, 1, 3 * D_MODEL))


def _rope_kernel(pos_ref, theta_ref, cos_ref, sin_ref):
    ang = pos_ref[0].astype(F32) * theta_ref[...]
    cos_ref[0] = jnp.cos(ang)
    sin_ref[0] = jnp.sin(ang)


def _rope_tables(positions):
    B, S = positions.shape
    half = RET_DIM // 2
    theta = ROPE_BASE ** (-jnp.arange(half, dtype=F32) / half)
    spec = pl.BlockSpec((1, S, half), lambda b: (b, 0, 0))
    return pl.pallas_call(
        _rope_kernel,
        grid=(B,),
        in_specs=[pl.BlockSpec((1, S, 1), lambda b: (b, 0, 0)),
                  pl.BlockSpec((1, half), lambda b: (0, 0))],
        out_specs=[spec, spec],
        out_shape=[jax.ShapeDtypeStruct((B, S, half), F32)] * 2,
        compiler_params=_params("arbitrary"),
        name="rope",
    )(positions.reshape(B, S, 1), theta.reshape(1, half))


def _row_order_of_column_block(cb):
    return 0 if cb < N_NATURAL_STEPS else 1 + (cb - N_NATURAL_STEPS) // STEPS_PER_GROUP


def _rotary_scale_of_column_block(cb):
    blk = cb * BLOCKS_PER_STEP
    if BLK_QR <= blk < BLK_KR:
        return 1.0
    if BLK_KR <= blk < BLK_VR:
        return RET_DIM ** -0.5
    return None


def _inproj_kernel(x_ref, mod_ref, nw_ref, w_ref, cos_ref, sin_ref, o_ref,
                   h0_ref, h1_ref, h2_ref, hn_ref):
    j = pl.program_id(1)
    S = x_ref.shape[1]
    rows = 128
    n_lane_blocks = D_MODEL // LANES
    h_refs = (h0_ref, h1_ref, h2_ref)
    half = RET_DIM // 2

    def project(h_ref, i, rotary_scale):
        res = jnp.dot(h_ref[...], w_ref[0, :, i * COL_BLOCK:(i + 1) * COL_BLOCK],
                      preferred_element_type=F32)
        blocks = [res[:, c * LANES:(c + 1) * LANES] for c in range(BLOCKS_PER_STEP)]
        if rotary_scale is not None:
            cos, sin = cos_ref[0], sin_ref[0]
            if rotary_scale != 1.0:
                cos, sin = cos * rotary_scale, sin * rotary_scale
            for c in range(0, BLOCKS_PER_STEP, RET_DIM // LANES):
                t1, t2 = blocks[c], blocks[c + half // LANES]
                blocks[c] = t1 * cos - t2 * sin
                blocks[c + half // LANES] = t2 * cos + t1 * sin
        for c, blk in enumerate(blocks):
            o_ref[0, i * BLOCKS_PER_STEP + c] = blk.astype(BF16)

    @pl.when(j == 0)
    def _():
        shift = mod_ref[0, :, 0:D_MODEL]
        wmul = nw_ref[0] * (1.0 + mod_ref[0, :, D_MODEL:2 * D_MODEL])

        def natural(c, carry):
            idx = pl.ds(pl.multiple_of(c * rows, rows), rows)
            xs = x_ref[0, idx, :]
            ms = jnp.mean(xs * xs, axis=-1, keepdims=True)
            hv = xs * lax.rsqrt(ms + EPS) * wmul + shift
            h0_ref[idx, :] = hv.astype(BF16)
            for cb in range(n_lane_blocks):
                hn_ref[cb, idx, :] = hv[:, cb * LANES:(cb + 1) * LANES]
            return carry

        lax.fori_loop(0, S // rows, natural, 0, unroll=2)

        def permuted(c, carry):
            p0 = pl.multiple_of(c * rows, rows)
            for h_ref, d in zip(h_refs[1:], ATT_DILATIONS[1:]):
                sub_len = S // d
                start = (p0 % sub_len) * d + p0 // sub_len
                hv = jnp.concatenate(
                    [hn_ref[cb, pl.ds(start, rows, stride=d), :] for cb in range(n_lane_blocks)],
                    axis=-1)
                h_ref[pl.ds(p0, rows), :] = hv.astype(BF16)
            return carry

        lax.fori_loop(0, S // rows, permuted, 0, unroll=2)

    steps_by_kind = {}
    for s in range(N_COL_STEPS // DOTS_PER_STEP):
        cbs = [s * DOTS_PER_STEP + i for i in range(DOTS_PER_STEP)]
        kind = tuple((_row_order_of_column_block(cb), _rotary_scale_of_column_block(cb))
                     for cb in cbs)
        steps_by_kind.setdefault(kind, []).append(s)
    for kind, steps in steps_by_kind.items():
        @pl.when(functools.reduce(jnp.logical_or, [j == s for s in steps]))
        def _(kind=kind):
            for i, (order, rotary_scale) in enumerate(kind):
                project(h_refs[order], i, rotary_scale)


def _inproj(x, mod_l, norm_w, w_in, cos, sin, layer):
    B, S, D = x.shape
    tab = pl.BlockSpec((1, S, LANES), lambda b, j: (b, 0, 0))
    return pl.pallas_call(
        _inproj_kernel,
        grid=(B, N_COL_STEPS // DOTS_PER_STEP),
        in_specs=[
            pl.BlockSpec((1, S, D), lambda b, j: (b, 0, 0)),
            pl.BlockSpec((1, 1, 3 * D), lambda b, j: (b, 0, 0)),
            pl.BlockSpec((1, 1, D), lambda b, j: (layer, 0, 0)),
            pl.BlockSpec((1, D, DOTS_PER_STEP * COL_BLOCK), lambda b, j: (layer, 0, j)),
            tab, tab,
        ],
        out_specs=pl.BlockSpec((1, DOTS_PER_STEP * BLOCKS_PER_STEP, S, LANES),
                               lambda b, j: (b, j, 0, 0)),
        out_shape=jax.ShapeDtypeStruct((B, N_BLOCKS, S, LANES), BF16),
        scratch_shapes=[pltpu.VMEM((S, D), BF16)] * N_ATT_GROUPS
                       + [pltpu.VMEM((D // LANES, S, LANES), F32)],
        compiler_params=_params("arbitrary", "arbitrary"),
        name="inproj",
    )(x, mod_l, norm_w, w_in, cos, sin)


def _attn_kernel(q0, k0, v0, q1, k1, v1, q2, k2, v2, z_ref, y_ref,
                 sb_sc, sd_sc, m_sc, acc_sc, l_sc):
    S = y_ref.shape[2]
    T = ATT_BLOCK
    c_exp2 = ATT_HEAD_DIM ** -0.5 * np.log2(np.e)
    row = lax.broadcasted_iota(jnp.int32, (T, 2 * T), 0)
    col = lax.broadcasted_iota(jnp.int32, (T, 2 * T), 1)
    mask_band = (col >= row) & (col <= row + T)
    mask_diag = (lax.broadcasted_iota(jnp.int32, (T, T), 1)
                 <= lax.broadcasted_iota(jnp.int32, (T, T), 0))
    groups = ((q0, k0, v0), (q1, k1, v1), (q2, k2, v2))
    d1, d2 = ATT_DILATIONS[1], ATT_DILATIONS[2]
    len1 = S // d1
    nb1 = len1 // T
    n_band0 = S // T - 1

    def scores(sl, g, q_start, band, blk, out_idx):
        q_ref, k_ref, _ = groups[g]
        q = q_ref[0, sl, pl.ds(q_start, T), :]
        if band:
            k = k_ref[0, sl, pl.ds(q_start - T, 2 * T), :]
        else:
            k = k_ref[0, sl, pl.ds(q_start, T), :]
        s = lax.dot_general(q, k, (((1,), (1,)), ((), ())), preferred_element_type=F32) * c_exp2
        s = jnp.where(mask_band if band else mask_diag, s, NEG_INF)
        (sb_sc if band else sd_sc)[blk] = s
        m_sc[g, out_idx, :] = jnp.broadcast_to(jnp.max(s, axis=-1, keepdims=True), (T, LANES))

    def weighted_values(sl, g, q_start, band, blk, out_idx):
        v_ref = groups[g][2]
        m = m_sc[0, out_idx, :]
        if band:
            p = jnp.exp2(sb_sc[blk] - jnp.concatenate([m, m], axis=-1))
            v = v_ref[0, sl, pl.ds(q_start - T, 2 * T), :]
        else:
            p = jnp.exp2(sd_sc[blk] - m)
            v = v_ref[0, sl, pl.ds(q_start, T), :]
        v1 = jnp.concatenate([v, jnp.ones_like(v)], axis=-1)
        r = jnp.dot(p.astype(BF16), v1, preferred_element_type=F32)
        acc_sc[g, out_idx, :] = r[:, :LANES]
        l_sc[g, out_idx, :] = r[:, LANES:]

    blocks = [(0, 0, False, 0, pl.ds(0, T))]
    blocks += [(0, (i + 1) * T, True, i, pl.ds((i + 1) * T, T)) for i in range(n_band0)]
    for r in range(d1):
        blocks.append((1, r * len1, False, 1 + r, pl.ds(r, T, stride=d1)))
        blocks += [(1, r * len1 + n * T, True, n_band0 + r * (nb1 - 1) + n - 1,
                    pl.ds(n * T * d1 + r, T, stride=d1)) for n in range(1, nb1)]
    wide = [(2, r * T, False, 1 + d1 + r, pl.ds(r, T, stride=d2)) for r in range(d2)]
    every = len(blocks) // len(wide)
    order = []
    for i, blk in enumerate(blocks):
        order.append(blk)
        if i % every == every - 1 and wide:
            order.append(wide.pop(0))
    order += wide

    def all_blocks(fn):
        for blk in order:
            fn(*blk)

    rows = 256

    def shared_max(c, carry):
        idx = pl.ds(pl.multiple_of(c * rows, rows), rows)
        m_sc[0, idx, :] = jnp.maximum(jnp.maximum(m_sc[0, idx, :], m_sc[1, idx, :]), m_sc[2, idx, :])
        return carry

    def finish(sl, c, carry):
        idx = pl.ds(pl.multiple_of(c * rows, rows), rows)
        z = z_ref[0, sl, idx, :].astype(F32)
        o = ((acc_sc[0, idx, :] + acc_sc[1, idx, :] + acc_sc[2, idx, :])
             / (l_sc[0, idx, :] + l_sc[1, idx, :] + l_sc[2, idx, :]))
        y_ref[0, sl, idx, :] = (o * (z * jax.nn.sigmoid(z))).astype(BF16)
        return carry

    for sl in range(ATT_SLOTS_PER_STEP):
        all_blocks(functools.partial(scores, sl))
        lax.fori_loop(0, S // rows, shared_max, 0)
        all_blocks(functools.partial(weighted_values, sl))
        lax.fori_loop(0, S // rows, functools.partial(finish, sl), 0)


def _attn(proj):
    B, _, S, _ = proj.shape
    sps = ATT_SLOTS_PER_STEP

    def spec(base):
        return pl.BlockSpec((1, sps, S, LANES), lambda b, s: (b, base // sps + s, 0, 0))

    in_specs = []
    for g in range(N_ATT_GROUPS):
        in_specs += [spec(BLK_Q[g]), spec(BLK_K[g]), spec(BLK_V[g])]
    in_specs.append(spec(BLK_ZA))
    n_diag = sum(ATT_DILATIONS)
    n_band = N_ATT_GROUPS * S // ATT_BLOCK - n_diag
    return pl.pallas_call(
        _attn_kernel,
        grid=(B, ATT_SLOTS // sps),
        in_specs=in_specs,
        out_specs=pl.BlockSpec((1, sps, S, LANES), lambda b, s: (b, s, 0, 0)),
        out_shape=jax.ShapeDtypeStruct((B, ATT_SLOTS, S, LANES), BF16),
        scratch_shapes=[pltpu.VMEM((n_band, ATT_BLOCK, 2 * ATT_BLOCK), F32),
                        pltpu.VMEM((n_diag, ATT_BLOCK, ATT_BLOCK), F32),
                        pltpu.VMEM((N_ATT_GROUPS, S, LANES), F32),
                        pltpu.VMEM((N_ATT_GROUPS, S, LANES), F32),
                        pltpu.VMEM((N_ATT_GROUPS, S, LANES), F32)],
        compiler_params=_params("arbitrary", "arbitrary"),
        name="attn",
    )(*([proj] * 10))


def _ret_decay_tables():
    H, C = RET_HEADS, RET_CHUNK
    log_g = np.log1p(-np.exp2(-5.0 - np.arange(H, dtype=np.float64)))
    idx = np.arange(C, dtype=np.float64)
    diff = idx[:, None] - idx[None, :]
    inner = np.where(diff >= 0, np.exp(log_g[:, None, None] * np.maximum(diff, 0.0)), 0.0)
    q_decay = np.exp(log_g[:, None] * (idx + 1.0))
    k_decay = np.exp(log_g[:, None] * (C - 1.0 - idx))
    chunk_decay = np.exp(log_g * C)
    f = lambda a: jnp.asarray(a, dtype=F32)
    return f(inner), f(q_decay[:, :, None]), f(k_decay[:, :, None]), f(chunk_decay)


def _ret_kernel(cd_ref, q_ref, k_ref, v_ref, z_ref, inner_ref, qd_ref, kd_ref,
                gnw_ref, y_ref, intra_sc, u_sc, st_sc):
    S = y_ref.shape[2]
    C = RET_CHUNK
    n_chunks = S // C
    half = RET_DIM // 2
    per_head = RET_DIM // LANES

    def head(ref, hh, idx):
        return jnp.concatenate([ref[0, hh * per_head + i, idx, :] for i in range(per_head)],
                               axis=-1)

    def state_free(hh):
        for n in range(n_chunks):
            idx = pl.ds(n * C, C)
            qc, kc, vc = head(q_ref, hh, idx), head(k_ref, hh, idx), head(v_ref, hh, idx)
            att = lax.dot_general(qc, kc, (((1,), (1,)), ((), ())), preferred_element_type=F32)
            att = (att * inner_ref[hh]).astype(BF16)
            intra_sc[hh, idx, :] = jnp.dot(att, vc, preferred_element_type=F32)
            if n + 1 < n_chunks:
                kdt = (kc.astype(F32) * kd_ref[hh]).T.astype(BF16)
                u_sc[hh, n] = jnp.dot(kdt, vc, preferred_element_type=F32)

    def recurrence(hh):
        cd = cd_ref[pl.program_id(1) * RET_HEADS_PER_STEP + hh]
        slab = 64
        for r0 in range(0, RET_DIM, slab):
            state = jnp.zeros((slab, RET_DIM), F32)
            for n in range(1, n_chunks):
                state = state * cd + u_sc[hh, n - 1, r0:r0 + slab, :]
                st_sc[hh, n, r0:r0 + slab, :] = state.astype(BF16)

    def finish(hh):
        gnw = gnw_ref[:, hh * RET_DIM:(hh + 1) * RET_DIM]
        for n in range(n_chunks):
            idx = pl.ds(n * C, C)
            out = intra_sc[hh, idx, :]
            if n > 0:
                out = out + jnp.dot(head(q_ref, hh, idx), st_sc[hh, n],
                                    preferred_element_type=F32) * qd_ref[hh]
            mu = jnp.mean(out, axis=-1, keepdims=True)
            cen = out - mu
            var = jnp.mean(cen * cen, axis=-1, keepdims=True)
            o = cen * lax.rsqrt(var + EPS) * gnw
            z = head(z_ref, hh, idx).astype(F32)
            y = (o * (z * jax.nn.sigmoid(z))).astype(BF16)
            y_ref[0, hh * per_head, idx, :] = y[:, :half]
            y_ref[0, hh * per_head + 1, idx, :] = y[:, half:]

    for hh in range(RET_HEADS_PER_STEP):
        state_free(hh)
        if hh > 0:
            finish(hh - 1)
        recurrence(hh)
    finish(RET_HEADS_PER_STEP - 1)


def _ret(proj, gn_w_l):
    B, _, S, _ = proj.shape
    inner, q_decay, k_decay, chunk_decay = _ret_decay_tables()
    blocks = RET_HEADS_PER_STEP * RET_DIM // LANES
    hps = RET_HEADS_PER_STEP

    def spec(base):
        return pl.BlockSpec((1, blocks, S, LANES), lambda b, h: (b, base // blocks + h, 0, 0))

    return pl.pallas_call(
        _ret_kernel,
        grid=(B, RET_HEADS // hps),
        in_specs=[
            pl.BlockSpec(memory_space=pltpu.SMEM),
            spec(BLK_QR), spec(BLK_KR), spec(BLK_VR), spec(BLK_ZR),
            pl.BlockSpec((hps, RET_CHUNK, RET_CHUNK), lambda b, h: (h, 0, 0)),
            pl.BlockSpec((hps, RET_CHUNK, 1), lambda b, h: (h, 0, 0)),
            pl.BlockSpec((hps, RET_CHUNK, 1), lambda b, h: (h, 0, 0)),
            pl.BlockSpec((1, hps * RET_DIM), lambda b, h: (0, h)),
        ],
        out_specs=pl.BlockSpec((1, blocks, S, LANES), lambda b, h: (b, h, 0, 0)),
        out_shape=jax.ShapeDtypeStruct((B, RET_HEADS * RET_DIM // LANES, S, LANES), BF16),
        scratch_shapes=[pltpu.VMEM((hps, S, RET_DIM), F32),
                        pltpu.VMEM((hps, S // RET_CHUNK, RET_DIM, RET_DIM), F32),
                        pltpu.VMEM((hps, S // RET_CHUNK, RET_DIM, RET_DIM), BF16)],
        compiler_params=_params("arbitrary", "arbitrary"),
        name="retention",
    )(chunk_decay, proj, proj, proj, proj, inner, q_decay, k_decay,
      gn_w_l.reshape(1, RET_HEADS * RET_DIM))


def _outproj_kernel(ya_ref, yr_ref, ga_ref, gr_ref, x_ref, gate_ref, wpa_ref, wpr_ref, wo_ref,
                    fnw_ref, o_ref, *, final_norm):
    def cat(ref):
        return jnp.concatenate([ref[0, i] for i in range(ref.shape[1])], axis=-1)

    a = jnp.dot(cat(ya_ref), wpa_ref[0], preferred_element_type=F32)
    r = jnp.dot(cat(yr_ref), wpr_ref[0], preferred_element_type=F32)
    merged = (jax.nn.sigmoid(cat(ga_ref).astype(F32)) * a
              + jax.nn.sigmoid(cat(gr_ref).astype(F32)) * r)
    out = x_ref[0] + gate_ref[0] * jnp.dot(merged.astype(BF16), wo_ref[0],
                                           preferred_element_type=F32)
    if final_norm:
        ms = jnp.mean(out * out, axis=-1, keepdims=True)
        out = out * lax.rsqrt(ms + EPS) * fnw_ref[...]
    o_ref[0] = out


def _outproj(x, ya, yr, proj, mod_l, wpa, wpr, wo, final_norm_w, layer, final_norm):
    B, S, D = x.shape
    tm = 512
    n_g = D // LANES
    weight = lambda w: pl.BlockSpec((1,) + w.shape[1:], lambda b, i: (layer, 0, 0))
    return pl.pallas_call(
        functools.partial(_outproj_kernel, final_norm=final_norm),
        grid=(B, S // tm),
        in_specs=[
            pl.BlockSpec((1, ya.shape[1], tm, LANES), lambda b, i: (b, 0, i, 0)),
            pl.BlockSpec((1, yr.shape[1], tm, LANES), lambda b, i: (b, 0, i, 0)),
            pl.BlockSpec((1, n_g, tm, LANES), lambda b, i: (b, BLK_GA // n_g, i, 0)),
            pl.BlockSpec((1, n_g, tm, LANES), lambda b, i: (b, BLK_GR // n_g, i, 0)),
            pl.BlockSpec((1, tm, D), lambda b, i: (b, i, 0)),
            pl.BlockSpec((1, 1, D), lambda b, i: (b, 0, 2)),
            weight(wpa), weight(wpr), weight(wo),
            pl.BlockSpec((1, D), lambda b, i: (0, 0)),
        ],
        out_specs=pl.BlockSpec((1, tm, D), lambda b, i: (b, i, 0)),
        out_shape=jax.ShapeDtypeStruct((B, S, D), F32),
        compiler_params=_params("arbitrary", "arbitrary"),
        name="outproj",
    )(ya, yr, proj, proj, x, mod_l, wpa, wpr, wo, final_norm_w.reshape(1, D))


def kernel(x, c, positions, norm_w, w_ada, b_ada, w_in, ret_gn_w, w_proj_attn, w_proj_ret, w_out,
           final_norm_w):
    B = x.shape[0]
    mod = _ada(c, w_ada, b_ada).reshape(DEPTH, B, 1, 3 * D_MODEL)
    cos, sin = _rope_tables(positions)
    w_in_b = jnp.concatenate([w_in[:, :, s:s + n] for s, n in _column_segments()],
                             axis=-1).astype(BF16)
    wpa_b = w_proj_attn.astype(BF16)
    wpr_b = w_proj_ret.astype(BF16)
    wo_b = w_out.astype(BF16)
    norm_w3 = norm_w.reshape(DEPTH, 1, D_MODEL)
    for l in range(DEPTH):
        proj = _inproj(x, mod[l], norm_w3, w_in_b, cos, sin, layer=l)
        ya = _attn(proj)
        yr = _ret(proj, ret_gn_w[l])
        x = _outproj(x, ya, yr, proj, mod[l], wpa_b, wpr_b, wo_b, final_norm_w, layer=l,
                     final_norm=(l == DEPTH - 1))
    return x
```

```python
import functools

import numpy as np
import jax
import jax.numpy as jnp
from jax import lax
from jax.experimental import pallas as pl
from jax.experimental.pallas import tpu as pltpu

D_MODEL = 1024
SEQ = 2048
DEPTH = 4
ATT_DILATIONS = (1, 4, 16)
N_ATT_GROUPS = 3
ATT_SLOTS = 4
ATT_HEAD_DIM = 128
ATT_BLOCK = 128
ATT_SLOTS_PER_STEP = 2
RET_HEADS = 4
RET_DIM = 256
RET_CHUNK = 256
RET_HEADS_PER_STEP = 2
ROPE_BASE = 10000.0
EPS = 1e-6
NEG_INF = -1e30

LANES = 128
COL_BLOCK = 512
BLOCKS_PER_STEP = COL_BLOCK // LANES
DOTS_PER_STEP = 2
ATT_QKV = N_ATT_GROUPS * ATT_SLOTS * ATT_HEAD_DIM
IN_WIDTH = 3 * ATT_QKV + ATT_SLOTS * ATT_HEAD_DIM + 4 * RET_HEADS * RET_DIM + 2 * D_MODEL
N_COL_STEPS = IN_WIDTH // COL_BLOCK
N_BLOCKS = IN_WIDTH // LANES


def _column_segments():
    grp = ATT_SLOTS * ATT_HEAD_DIM
    qkv = lambda t, g: (t * ATT_QKV + g * grp, grp)
    seg = [qkv(0, 0), qkv(1, 0), qkv(2, 0), (3 * ATT_QKV, IN_WIDTH - 3 * ATT_QKV)]
    for g in range(1, N_ATT_GROUPS):
        seg += [qkv(0, g), qkv(1, g), qkv(2, g)]
    return seg


_GRP_BLOCKS = ATT_SLOTS * ATT_HEAD_DIM // LANES
_WIDE_BLOCKS = RET_HEADS * RET_DIM // LANES
BLK_ZA = 3 * _GRP_BLOCKS
BLK_QR = BLK_ZA + _GRP_BLOCKS
BLK_KR = BLK_QR + _WIDE_BLOCKS
BLK_VR = BLK_KR + _WIDE_BLOCKS
BLK_ZR = BLK_VR + _WIDE_BLOCKS
BLK_GA = BLK_ZR + _WIDE_BLOCKS
BLK_GR = BLK_GA + D_MODEL // LANES
_BLK_G1 = BLK_GR + D_MODEL // LANES
BLK_Q = (0,) + tuple(_BLK_G1 + (g - 1) * 3 * _GRP_BLOCKS for g in range(1, N_ATT_GROUPS))
BLK_K = tuple(b + _GRP_BLOCKS for b in BLK_Q)
BLK_V = tuple(b + 2 * _GRP_BLOCKS for b in BLK_Q)
N_NATURAL_STEPS = _BLK_G1 // BLOCKS_PER_STEP
STEPS_PER_GROUP = 3 * _GRP_BLOCKS // BLOCKS_PER_STEP

VMEM_LIMIT = 60 * 1024 * 1024
F32 = jnp.float32
BF16 = jnp.bfloat16


def _params(*sem):
    return pltpu.CompilerParams(dimension_semantics=sem, vmem_limit_bytes=VMEM_LIMIT)


def _ada_kernel(c_ref, w_ref, b_ref, o_ref):
    c = c_ref[...]
    c_act = (c * jax.nn.sigmoid(c)).astype(BF16)
    acc = jnp.dot(c_act, w_ref[0].astype(BF16), preferred_element_type=F32)
    o_ref[0] = acc + b_ref[0]


def _ada(c, w_ada, b_ada):
    B = c.shape[0]
    n_col = 3 * D_MODEL // D_MODEL
    return pl.pallas_call(
        _ada_kernel,
        grid=(DEPTH, n_col),
        in_specs=[
            pl.BlockSpec((B, D_MODEL), lambda l, j: (0, 0)),
            pl.BlockSpec((1, D_MODEL, D_MODEL), lambda l, j: (l, 0, j)),
            pl.BlockSpec((1, 1, D_MODEL), lambda l, j: (l, 0, j)),
        ],
        out_specs=pl.BlockSpec((1, B, D_MODEL), lambda l, j: (l, 0, j)),
        out_shape=jax.ShapeDtypeStruct((DEPTH, B, 3 * D_MODEL), F32),
        compiler_params=_params("arbitrary", "arbitrary"),
        name="ada",
    )(c, w_ada, b_ada.reshape(DEPTH, 1, 3 * D_MODEL))


def _rope_kernel(pos_ref, theta_ref, cos_ref, sin_ref):
    ang = pos_ref[0].astype(F32) * theta_ref[...]
    cos_ref[0] = jnp.cos(ang)
    sin_ref[0] = jnp.sin(ang)


def _rope_tables(positions):
    B, S = positions.shape
    half = RET_DIM // 2
    theta = ROPE_BASE ** (-jnp.arange(half, dtype=F32) / half)
    spec = pl.BlockSpec((1, S, half), lambda b: (b, 0, 0))
    return pl.pallas_call(
        _rope_kernel,
        grid=(B,),
        in_specs=[pl.BlockSpec((1, S, 1), lambda b: (b, 0, 0)),
                  pl.BlockSpec((1, half), lambda b: (0, 0))],
        out_specs=[spec, spec],
        out_shape=[jax.ShapeDtypeStruct((B, S, half), F32)] * 2,
        compiler_params=_params("arbitrary"),
        name="rope",
    )(positions.reshape(B, S, 1), theta.reshape(1, half))


def _row_order_of_column_block(cb):
    return 0 if cb < N_NATURAL_STEPS else 1 + (cb - N_NATURAL_STEPS) // STEPS_PER_GROUP


def _rotary_scale_of_column_block(cb):
    blk = cb * BLOCKS_PER_STEP
    if BLK_QR <= blk < BLK_KR:
        return 1.0
    if BLK_KR <= blk < BLK_VR:
        return RET_DIM ** -0.5
    return None


def _inproj_kernel(x_ref, mod_ref, nw_ref, w_ref, cos_ref, sin_ref, o_ref,
                   h0_ref, h1_ref, h2_ref, hn_ref):
    j = pl.program_id(1)
    S = x_ref.shape[1]
    rows = 128
    n_lane_blocks = D_MODEL // LANES
    h_refs = (h0_ref, h1_ref, h2_ref)
    half = RET_DIM // 2

    def project(h_ref, i, rotary_scale):
        res = jnp.dot(h_ref[...], w_ref[0, :, i * COL_BLOCK:(i + 1) * COL_BLOCK],
                      preferred_element_type=F32)
        blocks = [res[:, c * LANES:(c + 1) * LANES] for c in range(BLOCKS_PER_STEP)]
        if rotary_scale is not None:
            cos, sin = cos_ref[0], sin_ref[0]
            if rotary_scale != 1.0:
                cos, sin = cos * rotary_scale, sin * rotary_scale
            for c in range(0, BLOCKS_PER_STEP, RET_DIM // LANES):
                t1, t2 = blocks[c], blocks[c + half // LANES]
                blocks[c] = t1 * cos - t2 * sin
                blocks[c + half // LANES] = t2 * cos + t1 * sin
        for c, blk in enumerate(blocks):
            o_ref[0, i * BLOCKS_PER_STEP + c] = blk.astype(BF16)

    @pl.when(j == 0)
    def _():
        shift = mod_ref[0, :, 0:D_MODEL]
        wmul = nw_ref[0] * (1.0 + mod_ref[0, :, D_MODEL:2 * D_MODEL])

        def natural(c, carry):
            idx = pl.ds(pl.multiple_of(c * rows, rows), rows)
            xs = x_ref[0, idx, :]
            ms = jnp.mean(xs * xs, axis=-1, keepdims=True)
            hv = xs * lax.rsqrt(ms + EPS) * wmul + shift
            h0_ref[idx, :] = hv.astype(BF16)
            for cb in range(n_lane_blocks):
                hn_ref[cb, idx, :] = hv[:, cb * LANES:(cb + 1) * LANES]
            return carry

        lax.fori_loop(0, S // rows, natural, 0, unroll=2)

        def permuted(c, carry):
            p0 = pl.multiple_of(c * rows, rows)
            for h_ref, d in zip(h_refs[1:], ATT_DILATIONS[1:]):
                sub_len = S // d
                start = (p0 % sub_len) * d + p0 // sub_len
                hv = jnp.concatenate(
                    [hn_ref[cb, pl.ds(start, rows, stride=d), :] for cb in range(n_lane_blocks)],
                    axis=-1)
                h_ref[pl.ds(p0, rows), :] = hv.astype(BF16)
            return carry

        lax.fori_loop(0, S // rows, permuted, 0, unroll=2)

    steps_by_kind = {}
    for s in range(N_COL_STEPS // DOTS_PER_STEP):
        cbs = [s * DOTS_PER_STEP + i for i in range(DOTS_PER_STEP)]
        kind = tuple((_row_order_of_column_block(cb), _rotary_scale_of_column_block(cb))
                     for cb in cbs)
        steps_by_kind.setdefault(kind, []).append(s)
    for kind, steps in steps_by_kind.items():
        @pl.when(functools.reduce(jnp.logical_or, [j == s for s in steps]))
        def _(kind=kind):
            for i, (order, rotary_scale) in enumerate(kind):
                project(h_refs[order], i, rotary_scale)


def _inproj(x, mod_l, norm_w, w_in, cos, sin, layer):
    B, S, D = x.shape
    tab = pl.BlockSpec((1, S, LANES), lambda b, j: (b, 0, 0))
    return pl.pallas_call(
        _inproj_kernel,
        grid=(B, N_COL_STEPS // DOTS_PER_STEP),
        in_specs=[
            pl.BlockSpec((1, S, D), lambda b, j: (b, 0, 0)),
            pl.BlockSpec((1, 1, 3 * D), lambda b, j: (b, 0, 0)),
            pl.BlockSpec((1, 1, D), lambda b, j: (layer, 0, 0)),
            pl.BlockSpec((1, D, DOTS_PER_STEP * COL_BLOCK), lambda b, j: (layer, 0, j)),
            tab, tab,
        ],
        out_specs=pl.BlockSpec((1, DOTS_PER_STEP * BLOCKS_PER_STEP, S, LANES),
                               lambda b, j: (b, j, 0, 0)),
        out_shape=jax.ShapeDtypeStruct((B, N_BLOCKS, S, LANES), BF16),
        scratch_shapes=[pltpu.VMEM((S, D), BF16)] * N_ATT_GROUPS
                       + [pltpu.VMEM((D // LANES, S, LANES), F32)],
        compiler_params=_params("arbitrary", "arbitrary"),
        name="inproj",
    )(x, mod_l, norm_w, w_in, cos, sin)


def _attn_kernel(q0, k0, v0, q1, k1, v1, q2, k2, v2, z_ref, y_ref,
                 sb_sc, sd_sc, m_sc, acc_sc, l_sc):
    S = y_ref.shape[2]
    T = ATT_BLOCK
    c_exp2 = ATT_HEAD_DIM ** -0.5 * np.log2(np.e)
    row = lax.broadcasted_iota(jnp.int32, (T, 2 * T), 0)
    col = lax.broadcasted_iota(jnp.int32, (T, 2 * T), 1)
    mask_band = (col >= row) & (col <= row + T)
    mask_diag = (lax.broadcasted_iota(jnp.int32, (T, T), 1)
                 <= lax.broadcasted_iota(jnp.int32, (T, T), 0))
    groups = ((q0, k0, v0), (q1, k1, v1), (q2, k2, v2))
    d1, d2 = ATT_DILATIONS[1], ATT_DILATIONS[2]
    len1 = S // d1
    nb1 = len1 // T
    n_band0 = S // T - 1

    def scores(sl, g, q_start, band, blk, out_idx):
        q_ref, k_ref, _ = groups[g]
        q = q_ref[0, sl, pl.ds(q_start, T), :]
        if band:
            k = k_ref[0, sl, pl.ds(q_start - T, 2 * T), :]
        else:
            k = k_ref[0, sl, pl.ds(q_start, T), :]
        s = lax.dot_general(q, k, (((1,), (1,)), ((), ())), preferred_element_type=F32) * c_exp2
        s = jnp.where(mask_band if band else mask_diag, s, NEG_INF)
        (sb_sc if band else sd_sc)[blk] = s
        m_sc[g, out_idx, :] = jnp.broadcast_to(jnp.max(s, axis=-1, keepdims=True), (T, LANES))

    def weighted_values(sl, g, q_start, band, blk, out_idx):
        v_ref = groups[g][2]
        m = m_sc[0, out_idx, :]
        if band:
            p = jnp.exp2(sb_sc[blk] - jnp.concatenate([m, m], axis=-1))
            v = v_ref[0, sl, pl.ds(q_start - T, 2 * T), :]
        else:
            p = jnp.exp2(sd_sc[blk] - m)
            v = v_ref[0, sl, pl.ds(q_start, T), :]
        v1 = jnp.concatenate([v, jnp.ones_like(v)], axis=-1)
        r = jnp.dot(p.astype(BF16), v1, preferred_element_type=F32)
        acc_sc[g, out_idx, :] = r[:, :LANES]
        l_sc[g, out_idx, :] = r[:, LANES:]

    blocks = [(0, 0, False, 0, pl.ds(0, T))]
    blocks += [(0, (i + 1) * T, True, i, pl.ds((i + 1) * T, T)) for i in range(n_band0)]
    for r in range(d1):
        blocks.append((1, r * len1, False, 1 + r, pl.ds(r, T, stride=d1)))
        blocks += [(1, r * len1 + n * T, True, n_band0 + r * (nb1 - 1) + n - 1,
                    pl.ds(n * T * d1 + r, T, stride=d1)) for n in range(1, nb1)]
    blocks += [(2, r * T, False, 1 + d1 + r, pl.ds(r, T, stride=d2)) for r in range(d2)]

    def all_blocks(fn):
        for blk in blocks:
            fn(*blk)

    rows = 256

    def shared_max(c, carry):
        idx = pl.ds(pl.multiple_of(c * rows, rows), rows)
        m_sc[0, idx, :] = jnp.maximum(jnp.maximum(m_sc[0, idx, :], m_sc[1, idx, :]), m_sc[2, idx, :])
        return carry

    def finish(sl, c, carry):
        idx = pl.ds(pl.multiple_of(c * rows, rows), rows)
        z = z_ref[0, sl, idx, :].astype(F32)
        o = ((acc_sc[0, idx, :] + acc_sc[1, idx, :] + acc_sc[2, idx, :])
             / (l_sc[0, idx, :] + l_sc[1, idx, :] + l_sc[2, idx, :]))
        y_ref[0, sl, idx, :] = (o * (z * jax.nn.sigmoid(z))).astype(BF16)
        return carry

    for sl in range(ATT_SLOTS_PER_STEP):
        all_blocks(functools.partial(scores, sl))
        lax.fori_loop(0, S // rows, shared_max, 0)
        all_blocks(functools.partial(weighted_values, sl))
        lax.fori_loop(0, S // rows, functools.partial(finish, sl), 0)


def _attn(proj):
    B, _, S, _ = proj.shape
    sps = ATT_SLOTS_PER_STEP

    def spec(base):
        return pl.BlockSpec((1, sps, S, LANES), lambda b, s: (b, base // sps + s, 0, 0))

    in_specs = []
    for g in range(N_ATT_GROUPS):
        in_specs += [spec(BLK_Q[g]), spec(BLK_K[g]), spec(BLK_V[g])]
    in_specs.append(spec(BLK_ZA))
    n_diag = sum(ATT_DILATIONS)
    n_band = N_ATT_GROUPS * S // ATT_BLOCK - n_diag
    return pl.pallas_call(
        _attn_kernel,
        grid=(B, ATT_SLOTS // sps),
        in_specs=in_specs,
        out_specs=pl.BlockSpec((1, sps, S, LANES), lambda b, s: (b, s, 0, 0)),
        out_shape=jax.ShapeDtypeStruct((B, ATT_SLOTS, S, LANES), BF16),
        scratch_shapes=[pltpu.VMEM((n_band, ATT_BLOCK, 2 * ATT_BLOCK), F32),
                        pltpu.VMEM((n_diag, ATT_BLOCK, ATT_BLOCK), F32),
                        pltpu.VMEM((N_ATT_GROUPS, S, LANES), F32),
                        pltpu.VMEM((N_ATT_GROUPS, S, LANES), F32),
                        pltpu.VMEM((N_ATT_GROUPS, S, LANES), F32)],
        compiler_params=_params("arbitrary", "arbitrary"),
        name="attn",
    )(*([proj] * 10))


def _ret_decay_tables():
    H, C = RET_HEADS, RET_CHUNK
    log_g = np.log1p(-np.exp2(-5.0 - np.arange(H, dtype=np.float64)))
    idx = np.arange(C, dtype=np.float64)
    diff = idx[:, None] - idx[None, :]
    inner = np.where(diff >= 0, np.exp(log_g[:, None, None] * np.maximum(diff, 0.0)), 0.0)
    q_decay = np.exp(log_g[:, None] * (idx + 1.0))
    k_decay = np.exp(log_g[:, None] * (C - 1.0 - idx))
    chunk_decay = np.exp(log_g * C)
    f = lambda a: jnp.asarray(a, dtype=F32)
    return f(inner), f(q_decay[:, :, None]), f(k_decay[:, :, None]), f(chunk_decay)


def _ret_kernel(cd_ref, q_ref, k_ref, v_ref, z_ref, inner_ref, qd_ref, kd_ref,
                gnw_ref, y_ref, intra_sc, u_sc, st_sc):
    S = y_ref.shape[2]
    C = RET_CHUNK
    n_chunks = S // C
    half = RET_DIM // 2
    per_head = RET_DIM // LANES

    def head(ref, hh, idx):
        return jnp.concatenate([ref[0, hh * per_head + i, idx, :] for i in range(per_head)],
                               axis=-1)

    def state_free(hh):
        for n in range(n_chunks):
            idx = pl.ds(n * C, C)
            qc, kc, vc = head(q_ref, hh, idx), head(k_ref, hh, idx), head(v_ref, hh, idx)
            att = lax.dot_general(qc, kc, (((1,), (1,)), ((), ())), preferred_element_type=F32)
            att = (att * inner_ref[hh]).astype(BF16)
            intra_sc[hh, idx, :] = jnp.dot(att, vc, preferred_element_type=F32)
            if n + 1 < n_chunks:
                kdt = (kc.astype(F32) * kd_ref[hh]).T.astype(BF16)
                u_sc[hh, n] = jnp.dot(kdt, vc, preferred_element_type=F32)

    def recurrence(hh):
        cd = cd_ref[pl.program_id(1) * RET_HEADS_PER_STEP + hh]
        slab = 64
        for r0 in range(0, RET_DIM, slab):
            state = jnp.zeros((slab, RET_DIM), F32)
            for n in range(1, n_chunks):
                state = state * cd + u_sc[hh, n - 1, r0:r0 + slab, :]
                st_sc[hh, n, r0:r0 + slab, :] = state.astype(BF16)

    def finish(hh):
        gnw = gnw_ref[:, hh * RET_DIM:(hh + 1) * RET_DIM]
        for n in range(n_chunks):
            idx = pl.ds(n * C, C)
            out = intra_sc[hh, idx, :]
            if n > 0:
                out = out + jnp.dot(head(q_ref, hh, idx), st_sc[hh, n],
                                    preferred_element_type=F32) * qd_ref[hh]
            mu = jnp.mean(out, axis=-1, keepdims=True)
            cen = out - mu
            var = jnp.mean(cen * cen, axis=-1, keepdims=True)
            o = cen * lax.rsqrt(var + EPS) * gnw
            z = head(z_ref, hh, idx).astype(F32)
            y = (o * (z * jax.nn.sigmoid(z))).astype(BF16)
            y_ref[0, hh * per_head, idx, :] = y[:, :half]
            y_ref[0, hh * per_head + 1, idx, :] = y[:, half:]

    for hh in range(RET_HEADS_PER_STEP):
        state_free(hh)
        if hh > 0:
            finish(hh - 1)
        recurrence(hh)
    finish(RET_HEADS_PER_STEP - 1)


def _ret(proj, gn_w_l):
    B, _, S, _ = proj.shape
    inner, q_decay, k_decay, chunk_decay = _ret_decay_tables()
    blocks = RET_HEADS_PER_STEP * RET_DIM // LANES
    hps = RET_HEADS_PER_STEP

    def spec(base):
        return pl.BlockSpec((1, blocks, S, LANES), lambda b, h: (b, base // blocks + h, 0, 0))

    return pl.pallas_call(
        _ret_kernel,
        grid=(B, RET_HEADS // hps),
        in_specs=[
            pl.BlockSpec(memory_space=pltpu.SMEM),
            spec(BLK_QR), spec(BLK_KR), spec(BLK_VR), spec(BLK_ZR),
            pl.BlockSpec((hps, RET_CHUNK, RET_CHUNK), lambda b, h: (h, 0, 0)),
            pl.BlockSpec((hps, RET_CHUNK, 1), lambda b, h: (h, 0, 0)),
            pl.BlockSpec((hps, RET_CHUNK, 1), lambda b, h: (h, 0, 0)),
            pl.BlockSpec((1, hps * RET_DIM), lambda b, h: (0, h)),
        ],
        out_specs=pl.BlockSpec((1, blocks, S, LANES), lambda b, h: (b, h, 0, 0)),
        out_shape=jax.ShapeDtypeStruct((B, RET_HEADS * RET_DIM // LANES, S, LANES), BF16),
        scratch_shapes=[pltpu.VMEM((hps, S, RET_DIM), F32),
                        pltpu.VMEM((hps, S // RET_CHUNK, RET_DIM, RET_DIM), F32),
                        pltpu.VMEM((hps, S // RET_CHUNK, RET_DIM, RET_DIM), BF16)],
        compiler_params=_params("arbitrary", "arbitrary"),
        name="retention",
    )(chunk_decay, proj, proj, proj, proj, inner, q_decay, k_decay,
      gn_w_l.reshape(1, RET_HEADS * RET_DIM))


def _outproj_kernel(ya_ref, yr_ref, ga_ref, gr_ref, x_ref, gate_ref, wpa_ref, wpr_ref, wo_ref,
                    fnw_ref, o_ref, *, final_norm):
    def cat(ref):
        return jnp.concatenate([ref[0, i] for i in range(ref.shape[1])], axis=-1)

    a = jnp.dot(cat(ya_ref), wpa_ref[0], preferred_element_type=F32)
    r = jnp.dot(cat(yr_ref), wpr_ref[0], preferred_element_type=F32)
    merged = (jax.nn.sigmoid(cat(ga_ref).astype(F32)) * a
              + jax.nn.sigmoid(cat(gr_ref).astype(F32)) * r)
    out = x_ref[0] + gate_ref[0] * jnp.dot(merged.astype(BF16), wo_ref[0],
                                           preferred_element_type=F32)
    if final_norm:
        ms = jnp.mean(out * out, axis=-1, keepdims=True)
        out = out * lax.rsqrt(ms + EPS) * fnw_ref[...]
    o_ref[0] = out


def _outproj(x, ya, yr, proj, mod_l, wpa, wpr, wo, final_norm_w, layer, final_norm):
    B, S, D = x.shape
    tm = 1024
    n_g = D // LANES
    weight = lambda w: pl.BlockSpec((1,) + w.shape[1:], lambda b, i: (layer, 0, 0))
    return pl.pallas_call(
        functools.partial(_outproj_kernel, final_norm=final_norm),
        grid=(B, S // tm),
        in_specs=[
            pl.BlockSpec((1, ya.shape[1], tm, LANES), lambda b, i: (b, 0, i, 0)),
            pl.BlockSpec((1, yr.shape[1], tm, LANES), lambda b, i: (b, 0, i, 0)),
            pl.BlockSpec((1, n_g, tm, LANES), lambda b, i: (b, BLK_GA // n_g, i, 0)),
            pl.BlockSpec((1, n_g, tm, LANES), lambda b, i: (b, BLK_GR // n_g, i, 0)),
            pl.BlockSpec((1, tm, D), lambda b, i: (b, i, 0)),
            pl.BlockSpec((1, 1, D), lambda b, i: (b, 0, 2)),
            weight(wpa), weight(wpr), weight(wo),
            pl.BlockSpec((1, D), lambda b, i: (0, 0)),
        ],
        out_specs=pl.BlockSpec((1, tm, D), lambda b, i: (b, i, 0)),
        out_shape=jax.ShapeDtypeStruct((B, S, D), F32),
        compiler_params=_params("arbitrary", "arbitrary"),
        name="outproj",
    )(ya, yr, proj, proj, x, mod_l, wpa, wpr, wo, final_norm_w.reshape(1, D))


def kernel(x, c, positions, norm_w, w_ada, b_ada, w_in, ret_gn_w, w_proj_attn, w_proj_ret, w_out,
           final_norm_w):
    B = x.shape[0]
    mod = _ada(c, w_ada, b_ada).reshape(DEPTH, B, 1, 3 * D_MODEL)
    cos, sin = _rope_tables(positions)
    w_in_b = jnp.concatenate([w_in[:, :, s:s + n] for s, n in _column_segments()],
                             axis=-1).astype(BF16)
    wpa_b = w_proj_attn.astype(BF16)
    wpr_b = w_proj_ret.astype(BF16)
    wo_b = w_out.astype(BF16)
    norm_w3 = norm_w.reshape(DEPTH, 1, D_MODEL)
    for l in range(DEPTH):
        proj = _inproj(x, mod[l], norm_w3, w_in_b, cos, sin, layer=l)
        ya = _attn(proj)
        yr = _ret(proj, ret_gn_w[l])
        x = _outproj(x, ya, yr, proj, mod[l], wpa_b, wpr_b, wo_b, final_norm_w, layer=l,
                     final_norm=(l == DEPTH - 1))
    return x
```

```python
import functools

import numpy as np
import jax
import jax.numpy as jnp
from jax import lax
from jax.experimental import pallas as pl
from jax.experimental.pallas import tpu as pltpu

D_MODEL = 1024
SEQ = 2048
DEPTH = 4
ATT_DILATIONS = (1, 4, 16)
N_ATT_GROUPS = 3
ATT_SLOTS = 4
ATT_HEAD_DIM = 128
ATT_BLOCK = 128
ATT_SLOTS_PER_STEP = 2
RET_HEADS = 4
RET_DIM = 256
RET_CHUNK = 256
RET_HEADS_PER_STEP = 2
ROPE_BASE = 10000.0
EPS = 1e-6
NEG_INF = -1e30

LANES = 128
COL_BLOCK = 512
BLOCKS_PER_STEP = COL_BLOCK // LANES
DOTS_PER_STEP = 2
ATT_QKV = N_ATT_GROUPS * ATT_SLOTS * ATT_HEAD_DIM
IN_WIDTH = 3 * ATT_QKV + ATT_SLOTS * ATT_HEAD_DIM + 4 * RET_HEADS * RET_DIM + 2 * D_MODEL
N_COL_STEPS = IN_WIDTH // COL_BLOCK
N_BLOCKS = IN_WIDTH // LANES


def _column_segments():
    grp = ATT_SLOTS * ATT_HEAD_DIM
    qkv = lambda t, g: (t * ATT_QKV + g * grp, grp)
    seg = [qkv(0, 0), qkv(1, 0), qkv(2, 0), (3 * ATT_QKV, IN_WIDTH - 3 * ATT_QKV)]
    for g in range(1, N_ATT_GROUPS):
        seg += [qkv(0, g), qkv(1, g), qkv(2, g)]
    return seg


_GRP_BLOCKS = ATT_SLOTS * ATT_HEAD_DIM // LANES
_WIDE_BLOCKS = RET_HEADS * RET_DIM // LANES
BLK_ZA = 3 * _GRP_BLOCKS
BLK_QR = BLK_ZA + _GRP_BLOCKS
BLK_KR = BLK_QR + _WIDE_BLOCKS
BLK_VR = BLK_KR + _WIDE_BLOCKS
BLK_ZR = BLK_VR + _WIDE_BLOCKS
BLK_GA = BLK_ZR + _WIDE_BLOCKS
BLK_GR = BLK_GA + D_MODEL // LANES
_BLK_G1 = BLK_GR + D_MODEL // LANES
BLK_Q = (0,) + tuple(_BLK_G1 + (g - 1) * 3 * _GRP_BLOCKS for g in range(1, N_ATT_GROUPS))
BLK_K = tuple(b + _GRP_BLOCKS for b in BLK_Q)
BLK_V = tuple(b + 2 * _GRP_BLOCKS for b in BLK_Q)
N_NATURAL_STEPS = _BLK_G1 // BLOCKS_PER_STEP
STEPS_PER_GROUP = 3 * _GRP_BLOCKS // BLOCKS_PER_STEP

VMEM_LIMIT = 60 * 1024 * 1024
F32 = jnp.float32
BF16 = jnp.bfloat16


def _params(*sem):
    return pltpu.CompilerParams(dimension_semantics=sem, vmem_limit_bytes=VMEM_LIMIT)


def _ada_kernel(c_ref, w_ref, b_ref, o_ref):
    c = c_ref[...]
    c_act = (c * jax.nn.sigmoid(c)).astype(BF16)
    acc = jnp.dot(c_act, w_ref[0].astype(BF16), preferred_element_type=F32)
    o_ref[0] = acc + b_ref[0]


def _ada(c, w_ada, b_ada):
    B = c.shape[0]
    n_col = 3 * D_MODEL // D_MODEL
    return pl.pallas_call(
        _ada_kernel,
        grid=(DEPTH, n_col),
        in_specs=[
            pl.BlockSpec((B, D_MODEL), lambda l, j: (0, 0)),
            pl.BlockSpec((1, D_MODEL, D_MODEL), lambda l, j: (l, 0, j)),
            pl.BlockSpec((1, 1, D_MODEL), lambda l, j: (l, 0, j)),
        ],
        out_specs=pl.BlockSpec((1, B, D_MODEL), lambda l, j: (l, 0, j)),
        out_shape=jax.ShapeDtypeStruct((DEPTH, B, 3 * D_MODEL), F32),
        compiler_params=_params("arbitrary", "arbitrary"),
        name="ada",
    )(c, w_ada, b_ada.reshape(DEPTH, 1, 3 * D_MODEL))


def _rope_kernel(pos_ref, theta_ref, cos_ref, sin_ref):
    ang = pos_ref[0].astype(F32) * theta_ref[...]
    cos_ref[0] = jnp.cos(ang)
    sin_ref[0] = jnp.sin(ang)


def _rope_tables(positions):
    B, S = positions.shape
    half = RET_DIM // 2
    theta = ROPE_BASE ** (-jnp.arange(half, dtype=F32) / half)
    spec = pl.BlockSpec((1, S, half), lambda b: (b, 0, 0))
    return pl.pallas_call(
        _rope_kernel,
        grid=(B,),
        in_specs=[pl.BlockSpec((1, S, 1), lambda b: (b, 0, 0)),
                  pl.BlockSpec((1, half), lambda b: (0, 0))],
        out_specs=[spec, spec],
        out_shape=[jax.ShapeDtypeStruct((B, S, half), F32)] * 2,
        compiler_params=_params("arbitrary"),
        name="rope",
    )(positions.reshape(B, S, 1), theta.reshape(1, half))


def _row_order_of_column_block(cb):
    return 0 if cb < N_NATURAL_STEPS else 1 + (cb - N_NATURAL_STEPS) // STEPS_PER_GROUP


def _rotary_scale_of_column_block(cb):
    blk = cb * BLOCKS_PER_STEP
    if BLK_QR <= blk < BLK_KR:
        return 1.0
    if BLK_KR <= blk < BLK_VR:
        return RET_DIM ** -0.5
    return None


def _inproj_kernel(x_hbm, mod_ref, nw_ref, w_ref, cos_ref, sin_ref, o_ref,
                   h0_ref, h1_ref, h2_ref, hn_ref, hp_ref, x_ref, x_sem):
    b, j = pl.program_id(0), pl.program_id(1)
    S = x_ref.shape[0]
    rows = 128

    def x_copy(seq):
        return pltpu.make_async_copy(x_hbm.at[seq], x_ref, x_sem)

    @pl.when((j == 0) & (b == 0))
    def _():
        x_copy(b).start()

    @pl.when((j == 1) & (b + 1 < pl.num_programs(0)))
    def _():
        x_copy(b + 1).start()

    n_lane_blocks = D_MODEL // LANES
    h_refs = (h0_ref, h1_ref, h2_ref)
    half = RET_DIM // 2

    def project(h_ref, i, rotary_scale):
        res = jnp.dot(h_ref[...], w_ref[0, :, i * COL_BLOCK:(i + 1) * COL_BLOCK],
                      preferred_element_type=F32)
        blocks = [res[:, c * LANES:(c + 1) * LANES] for c in range(BLOCKS_PER_STEP)]
        if rotary_scale is not None:
            cos, sin = cos_ref[0], sin_ref[0]
            if rotary_scale != 1.0:
                cos, sin = cos * rotary_scale, sin * rotary_scale
            for c in range(0, BLOCKS_PER_STEP, RET_DIM // LANES):
                t1, t2 = blocks[c], blocks[c + half // LANES]
                blocks[c] = t1 * cos - t2 * sin
                blocks[c + half // LANES] = t2 * cos + t1 * sin
        for c, blk in enumerate(blocks):
            o_ref[0, i * BLOCKS_PER_STEP + c] = blk.astype(BF16)

    @pl.when(j == 0)
    def _():
        x_copy(b).wait()
        shift = mod_ref[0, :, 0:D_MODEL]
        wmul = nw_ref[0] * (1.0 + mod_ref[0, :, D_MODEL:2 * D_MODEL])

        def natural(c, carry):
            idx = pl.ds(pl.multiple_of(c * rows, rows), rows)
            xs = x_ref[idx, :]
            ms = jnp.mean(xs * xs, axis=-1, keepdims=True)
            hv = xs * lax.rsqrt(ms + EPS) * wmul + shift
            h0_ref[idx, :] = hv.astype(BF16)
            for cb in range(n_lane_blocks):
                hn_ref[cb, idx, :] = hv[:, cb * LANES:(cb + 1) * LANES]
            return carry

        lax.fori_loop(0, S // rows, natural, 0, unroll=2)

        src_ref, src_d = hn_ref, 1
        for k, (h_ref, d) in enumerate(zip(h_refs[1:], ATT_DILATIONS[1:])):
            step = d // src_d
            n_src = S // step
            last = k == len(ATT_DILATIONS) - 2

            def permuted(c, carry, h_ref=h_ref, src_ref=src_ref, step=step, n_src=n_src,
                         last=last):
                p0 = pl.multiple_of(c * rows, rows)
                start = (p0 % n_src) * step + p0 // n_src
                pieces = [src_ref[cb, pl.ds(start, rows, stride=step), :]
                          for cb in range(n_lane_blocks)]
                h_ref[pl.ds(p0, rows), :] = jnp.concatenate(pieces, axis=-1).astype(BF16)
                if not last:
                    for cb in range(n_lane_blocks):
                        hp_ref[cb, pl.ds(p0, rows), :] = pieces[cb]
                return carry

            lax.fori_loop(0, S // rows, permuted, 0, unroll=2)
            src_ref, src_d = hp_ref, d

    steps_by_kind = {}
    for s in range(N_COL_STEPS // DOTS_PER_STEP):
        cbs = [s * DOTS_PER_STEP + i for i in range(DOTS_PER_STEP)]
        kind = tuple((_row_order_of_column_block(cb), _rotary_scale_of_column_block(cb))
                     for cb in cbs)
        steps_by_kind.setdefault(kind, []).append(s)
    for kind, steps in steps_by_kind.items():
        @pl.when(functools.reduce(jnp.logical_or, [j == s for s in steps]))
        def _(kind=kind):
            for i, (order, rotary_scale) in enumerate(kind):
                project(h_refs[order], i, rotary_scale)


def _inproj(x, mod_l, norm_w, w_in, cos, sin, layer):
    B, S, D = x.shape
    tab = pl.BlockSpec((1, S, LANES), lambda b, j: (b, 0, 0))
    return pl.pallas_call(
        _inproj_kernel,
        grid=(B, N_COL_STEPS // DOTS_PER_STEP),
        in_specs=[
            pl.BlockSpec(memory_space=pl.ANY),
            pl.BlockSpec((1, 1, 3 * D), lambda b, j: (b, 0, 0)),
            pl.BlockSpec((1, 1, D), lambda b, j: (layer, 0, 0)),
            pl.BlockSpec((1, D, DOTS_PER_STEP * COL_BLOCK), lambda b, j: (layer, 0, j)),
            tab, tab,
        ],
        out_specs=pl.BlockSpec((1, DOTS_PER_STEP * BLOCKS_PER_STEP, S, LANES),
                               lambda b, j: (b, j, 0, 0)),
        out_shape=jax.ShapeDtypeStruct((B, N_BLOCKS, S, LANES), BF16),
        scratch_shapes=[pltpu.VMEM((S, D), BF16)] * N_ATT_GROUPS
                       + [pltpu.VMEM((D // LANES, S, LANES), F32)] * 2
                       + [pltpu.VMEM((S, D), F32), pltpu.SemaphoreType.DMA(())],
        compiler_params=_params("arbitrary", "arbitrary"),
        name="inproj",
    )(x, mod_l, norm_w, w_in, cos, sin)


def _attn_kernel(q0, k0, v0, q1, k1, v1, q2, k2, v2, z_ref, y_ref,
                 sb_sc, sd_sc, m_sc, acc_sc, l_sc):
    S = y_ref.shape[2]
    T = ATT_BLOCK
    c_exp2 = ATT_HEAD_DIM ** -0.5 * np.log2(np.e)
    row = lax.broadcasted_iota(jnp.int32, (T, 2 * T), 0)
    col = lax.broadcasted_iota(jnp.int32, (T, 2 * T), 1)
    mask_band = (col >= row) & (col <= row + T)
    mask_diag = (lax.broadcasted_iota(jnp.int32, (T, T), 1)
                 <= lax.broadcasted_iota(jnp.int32, (T, T), 0))
    groups = ((q0, k0, v0), (q1, k1, v1), (q2, k2, v2))
    d1, d2 = ATT_DILATIONS[1], ATT_DILATIONS[2]
    len1 = S // d1
    nb1 = len1 // T
    n_band0 = S // T - 1

    def scores(sl, g, q_start, band, blk, out_idx):
        q_ref, k_ref, _ = groups[g]
        q = q_ref[0, sl, pl.ds(q_start, T), :]
        if band:
            k = k_ref[0, sl, pl.ds(q_start - T, 2 * T), :]
        else:
            k = k_ref[0, sl, pl.ds(q_start, T), :]
        s = lax.dot_general(q, k, (((1,), (1,)), ((), ())), preferred_element_type=F32) * c_exp2
        s = jnp.where(mask_band if band else mask_diag, s, NEG_INF)
        (sb_sc if band else sd_sc)[blk] = s
        m_sc[g, out_idx, :] = jnp.broadcast_to(jnp.max(s, axis=-1, keepdims=True), (T, LANES))

    def weighted_values(sl, g, q_start, band, blk, out_idx):
        v_ref = groups[g][2]
        m = m_sc[0, out_idx, :]
        if band:
            p = jnp.exp2(sb_sc[blk] - jnp.concatenate([m, m], axis=-1))
            v = v_ref[0, sl, pl.ds(q_start - T, 2 * T), :]
        else:
            p = jnp.exp2(sd_sc[blk] - m)
            v = v_ref[0, sl, pl.ds(q_start, T), :]
        v1 = jnp.concatenate([v, jnp.ones_like(v)], axis=-1)
        r = jnp.dot(p.astype(BF16), v1, preferred_element_type=F32)
        acc_sc[g, out_idx, :] = r[:, :LANES]
        l_sc[g, out_idx, :] = r[:, LANES:]

    blocks = [(0, 0, False, 0, pl.ds(0, T))]
    blocks += [(0, (i + 1) * T, True, i, pl.ds((i + 1) * T, T)) for i in range(n_band0)]
    for r in range(d1):
        blocks.append((1, r * len1, False, 1 + r, pl.ds(r, T, stride=d1)))
        blocks += [(1, r * len1 + n * T, True, n_band0 + r * (nb1 - 1) + n - 1,
                    pl.ds(n * T * d1 + r, T, stride=d1)) for n in range(1, nb1)]
    blocks += [(2, r * T, False, 1 + d1 + r, pl.ds(r, T, stride=d2)) for r in range(d2)]

    def all_blocks(fn):
        for blk in blocks:
            fn(*blk)

    rows = 256

    def shared_max(c, carry):
        idx = pl.ds(pl.multiple_of(c * rows, rows), rows)
        m_sc[0, idx, :] = jnp.maximum(jnp.maximum(m_sc[0, idx, :], m_sc[1, idx, :]), m_sc[2, idx, :])
        return carry

    def finish(sl, c, carry):
        idx = pl.ds(pl.multiple_of(c * rows, rows), rows)
        z = z_ref[0, sl, idx, :].astype(F32)
        o = ((acc_sc[0, idx, :] + acc_sc[1, idx, :] + acc_sc[2, idx, :])
             / (l_sc[0, idx, :] + l_sc[1, idx, :] + l_sc[2, idx, :]))
        y_ref[0, sl, idx, :] = (o * (z * jax.nn.sigmoid(z))).astype(BF16)
        return carry

    for sl in range(ATT_SLOTS_PER_STEP):
        all_blocks(functools.partial(scores, sl))
        lax.fori_loop(0, S // rows, shared_max, 0)
        all_blocks(functools.partial(weighted_values, sl))
        lax.fori_loop(0, S // rows, functools.partial(finish, sl), 0)


def _attn(proj):
    B, _, S, _ = proj.shape
    sps = ATT_SLOTS_PER_STEP

    def spec(base):
        return pl.BlockSpec((1, sps, S, LANES), lambda b, s: (b, base // sps + s, 0, 0))

    in_specs = []
    for g in range(N_ATT_GROUPS):
        in_specs += [spec(BLK_Q[g]), spec(BLK_K[g]), spec(BLK_V[g])]
    in_specs.append(spec(BLK_ZA))
    n_diag = sum(ATT_DILATIONS)
    n_band = N_ATT_GROUPS * S // ATT_BLOCK - n_diag
    return pl.pallas_call(
        _attn_kernel,
        grid=(B, ATT_SLOTS // sps),
        in_specs=in_specs,
        out_specs=pl.BlockSpec((1, sps, S, LANES), lambda b, s: (b, s, 0, 0)),
        out_shape=jax.ShapeDtypeStruct((B, ATT_SLOTS, S, LANES), BF16),
        scratch_shapes=[pltpu.VMEM((n_band, ATT_BLOCK, 2 * ATT_BLOCK), F32),
                        pltpu.VMEM((n_diag, ATT_BLOCK, ATT_BLOCK), F32),
                        pltpu.VMEM((N_ATT_GROUPS, S, LANES), F32),
                        pltpu.VMEM((N_ATT_GROUPS, S, LANES), F32),
                        pltpu.VMEM((N_ATT_GROUPS, S, LANES), F32)],
        compiler_params=_params("arbitrary", "arbitrary"),
        name="attn",
    )(*([proj] * 10))


def _ret_decay_tables():
    H, C = RET_HEADS, RET_CHUNK
    log_g = np.log1p(-np.exp2(-5.0 - np.arange(H, dtype=np.float64)))
    idx = np.arange(C, dtype=np.float64)
    diff = idx[:, None] - idx[None, :]
    inner = np.where(diff >= 0, np.exp(log_g[:, None, None] * np.maximum(diff, 0.0)), 0.0)
    q_decay = np.exp(log_g[:, None] * (idx + 1.0))
    k_decay = np.exp(log_g[:, None] * (C - 1.0 - idx))
    chunk_decay = np.exp(log_g * C)
    f = lambda a: jnp.asarray(a, dtype=F32)
    return f(inner), f(q_decay[:, :, None]), f(k_decay[:, :, None]), f(chunk_decay)


def _ret_kernel(cd_ref, q_ref, k_ref, v_ref, z_ref, inner_ref, qd_ref, kd_ref,
                gnw_ref, y_ref, intra_sc, u_sc, st_sc):
    S = y_ref.shape[2]
    C = RET_CHUNK
    n_chunks = S // C
    half = RET_DIM // 2
    per_head = RET_DIM // LANES

    def head(ref, hh, idx):
        return jnp.concatenate([ref[0, hh * per_head + i, idx, :] for i in range(per_head)],
                               axis=-1)

    def state_free(hh):
        for n in range(n_chunks):
            idx = pl.ds(n * C, C)
            qc, kc, vc = head(q_ref, hh, idx), head(k_ref, hh, idx), head(v_ref, hh, idx)
            att = lax.dot_general(qc, kc, (((1,), (1,)), ((), ())), preferred_element_type=F32)
            att = (att * inner_ref[hh]).astype(BF16)
            intra_sc[hh, idx, :] = jnp.dot(att, vc, preferred_element_type=F32)
            if n + 1 < n_chunks:
                kdt = (kc.astype(F32) * kd_ref[hh]).T.astype(BF16)
                u_sc[hh, n] = jnp.dot(kdt, vc, preferred_element_type=F32)

    def recurrence(hh):
        cd = cd_ref[pl.program_id(1) * RET_HEADS_PER_STEP + hh]
        slab = 64
        for r0 in range(0, RET_DIM, slab):
            state = jnp.zeros((slab, RET_DIM), F32)
            for n in range(1, n_chunks):
                state = state * cd + u_sc[hh, n - 1, r0:r0 + slab, :]
                st_sc[hh, n, r0:r0 + slab, :] = state.astype(BF16)

    def finish(hh):
        gnw = gnw_ref[:, hh * RET_DIM:(hh + 1) * RET_DIM]
        for n in range(n_chunks):
            idx = pl.ds(n * C, C)
            out = intra_sc[hh, idx, :]
            if n > 0:
                out = out + jnp.dot(head(q_ref, hh, idx), st_sc[hh, n],
                                    preferred_element_type=F32) * qd_ref[hh]
            mu = jnp.mean(out, axis=-1, keepdims=True)
            cen = out - mu
            var = jnp.mean(cen * cen, axis=-1, keepdims=True)
            o = cen * lax.rsqrt(var + EPS) * gnw
            z = head(z_ref, hh, idx).astype(F32)
            y = (o * (z * jax.nn.sigmoid(z))).astype(BF16)
            y_ref[0, hh * per_head, idx, :] = y[:, :half]
            y_ref[0, hh * per_head + 1, idx, :] = y[:, half:]

    for hh in range(RET_HEADS_PER_STEP):
        state_free(hh)
        if hh > 0:
            finish(hh - 1)
        recurrence(hh)
    finish(RET_HEADS_PER_STEP - 1)


def _ret(proj, gn_w_l):
    B, _, S, _ = proj.shape
    inner, q_decay, k_decay, chunk_decay = _ret_decay_tables()
    blocks = RET_HEADS_PER_STEP * RET_DIM // LANES
    hps = RET_HEADS_PER_STEP

    def spec(base):
        return pl.BlockSpec((1, blocks, S, LANES), lambda b, h: (b, base // blocks + h, 0, 0))

    return pl.pallas_call(
        _ret_kernel,
        grid=(B, RET_HEADS // hps),
        in_specs=[
            pl.BlockSpec(memory_space=pltpu.SMEM),
            spec(BLK_QR), spec(BLK_KR), spec(BLK_VR), spec(BLK_ZR),
            pl.BlockSpec((hps, RET_CHUNK, RET_CHUNK), lambda b, h: (h, 0, 0)),
            pl.BlockSpec((hps, RET_CHUNK, 1), lambda b, h: (h, 0, 0)),
            pl.BlockSpec((hps, RET_CHUNK, 1), lambda b, h: (h, 0, 0)),
            pl.BlockSpec((1, hps * RET_DIM), lambda b, h: (0, h)),
        ],
        out_specs=pl.BlockSpec((1, blocks, S, LANES), lambda b, h: (b, h, 0, 0)),
        out_shape=jax.ShapeDtypeStruct((B, RET_HEADS * RET_DIM // LANES, S, LANES), BF16),
        scratch_shapes=[pltpu.VMEM((hps, S, RET_DIM), F32),
                        pltpu.VMEM((hps, S // RET_CHUNK, RET_DIM, RET_DIM), F32),
                        pltpu.VMEM((hps, S // RET_CHUNK, RET_DIM, RET_DIM), BF16)],
        compiler_params=_params("arbitrary", "arbitrary"),
        name="retention",
    )(chunk_decay, proj, proj, proj, proj, inner, q_decay, k_decay,
      gn_w_l.reshape(1, RET_HEADS * RET_DIM))


def _outproj_kernel(ya_ref, yr_ref, ga_ref, gr_ref, x_ref, gate_ref, wpa_ref, wpr_ref, wo_ref,
                    fnw_ref, o_ref, *, final_norm):
    def cat(ref):
        return jnp.concatenate([ref[0, i] for i in range(ref.shape[1])], axis=-1)

    a = jnp.dot(cat(ya_ref), wpa_ref[0], preferred_element_type=F32)
    r = jnp.dot(cat(yr_ref), wpr_ref[0], preferred_element_type=F32)
    merged = (jax.nn.sigmoid(cat(ga_ref).astype(F32)) * a
              + jax.nn.sigmoid(cat(gr_ref).astype(F32)) * r)
    out = x_ref[0] + gate_ref[0] * jnp.dot(merged.astype(BF16), wo_ref[0],
                                           preferred_element_type=F32)
    if final_norm:
        ms = jnp.mean(out * out, axis=-1, keepdims=True)
        out = out * lax.rsqrt(ms + EPS) * fnw_ref[...]
    o_ref[0] = out


def _outproj(x, ya, yr, proj, mod_l, wpa, wpr, wo, final_norm_w, layer, final_norm):
    B, S, D = x.shape
    tm = 1024
    n_g = D // LANES
    weight = lambda w: pl.BlockSpec((1,) + w.shape[1:], lambda b, i: (layer, 0, 0))
    return pl.pallas_call(
        functools.partial(_outproj_kernel, final_norm=final_norm),
        grid=(B, S // tm),
        in_specs=[
            pl.BlockSpec((1, ya.shape[1], tm, LANES), lambda b, i: (b, 0, i, 0)),
            pl.BlockSpec((1, yr.shape[1], tm, LANES), lambda b, i: (b, 0, i, 0)),
            pl.BlockSpec((1, n_g, tm, LANES), lambda b, i: (b, BLK_GA // n_g, i, 0)),
            pl.BlockSpec((1, n_g, tm, LANES), lambda b, i: (b, BLK_GR // n_g, i, 0)),
            pl.BlockSpec((1, tm, D), lambda b, i: (b, i, 0)),
            pl.BlockSpec((1, 1, D), lambda b, i: (b, 0, 2)),
            weight(wpa), weight(wpr), weight(wo),
            pl.BlockSpec((1, D), lambda b, i: (0, 0)),
        ],
        out_specs=pl.BlockSpec((1, tm, D), lambda b, i: (b, i, 0)),
        out_shape=jax.ShapeDtypeStruct((B, S, D), F32),
        compiler_params=_params("arbitrary", "arbitrary"),
        name="outproj",
    )(ya, yr, proj, proj, x, mod_l, wpa, wpr, wo, final_norm_w.reshape(1, D))


def kernel(x, c, positions, norm_w, w_ada, b_ada, w_in, ret_gn_w, w_proj_attn, w_proj_ret, w_out,
           final_norm_w):
    B = x.shape[0]
    mod = _ada(c, w_ada, b_ada).reshape(DEPTH, B, 1, 3 * D_MODEL)
    cos, sin = _rope_tables(positions)
    w_in_b = jnp.concatenate([w_in[:, :, s:s + n] for s, n in _column_segments()],
                             axis=-1).astype(BF16)
    wpa_b = w_proj_attn.astype(BF16)
    wpr_b = w_proj_ret.astype(BF16)
    wo_b = w_out.astype(BF16)
    norm_w3 = norm_w.reshape(DEPTH, 1, D_MODEL)
    for l in range(DEPTH):
        proj = _inproj(x, mod[l], norm_w3, w_in_b, cos, sin, layer=l)
        ya = _attn(proj)
        yr = _ret(proj, ret_gn_w[l])
        x = _outproj(x, ya, yr, proj, mod[l], wpa_b, wpr_b, wo_b, final_norm_w, layer=l,
                     final_norm=(l == DEPTH - 1))
    return x
```

```python
import functools

import numpy as np
import jax
import jax.numpy as jnp
from jax import lax
from jax.experimental import pallas as pl
from jax.experimental.pallas import tpu as pltpu

D_MODEL = 1024
SEQ = 2048
DEPTH = 4
ATT_DILATIONS = (1, 4, 16)
N_ATT_GROUPS = 3
ATT_SLOTS = 4
ATT_HEAD_DIM = 128
ATT_BLOCK = 128
ATT_SLOTS_PER_STEP = 2
RET_HEADS = 4
RET_DIM = 256
RET_CHUNK = 256
RET_HEADS_PER_STEP = 2
ROPE_BASE = 10000.0
EPS = 1e-6
NEG_INF = -1e30

LANES = 128
COL_BLOCK = 512
BLOCKS_PER_STEP = COL_BLOCK // LANES
DOTS_PER_STEP = 2
ATT_QKV = N_ATT_GROUPS * ATT_SLOTS * ATT_HEAD_DIM
IN_WIDTH = 3 * ATT_QKV + ATT_SLOTS * ATT_HEAD_DIM + 4 * RET_HEADS * RET_DIM + 2 * D_MODEL
N_COL_STEPS = IN_WIDTH // COL_BLOCK
N_BLOCKS = IN_WIDTH // LANES


def _column_segments():
    grp = ATT_SLOTS * ATT_HEAD_DIM
    qkv = lambda t, g: (t * ATT_QKV + g * grp, grp)
    seg = [qkv(0, 0), qkv(1, 0), qkv(2, 0), (3 * ATT_QKV, IN_WIDTH - 3 * ATT_QKV)]
    for g in range(1, N_ATT_GROUPS):
        seg += [qkv(0, g), qkv(1, g), qkv(2, g)]
    return seg


_GRP_BLOCKS = ATT_SLOTS * ATT_HEAD_DIM // LANES
_WIDE_BLOCKS = RET_HEADS * RET_DIM // LANES
BLK_ZA = 3 * _GRP_BLOCKS
BLK_QR = BLK_ZA + _GRP_BLOCKS
BLK_KR = BLK_QR + _WIDE_BLOCKS
BLK_VR = BLK_KR + _WIDE_BLOCKS
BLK_ZR = BLK_VR + _WIDE_BLOCKS
BLK_GA = BLK_ZR + _WIDE_BLOCKS
BLK_GR = BLK_GA + D_MODEL // LANES
_BLK_G1 = BLK_GR + D_MODEL // LANES
BLK_Q = (0,) + tuple(_BLK_G1 + (g - 1) * 3 * _GRP_BLOCKS for g in range(1, N_ATT_GROUPS))
BLK_K = tuple(b + _GRP_BLOCKS for b in BLK_Q)
BLK_V = tuple(b + 2 * _GRP_BLOCKS for b in BLK_Q)
N_NATURAL_STEPS = _BLK_G1 // BLOCKS_PER_STEP
STEPS_PER_GROUP = 3 * _GRP_BLOCKS // BLOCKS_PER_STEP

VMEM_LIMIT = 60 * 1024 * 1024
F32 = jnp.float32
BF16 = jnp.bfloat16


def _params(*sem):
    return pltpu.CompilerParams(dimension_semantics=sem, vmem_limit_bytes=VMEM_LIMIT)


def _ada_kernel(c_ref, w_ref, b_ref, o_ref):
    c = c_ref[...]
    c_act = (c * jax.nn.sigmoid(c)).astype(BF16)
    acc = jnp.dot(c_act, w_ref[0].astype(BF16), preferred_element_type=F32)
    o_ref[0] = acc + b_ref[0]


def _ada(c, w_ada, b_ada):
    B = c.shape[0]
    n_col = 3 * D_MODEL // D_MODEL
    return pl.pallas_call(
        _ada_kernel,
        grid=(DEPTH, n_col),
        in_specs=[
            pl.BlockSpec((B, D_MODEL), lambda l, j: (0, 0)),
            pl.BlockSpec((1, D_MODEL, D_MODEL), lambda l, j: (l, 0, j)),
            pl.BlockSpec((1, 1, D_MODEL), lambda l, j: (l, 0, j)),
        ],
        out_specs=pl.BlockSpec((1, B, D_MODEL), lambda l, j: (l, 0, j)),
        out_shape=jax.ShapeDtypeStruct((DEPTH, B, 3 * D_MODEL), F32),
        compiler_params=_params("arbitrary", "arbitrary"),
        name="ada",
    )(c, w_ada, b_ada.reshape(DEPTH, 1, 3 * D_MODEL))


def _rope_kernel(pos_ref, theta_ref, cos_ref, sin_ref):
    ang = pos_ref[0].astype(F32) * theta_ref[...]
    cos_ref[0] = jnp.cos(ang)
    sin_ref[0] = jnp.sin(ang)


def _rope_tables(positions):
    B, S = positions.shape
    half = RET_DIM // 2
    theta = ROPE_BASE ** (-jnp.arange(half, dtype=F32) / half)
    spec = pl.BlockSpec((1, S, half), lambda b: (b, 0, 0))
    return pl.pallas_call(
        _rope_kernel,
        grid=(B,),
        in_specs=[pl.BlockSpec((1, S, 1), lambda b: (b, 0, 0)),
                  pl.BlockSpec((1, half), lambda b: (0, 0))],
        out_specs=[spec, spec],
        out_shape=[jax.ShapeDtypeStruct((B, S, half), F32)] * 2,
        compiler_params=_params("arbitrary"),
        name="rope",
    )(positions.reshape(B, S, 1), theta.reshape(1, half))


def _row_order_of_column_block(cb):
    return 0 if cb < N_NATURAL_STEPS else 1 + (cb - N_NATURAL_STEPS) // STEPS_PER_GROUP


def _rotary_scale_of_column_block(cb):
    blk = cb * BLOCKS_PER_STEP
    if BLK_QR <= blk < BLK_KR:
        return 1.0
    if BLK_KR <= blk < BLK_VR:
        return RET_DIM ** -0.5
    return None


def _inproj_kernel(x_hbm, mod_ref, nw_ref, w_ref, cos_ref, sin_ref, o_ref,
                   h0_ref, h1_ref, h2_ref, hn_ref, hp_ref, x_ref, x_sem):
    b, j = pl.program_id(0), pl.program_id(1)
    S = x_ref.shape[0]
    rows = 128

    def x_copy(seq):
        return pltpu.make_async_copy(x_hbm.at[seq], x_ref, x_sem)

    @pl.when((j == 0) & (b == 0))
    def _():
        x_copy(b).start()

    @pl.when((j == 1) & (b + 1 < pl.num_programs(0)))
    def _():
        x_copy(b + 1).start()

    n_lane_blocks = D_MODEL // LANES
    h_refs = (h0_ref, h1_ref, h2_ref)
    half = RET_DIM // 2

    def project(h_ref, i, rotary_scale):
        res = jnp.dot(h_ref[...], w_ref[0, :, i * COL_BLOCK:(i + 1) * COL_BLOCK],
                      preferred_element_type=F32)
        blocks = [res[:, c * LANES:(c + 1) * LANES] for c in range(BLOCKS_PER_STEP)]
        if rotary_scale is not None:
            cos, sin = cos_ref[0], sin_ref[0]
            if rotary_scale != 1.0:
                cos, sin = cos * rotary_scale, sin * rotary_scale
            for c in range(0, BLOCKS_PER_STEP, RET_DIM // LANES):
                t1, t2 = blocks[c], blocks[c + half // LANES]
                blocks[c] = t1 * cos - t2 * sin
                blocks[c + half // LANES] = t2 * cos + t1 * sin
        for c, blk in enumerate(blocks):
            o_ref[0, i * BLOCKS_PER_STEP + c] = blk.astype(BF16)

    @pl.when(j == 0)
    def _():
        x_copy(b).wait()
        shift = mod_ref[0, :, 0:D_MODEL]
        wmul = nw_ref[0] * (1.0 + mod_ref[0, :, D_MODEL:2 * D_MODEL])

        def natural(c, carry):
            idx = pl.ds(pl.multiple_of(c * rows, rows), rows)
            xs = x_ref[idx, :]
            ms = jnp.mean(xs * xs, axis=-1, keepdims=True)
            hv = xs * lax.rsqrt(ms + EPS) * wmul + shift
            h0_ref[idx, :] = hv.astype(BF16)
            for cb in range(n_lane_blocks):
                hn_ref[cb, idx, :] = hv[:, cb * LANES:(cb + 1) * LANES]
            return carry

        lax.fori_loop(0, S // rows, natural, 0, unroll=2)

        src_ref, src_d = hn_ref, 1
        for k, (h_ref, d) in enumerate(zip(h_refs[1:], ATT_DILATIONS[1:])):
            step = d // src_d
            n_src = S // step
            last = k == len(ATT_DILATIONS) - 2

            def permuted(c, carry, h_ref=h_ref, src_ref=src_ref, step=step, n_src=n_src,
                         last=last):
                p0 = pl.multiple_of(c * rows, rows)
                start = (p0 % n_src) * step + p0 // n_src
                pieces = [src_ref[cb, pl.ds(start, rows, stride=step), :]
                          for cb in range(n_lane_blocks)]
                h_ref[pl.ds(p0, rows), :] = jnp.concatenate(pieces, axis=-1).astype(BF16)
                if not last:
                    for cb in range(n_lane_blocks):
                        hp_ref[cb, pl.ds(p0, rows), :] = pieces[cb]
                return carry

            lax.fori_loop(0, S // rows, permuted, 0, unroll=2)
            src_ref, src_d = hp_ref, d

    steps_by_kind = {}
    for s in range(N_COL_STEPS // DOTS_PER_STEP):
        cbs = [s * DOTS_PER_STEP + i for i in range(DOTS_PER_STEP)]
        kind = tuple((_row_order_of_column_block(cb), _rotary_scale_of_column_block(cb))
                     for cb in cbs)
        steps_by_kind.setdefault(kind, []).append(s)
    for kind, steps in steps_by_kind.items():
        @pl.when(functools.reduce(jnp.logical_or, [j == s for s in steps]))
        def _(kind=kind):
            for i, (order, rotary_scale) in enumerate(kind):
                project(h_refs[order], i, rotary_scale)


def _inproj(x, mod_l, norm_w, w_in, cos, sin, layer):
    B, S, D = x.shape
    tab = pl.BlockSpec((1, S, LANES), lambda b, j: (b, 0, 0))
    return pl.pallas_call(
        _inproj_kernel,
        grid=(B, N_COL_STEPS // DOTS_PER_STEP),
        in_specs=[
            pl.BlockSpec(memory_space=pl.ANY),
            pl.BlockSpec((1, 1, 3 * D), lambda b, j: (b, 0, 0)),
            pl.BlockSpec((1, 1, D), lambda b, j: (layer, 0, 0)),
            pl.BlockSpec((1, D, DOTS_PER_STEP * COL_BLOCK), lambda b, j: (layer, 0, j)),
            tab, tab,
        ],
        out_specs=pl.BlockSpec((1, DOTS_PER_STEP * BLOCKS_PER_STEP, S, LANES),
                               lambda b, j: (b, j, 0, 0)),
        out_shape=jax.ShapeDtypeStruct((B, N_BLOCKS, S, LANES), BF16),
        scratch_shapes=[pltpu.VMEM((S, D), BF16)] * N_ATT_GROUPS
                       + [pltpu.VMEM((D // LANES, S, LANES), F32)] * 2
                       + [pltpu.VMEM((S, D), F32), pltpu.SemaphoreType.DMA(())],
        compiler_params=_params("arbitrary", "arbitrary"),
        name="inproj",
    )(x, mod_l, norm_w, w_in, cos, sin)


def _attn_kernel(q0, k0, v0, q1, k1, v1, q2, k2, v2, z_ref, y_ref,
                 sb_sc, sd_sc, m_sc, acc_sc, l_sc):
    S = y_ref.shape[2]
    T = ATT_BLOCK
    c_exp2 = ATT_HEAD_DIM ** -0.5 * np.log2(np.e)
    row = lax.broadcasted_iota(jnp.int32, (T, 2 * T), 0)
    col = lax.broadcasted_iota(jnp.int32, (T, 2 * T), 1)
    mask_band = (col >= row) & (col <= row + T)
    mask_diag = (lax.broadcasted_iota(jnp.int32, (T, T), 1)
                 <= lax.broadcasted_iota(jnp.int32, (T, T), 0))
    groups = ((q0, k0, v0), (q1, k1, v1), (q2, k2, v2))
    d1, d2 = ATT_DILATIONS[1], ATT_DILATIONS[2]
    len1 = S // d1
    nb1 = len1 // T
    n_band0 = S // T - 1

    def scores(sl, g, q_start, band, blk, out_idx, m_idx):
        q_ref, k_ref, _ = groups[g]
        q = q_ref[0, sl, pl.ds(q_start, T), :]
        if band:
            k = k_ref[0, sl, pl.ds(q_start - T, 2 * T), :]
        else:
            k = k_ref[0, sl, pl.ds(q_start, T), :]
        s = lax.dot_general(q, k, (((1,), (1,)), ((), ())), preferred_element_type=F32) * c_exp2
        s = jnp.where(mask_band if band else mask_diag, s, NEG_INF)
        (sb_sc if band else sd_sc)[blk] = s
        m_sc[g, out_idx, :] = jnp.broadcast_to(jnp.max(s, axis=-1, keepdims=True), (T, LANES))

    def weighted_values(sl, g, q_start, band, blk, out_idx, m_idx):
        v_ref = groups[g][2]
        m = m_sc[min(g, 1), m_idx, :]
        if band:
            p = jnp.exp2(sb_sc[blk] - jnp.concatenate([m, m], axis=-1))
            v = v_ref[0, sl, pl.ds(q_start - T, 2 * T), :]
        else:
            p = jnp.exp2(sd_sc[blk] - m)
            v = v_ref[0, sl, pl.ds(q_start, T), :]
        v1 = jnp.concatenate([v, jnp.ones_like(v)], axis=-1)
        r = jnp.dot(p.astype(BF16), v1, preferred_element_type=F32)
        acc_sc[g, out_idx, :] = r[:, :LANES]
        l_sc[g, out_idx, :] = r[:, LANES:]

    blocks = []
    for i in range(S // T):
        blocks.append((0, i * T, i > 0, max(i - 1, 0), pl.ds(i * T, T), pl.ds(i * T, T)))
    for r in range(d1):
        for n in range(nb1):
            idx = pl.ds(r * len1 + n * T, T)
            blk = 1 + r if n == 0 else n_band0 + r * (nb1 - 1) + n - 1
            blocks.append((1, r * len1 + n * T, n > 0, blk, idx, idx))
    ratio = d2 // d1
    for r in range(d2):
        idx = pl.ds((r % d1) * len1 + r // d1, T, stride=ratio)
        blocks.append((2, r * T, False, 1 + d1 + r, idx, idx))

    def all_blocks(fn):
        for blk in blocks:
            fn(*blk)

    rows = 256

    def natural_rows(c):
        p0 = pl.multiple_of(c * rows, rows)
        return pl.ds(p0, rows), pl.ds((p0 % len1) * d1 + p0 // len1, rows, stride=d1)

    def shared_max(c, carry):
        idx, nat = natural_rows(c)
        m = jnp.maximum(jnp.maximum(m_sc[0, nat, :], m_sc[1, idx, :]), m_sc[2, idx, :])
        m_sc[0, nat, :] = m
        m_sc[1, idx, :] = m
        return carry

    def combine(c, carry):
        idx, nat = natural_rows(c)
        o = ((acc_sc[0, nat, :] + acc_sc[1, idx, :] + acc_sc[2, idx, :])
             / (l_sc[0, nat, :] + l_sc[1, idx, :] + l_sc[2, idx, :]))
        acc_sc[0, nat, :] = o
        return carry

    def finish(sl, c, carry):
        idx = pl.ds(pl.multiple_of(c * rows, rows), rows)
        z = z_ref[0, sl, idx, :].astype(F32)
        y_ref[0, sl, idx, :] = (acc_sc[0, idx, :] * (z * jax.nn.sigmoid(z))).astype(BF16)
        return carry

    for sl in range(ATT_SLOTS_PER_STEP):
        all_blocks(functools.partial(scores, sl))
        lax.fori_loop(0, S // rows, shared_max, 0, unroll=True)
        all_blocks(functools.partial(weighted_values, sl))
        lax.fori_loop(0, S // rows, combine, 0, unroll=True)
        lax.fori_loop(0, S // rows, functools.partial(finish, sl), 0)


def _attn(proj):
    B, _, S, _ = proj.shape
    sps = ATT_SLOTS_PER_STEP

    def spec(base):
        return pl.BlockSpec((1, sps, S, LANES), lambda b, s: (b, base // sps + s, 0, 0))

    in_specs = []
    for g in range(N_ATT_GROUPS):
        in_specs += [spec(BLK_Q[g]), spec(BLK_K[g]), spec(BLK_V[g])]
    in_specs.append(spec(BLK_ZA))
    n_diag = sum(ATT_DILATIONS)
    n_band = N_ATT_GROUPS * S // ATT_BLOCK - n_diag
    return pl.pallas_call(
        _attn_kernel,
        grid=(B, ATT_SLOTS // sps),
        in_specs=in_specs,
        out_specs=pl.BlockSpec((1, sps, S, LANES), lambda b, s: (b, s, 0, 0)),
        out_shape=jax.ShapeDtypeStruct((B, ATT_SLOTS, S, LANES), BF16),
        scratch_shapes=[pltpu.VMEM((n_band, ATT_BLOCK, 2 * ATT_BLOCK), F32),
                        pltpu.VMEM((n_diag, ATT_BLOCK, ATT_BLOCK), F32),
                        pltpu.VMEM((N_ATT_GROUPS, S, LANES), F32),
                        pltpu.VMEM((N_ATT_GROUPS, S, LANES), F32),
                        pltpu.VMEM((N_ATT_GROUPS, S, LANES), F32)],
        compiler_params=_params("arbitrary", "arbitrary"),
        name="attn",
    )(*([proj] * 10))


def _ret_decay_tables():
    H, C = RET_HEADS, RET_CHUNK
    log_g = np.log1p(-np.exp2(-5.0 - np.arange(H, dtype=np.float64)))
    idx = np.arange(C, dtype=np.float64)
    diff = idx[:, None] - idx[None, :]
    inner = np.where(diff >= 0, np.exp(log_g[:, None, None] * np.maximum(diff, 0.0)), 0.0)
    q_decay = np.exp(log_g[:, None] * (idx + 1.0))
    k_decay = np.exp(log_g[:, None] * (C - 1.0 - idx))
    chunk_decay = np.exp(log_g * C)
    f = lambda a: jnp.asarray(a, dtype=F32)
    return f(inner), f(q_decay[:, :, None]), f(k_decay[:, :, None]), f(chunk_decay)


def _ret_kernel(cd_ref, q_ref, k_ref, v_ref, z_ref, inner_ref, qd_ref, kd_ref,
                gnw_ref, y_ref, intra_sc, u_sc, st_sc):
    S = y_ref.shape[2]
    C = RET_CHUNK
    n_chunks = S // C
    half = RET_DIM // 2
    per_head = RET_DIM // LANES

    def head(ref, hh, idx):
        return jnp.concatenate([ref[0, hh * per_head + i, idx, :] for i in range(per_head)],
                               axis=-1)

    def state_free(hh):
        for n in range(n_chunks):
            idx = pl.ds(n * C, C)
            qc, kc, vc = head(q_ref, hh, idx), head(k_ref, hh, idx), head(v_ref, hh, idx)
            att = lax.dot_general(qc, kc, (((1,), (1,)), ((), ())), preferred_element_type=F32)
            att = (att * inner_ref[hh]).astype(BF16)
            intra_sc[hh, idx, :] = jnp.dot(att, vc, preferred_element_type=F32)
            if n + 1 < n_chunks:
                kdt = (kc.astype(F32) * kd_ref[hh]).T.astype(BF16)
                u_sc[hh, n] = jnp.dot(kdt, vc, preferred_element_type=F32)

    def recurrence(hh):
        cd = cd_ref[pl.program_id(1) * RET_HEADS_PER_STEP + hh]
        slab = 64
        for r0 in range(0, RET_DIM, slab):
            state = jnp.zeros((slab, RET_DIM), F32)
            for n in range(1, n_chunks):
                state = state * cd + u_sc[hh, n - 1, r0:r0 + slab, :]
                st_sc[hh, n, r0:r0 + slab, :] = state.astype(BF16)

    def finish(hh):
        gnw = gnw_ref[:, hh * RET_DIM:(hh + 1) * RET_DIM]
        for n in range(n_chunks):
            idx = pl.ds(n * C, C)
            out = intra_sc[hh, idx, :]
            if n > 0:
                out = out + jnp.dot(head(q_ref, hh, idx), st_sc[hh, n],
                                    preferred_element_type=F32) * qd_ref[hh]
            mu = jnp.mean(out, axis=-1, keepdims=True)
            cen = out - mu
            var = jnp.mean(cen * cen, axis=-1, keepdims=True)
            o = cen * lax.rsqrt(var + EPS) * gnw
            z = head(z_ref, hh, idx).astype(F32)
            y = (o * (z * jax.nn.sigmoid(z))).astype(BF16)
            y_ref[0, hh * per_head, idx, :] = y[:, :half]
            y_ref[0, hh * per_head + 1, idx, :] = y[:, half:]

    for hh in range(RET_HEADS_PER_STEP):
        state_free(hh)
        if hh > 0:
            finish(hh - 1)
        recurrence(hh)
    finish(RET_HEADS_PER_STEP - 1)


def _ret(proj, gn_w_l):
    B, _, S, _ = proj.shape
    inner, q_decay, k_decay, chunk_decay = _ret_decay_tables()
    blocks = RET_HEADS_PER_STEP * RET_DIM // LANES
    hps = RET_HEADS_PER_STEP

    def spec(base):
        return pl.BlockSpec((1, blocks, S, LANES), lambda b, h: (b, base // blocks + h, 0, 0))

    return pl.pallas_call(
        _ret_kernel,
        grid=(B, RET_HEADS // hps),
        in_specs=[
            pl.BlockSpec(memory_space=pltpu.SMEM),
            spec(BLK_QR), spec(BLK_KR), spec(BLK_VR), spec(BLK_ZR),
            pl.BlockSpec((hps, RET_CHUNK, RET_CHUNK), lambda b, h: (h, 0, 0)),
            pl.BlockSpec((hps, RET_CHUNK, 1), lambda b, h: (h, 0, 0)),
            pl.BlockSpec((hps, RET_CHUNK, 1), lambda b, h: (h, 0, 0)),
            pl.BlockSpec((1, hps * RET_DIM), lambda b, h: (0, h)),
        ],
        out_specs=pl.BlockSpec((1, blocks, S, LANES), lambda b, h: (b, h, 0, 0)),
        out_shape=jax.ShapeDtypeStruct((B, RET_HEADS * RET_DIM // LANES, S, LANES), BF16),
        scratch_shapes=[pltpu.VMEM((hps, S, RET_DIM), F32),
                        pltpu.VMEM((hps, S // RET_CHUNK, RET_DIM, RET_DIM), F32),
                        pltpu.VMEM((hps, S // RET_CHUNK, RET_DIM, RET_DIM), BF16)],
        compiler_params=_params("arbitrary", "arbitrary"),
        name="retention",
    )(chunk_decay, proj, proj, proj, proj, inner, q_decay, k_decay,
      gn_w_l.reshape(1, RET_HEADS * RET_DIM))


def _outproj_kernel(ya_ref, yr_ref, ga_ref, gr_ref, x_ref, gate_ref, wpa_ref, wpr_ref, wo_ref,
                    fnw_ref, o_ref, *, final_norm):
    def cat(ref):
        return jnp.concatenate([ref[0, i] for i in range(ref.shape[1])], axis=-1)

    a = jnp.dot(cat(ya_ref), wpa_ref[0], preferred_element_type=F32)
    r = jnp.dot(cat(yr_ref), wpr_ref[0], preferred_element_type=F32)
    merged = (jax.nn.sigmoid(cat(ga_ref).astype(F32)) * a
              + jax.nn.sigmoid(cat(gr_ref).astype(F32)) * r)
    out = x_ref[0] + gate_ref[0] * jnp.dot(merged.astype(BF16), wo_ref[0],
                                           preferred_element_type=F32)
    if final_norm:
        ms = jnp.mean(out * out, axis=-1, keepdims=True)
        out = out * lax.rsqrt(ms + EPS) * fnw_ref[...]
    o_ref[0] = out


def _outproj(x, ya, yr, proj, mod_l, wpa, wpr, wo, final_norm_w, layer, final_norm):
    B, S, D = x.shape
    tm = 1024
    n_g = D // LANES
    weight = lambda w: pl.BlockSpec((1,) + w.shape[1:], lambda b, i: (layer, 0, 0))
    return pl.pallas_call(
        functools.partial(_outproj_kernel, final_norm=final_norm),
        grid=(B, S // tm),
        in_specs=[
            pl.BlockSpec((1, ya.shape[1], tm, LANES), lambda b, i: (b, 0, i, 0)),
            pl.BlockSpec((1, yr.shape[1], tm, LANES), lambda b, i: (b, 0, i, 0)),
            pl.BlockSpec((1, n_g, tm, LANES), lambda b, i: (b, BLK_GA // n_g, i, 0)),
            pl.BlockSpec((1, n_g, tm, LANES), lambda b, i: (b, BLK_GR // n_g, i, 0)),
            pl.BlockSpec((1, tm, D), lambda b, i: (b, i, 0)),
            pl.BlockSpec((1, 1, D), lambda b, i: (b, 0, 2)),
            weight(wpa), weight(wpr), weight(wo),
            pl.BlockSpec((1, D), lambda b, i: (0, 0)),
        ],
        out_specs=pl.BlockSpec((1, tm, D), lambda b, i: (b, i, 0)),
        out_shape=jax.ShapeDtypeStruct((B, S, D), F32),
        compiler_params=_params("arbitrary", "arbitrary"),
        name="outproj",
    )(ya, yr, proj, proj, x, mod_l, wpa, wpr, wo, final_norm_w.reshape(1, D))


def kernel(x, c, positions, norm_w, w_ada, b_ada, w_in, ret_gn_w, w_proj_attn, w_proj_ret, w_out,
           final_norm_w):
    B = x.shape[0]
    mod = _ada(c, w_ada, b_ada).reshape(DEPTH, B, 1, 3 * D_MODEL)
    cos, sin = _rope_tables(positions)
    w_in_b = jnp.concatenate([w_in[:, :, s:s + n] for s, n in _column_segments()],
                             axis=-1).astype(BF16)
    wpa_b = w_proj_attn.astype(BF16)
    wpr_b = w_proj_ret.astype(BF16)
    wo_b = w_out.astype(BF16)
    norm_w3 = norm_w.reshape(DEPTH, 1, D_MODEL)
    for l in range(DEPTH):
        proj = _inproj(x, mod[l], norm_w3, w_in_b, cos, sin, layer=l)
        ya = _attn(proj)
        yr = _ret(proj, ret_gn_w[l])
        x = _outproj(x, ya, yr, proj, mod[l], wpa_b, wpr_b, wo_b, final_norm_w, layer=l,
                     final_norm=(l == DEPTH - 1))
    return x
```

```python
import functools

import numpy as np
import jax
import jax.numpy as jnp
from jax import lax
from jax.experimental import pallas as pl
from jax.experimental.pallas import tpu as pltpu

D_MODEL = 1024
SEQ = 2048
DEPTH = 4
ATT_DILATIONS = (1, 4, 16)
N_ATT_GROUPS = 3
ATT_SLOTS = 4
ATT_HEAD_DIM = 128
ATT_BLOCK = 128
ATT_SLOTS_PER_STEP = 2
RET_HEADS = 4
RET_DIM = 256
RET_CHUNK = 512
RET_HEADS_PER_STEP = 2
ROPE_BASE = 10000.0
EPS = 1e-6
NEG_INF = -1e30

LANES = 128
COL_BLOCK = 512
BLOCKS_PER_STEP = COL_BLOCK // LANES
DOTS_PER_STEP = 2
ATT_QKV = N_ATT_GROUPS * ATT_SLOTS * ATT_HEAD_DIM
IN_WIDTH = 3 * ATT_QKV + ATT_SLOTS * ATT_HEAD_DIM + 4 * RET_HEADS * RET_DIM + 2 * D_MODEL
N_COL_STEPS = IN_WIDTH // COL_BLOCK
N_BLOCKS = IN_WIDTH // LANES


def _column_segments():
    grp = ATT_SLOTS * ATT_HEAD_DIM
    qkv = lambda t, g: (t * ATT_QKV + g * grp, grp)
    seg = [qkv(0, 0), qkv(1, 0), qkv(2, 0), (3 * ATT_QKV, IN_WIDTH - 3 * ATT_QKV)]
    for g in range(1, N_ATT_GROUPS):
        seg += [qkv(0, g), qkv(1, g), qkv(2, g)]
    return seg


_GRP_BLOCKS = ATT_SLOTS * ATT_HEAD_DIM // LANES
_WIDE_BLOCKS = RET_HEADS * RET_DIM // LANES
BLK_ZA = 3 * _GRP_BLOCKS
BLK_QR = BLK_ZA + _GRP_BLOCKS
BLK_KR = BLK_QR + _WIDE_BLOCKS
BLK_VR = BLK_KR + _WIDE_BLOCKS
BLK_ZR = BLK_VR + _WIDE_BLOCKS
BLK_GA = BLK_ZR + _WIDE_BLOCKS
BLK_GR = BLK_GA + D_MODEL // LANES
_BLK_G1 = BLK_GR + D_MODEL // LANES
BLK_Q = (0,) + tuple(_BLK_G1 + (g - 1) * 3 * _GRP_BLOCKS for g in range(1, N_ATT_GROUPS))
BLK_K = tuple(b + _GRP_BLOCKS for b in BLK_Q)
BLK_V = tuple(b + 2 * _GRP_BLOCKS for b in BLK_Q)
N_NATURAL_STEPS = _BLK_G1 // BLOCKS_PER_STEP
STEPS_PER_GROUP = 3 * _GRP_BLOCKS // BLOCKS_PER_STEP

VMEM_LIMIT = 60 * 1024 * 1024
F32 = jnp.float32
BF16 = jnp.bfloat16


def _params(*sem):
    return pltpu.CompilerParams(dimension_semantics=sem, vmem_limit_bytes=VMEM_LIMIT)


def _ada_kernel(c_ref, w_ref, b_ref, o_ref):
    c = c_ref[...]
    c_act = (c * jax.nn.sigmoid(c)).astype(BF16)
    acc = jnp.dot(c_act, w_ref[0].astype(BF16), preferred_element_type=F32)
    o_ref[0] = acc + b_ref[0]


def _ada(c, w_ada, b_ada):
    B = c.shape[0]
    n_col = 3 * D_MODEL // D_MODEL
    return pl.pallas_call(
        _ada_kernel,
        grid=(DEPTH, n_col),
        in_specs=[
            pl.BlockSpec((B, D_MODEL), lambda l, j: (0, 0)),
            pl.BlockSpec((1, D_MODEL, D_MODEL), lambda l, j: (l, 0, j)),
            pl.BlockSpec((1, 1, D_MODEL), lambda l, j: (l, 0, j)),
        ],
        out_specs=pl.BlockSpec((1, B, D_MODEL), lambda l, j: (l, 0, j)),
        out_shape=jax.ShapeDtypeStruct((DEPTH, B, 3 * D_MODEL), F32),
        compiler_params=_params("arbitrary", "arbitrary"),
        name="ada",
    )(c, w_ada, b_ada.reshape(DEPTH, 1, 3 * D_MODEL))


def _rope_kernel(pos_ref, theta_ref, cos_ref, sin_ref):
    ang = pos_ref[0].astype(F32) * theta_ref[...]
    cos_ref[0] = jnp.cos(ang)
    sin_ref[0] = jnp.sin(ang)


def _rope_tables(positions):
    B, S = positions.shape
    half = RET_DIM // 2
    theta = ROPE_BASE ** (-jnp.arange(half, dtype=F32) / half)
    spec = pl.BlockSpec((1, S, half), lambda b: (b, 0, 0))
    return pl.pallas_call(
        _rope_kernel,
        grid=(B,),
        in_specs=[pl.BlockSpec((1, S, 1), lambda b: (b, 0, 0)),
                  pl.BlockSpec((1, half), lambda b: (0, 0))],
        out_specs=[spec, spec],
        out_shape=[jax.ShapeDtypeStruct((B, S, half), F32)] * 2,
        compiler_params=_params("arbitrary"),
        name="rope",
    )(positions.reshape(B, S, 1), theta.reshape(1, half))


def _row_order_of_column_block(cb):
    return 0 if cb < N_NATURAL_STEPS else 1 + (cb - N_NATURAL_STEPS) // STEPS_PER_GROUP


def _rotary_scale_of_column_block(cb):
    blk = cb * BLOCKS_PER_STEP
    if BLK_QR <= blk < BLK_KR:
        return 1.0
    if BLK_KR <= blk < BLK_VR:
        return RET_DIM ** -0.5
    return None


def _inproj_kernel(x_hbm, mod_ref, nw_ref, w_ref, cos_ref, sin_ref, o_ref,
                   h0_ref, h1_ref, h2_ref, hn_ref, hp_ref, x_ref, x_sem):
    b, j = pl.program_id(0), pl.program_id(1)
    S = x_ref.shape[0]
    rows = 128

    def x_copy(seq):
        return pltpu.make_async_copy(x_hbm.at[seq], x_ref, x_sem)

    @pl.when((j == 0) & (b == 0))
    def _():
        x_copy(b).start()

    @pl.when((j == 1) & (b + 1 < pl.num_programs(0)))
    def _():
        x_copy(b + 1).start()

    n_lane_blocks = D_MODEL // LANES
    h_refs = (h0_ref, h1_ref, h2_ref)
    half = RET_DIM // 2

    def project(h_ref, i, rotary_scale):
        res = jnp.dot(h_ref[...], w_ref[0, :, i * COL_BLOCK:(i + 1) * COL_BLOCK],
                      preferred_element_type=F32)
        blocks = [res[:, c * LANES:(c + 1) * LANES] for c in range(BLOCKS_PER_STEP)]
        if rotary_scale is not None:
            cos, sin = cos_ref[0], sin_ref[0]
            if rotary_scale != 1.0:
                cos, sin = cos * rotary_scale, sin * rotary_scale
            for c in range(0, BLOCKS_PER_STEP, RET_DIM // LANES):
                t1, t2 = blocks[c], blocks[c + half // LANES]
                blocks[c] = t1 * cos - t2 * sin
                blocks[c + half // LANES] = t2 * cos + t1 * sin
        for c, blk in enumerate(blocks):
            o_ref[0, i * BLOCKS_PER_STEP + c] = blk.astype(BF16)

    @pl.when(j == 0)
    def _():
        x_copy(b).wait()
        shift = mod_ref[0, :, 0:D_MODEL]
        wmul = nw_ref[0] * (1.0 + mod_ref[0, :, D_MODEL:2 * D_MODEL])

        def natural(c, carry):
            idx = pl.ds(pl.multiple_of(c * rows, rows), rows)
            xs = x_ref[idx, :]
            ms = jnp.mean(xs * xs, axis=-1, keepdims=True)
            hv = xs * lax.rsqrt(ms + EPS) * wmul + shift
            h0_ref[idx, :] = hv.astype(BF16)
            for cb in range(n_lane_blocks):
                hn_ref[cb, idx, :] = hv[:, cb * LANES:(cb + 1) * LANES]
            return carry

        lax.fori_loop(0, S // rows, natural, 0, unroll=4)

        src_ref, src_d = hn_ref, 1
        for k, (h_ref, d) in enumerate(zip(h_refs[1:], ATT_DILATIONS[1:])):
            step = d // src_d
            n_src = S // step
            last = k == len(ATT_DILATIONS) - 2

            def permuted(c, carry, h_ref=h_ref, src_ref=src_ref, step=step, n_src=n_src,
                         last=last):
                p0 = pl.multiple_of(c * rows, rows)
                start = (p0 % n_src) * step + p0 // n_src
                pieces = [src_ref[cb, pl.ds(start, rows, stride=step), :]
                          for cb in range(n_lane_blocks)]
                h_ref[pl.ds(p0, rows), :] = jnp.concatenate(pieces, axis=-1).astype(BF16)
                if not last:
                    for cb in range(n_lane_blocks):
                        hp_ref[cb, pl.ds(p0, rows), :] = pieces[cb]
                return carry

            lax.fori_loop(0, S // rows, permuted, 0, unroll=4)
            src_ref, src_d = hp_ref, d

    steps_by_kind = {}
    for s in range(N_COL_STEPS // DOTS_PER_STEP):
        cbs = [s * DOTS_PER_STEP + i for i in range(DOTS_PER_STEP)]
        kind = tuple((_row_order_of_column_block(cb), _rotary_scale_of_column_block(cb))
                     for cb in cbs)
        steps_by_kind.setdefault(kind, []).append(s)
    for kind, steps in steps_by_kind.items():
        @pl.when(functools.reduce(jnp.logical_or, [j == s for s in steps]))
        def _(kind=kind):
            for i, (order, rotary_scale) in enumerate(kind):
                project(h_refs[order], i, rotary_scale)


def _inproj(x, mod_l, norm_w, w_in, cos, sin, layer):
    B, S, D = x.shape
    tab = pl.BlockSpec((1, S, LANES), lambda b, j: (b, 0, 0))
    return pl.pallas_call(
        _inproj_kernel,
        grid=(B, N_COL_STEPS // DOTS_PER_STEP),
        in_specs=[
            pl.BlockSpec(memory_space=pl.ANY),
            pl.BlockSpec((1, 1, 3 * D), lambda b, j: (b, 0, 0)),
            pl.BlockSpec((1, 1, D), lambda b, j: (layer, 0, 0)),
            pl.BlockSpec((1, D, DOTS_PER_STEP * COL_BLOCK), lambda b, j: (layer, 0, j)),
            tab, tab,
        ],
        out_specs=pl.BlockSpec((1, DOTS_PER_STEP * BLOCKS_PER_STEP, S, LANES),
                               lambda b, j: (b, j, 0, 0)),
        out_shape=jax.ShapeDtypeStruct((B, N_BLOCKS, S, LANES), BF16),
        scratch_shapes=[pltpu.VMEM((S, D), BF16)] * N_ATT_GROUPS
                       + [pltpu.VMEM((D // LANES, S, LANES), F32)] * 2
                       + [pltpu.VMEM((S, D), F32), pltpu.SemaphoreType.DMA(())],
        compiler_params=_params("arbitrary", "arbitrary"),
        name="inproj",
    )(x, mod_l, norm_w, w_in, cos, sin)


def _attn_kernel(q0, k0, v0, q1, k1, v1, q2, k2, v2, z_ref, y_ref,
                 sb_sc, sd_sc, m_sc, acc_sc, l_sc):
    S = y_ref.shape[2]
    T = ATT_BLOCK
    c_exp2 = ATT_HEAD_DIM ** -0.5 * np.log2(np.e)
    row = lax.broadcasted_iota(jnp.int32, (T, 2 * T), 0)
    col = lax.broadcasted_iota(jnp.int32, (T, 2 * T), 1)
    mask_band = (col >= row) & (col <= row + T)
    mask_diag = (lax.broadcasted_iota(jnp.int32, (T, T), 1)
                 <= lax.broadcasted_iota(jnp.int32, (T, T), 0))
    groups = ((q0, k0, v0), (q1, k1, v1), (q2, k2, v2))
    d1, d2 = ATT_DILATIONS[1], ATT_DILATIONS[2]
    len1 = S // d1
    nb1 = len1 // T
    n_band0 = S // T - 1

    def scores(sl, g, q_start, band, blk, out_idx, m_idx):
        q_ref, k_ref, _ = groups[g]
        q = q_ref[0, sl, pl.ds(q_start, T), :]
        if band:
            k = k_ref[0, sl, pl.ds(q_start - T, 2 * T), :]
        else:
            k = k_ref[0, sl, pl.ds(q_start, T), :]
        s = lax.dot_general(q, k, (((1,), (1,)), ((), ())), preferred_element_type=F32) * c_exp2
        s = jnp.where(mask_band if band else mask_diag, s, NEG_INF)
        (sb_sc if band else sd_sc)[blk] = s
        m_sc[g, out_idx, :] = jnp.broadcast_to(jnp.max(s, axis=-1, keepdims=True), (T, LANES))

    def weighted_values(sl, g, q_start, band, blk, out_idx, m_idx):
        v_ref = groups[g][2]
        m = m_sc[min(g, 1), m_idx, :]
        if band:
            p = jnp.exp2(sb_sc[blk] - jnp.concatenate([m, m], axis=-1))
            v = v_ref[0, sl, pl.ds(q_start - T, 2 * T), :]
        else:
            p = jnp.exp2(sd_sc[blk] - m)
            v = v_ref[0, sl, pl.ds(q_start, T), :]
        v1 = jnp.concatenate([v, jnp.ones_like(v)], axis=-1)
        r = jnp.dot(p.astype(BF16), v1, preferred_element_type=F32)
        acc_sc[g, out_idx, :] = r[:, :LANES]
        l_sc[g, out_idx, :] = r[:, LANES:]

    blocks = []
    for i in range(S // T):
        blocks.append((0, i * T, i > 0, max(i - 1, 0), pl.ds(i * T, T), pl.ds(i * T, T)))
    for r in range(d1):
        for n in range(nb1):
            idx = pl.ds(r * len1 + n * T, T)
            blk = 1 + r if n == 0 else n_band0 + r * (nb1 - 1) + n - 1
            blocks.append((1, r * len1 + n * T, n > 0, blk, idx, idx))
    ratio = d2 // d1
    for r in range(d2):
        idx = pl.ds((r % d1) * len1 + r // d1, T, stride=ratio)
        blocks.append((2, r * T, False, 1 + d1 + r, idx, idx))

    def all_blocks(fn):
        for blk in blocks:
            fn(*blk)

    rows = 256

    def natural_rows(c):
        p0 = pl.multiple_of(c * rows, rows)
        return pl.ds(p0, rows), pl.ds((p0 % len1) * d1 + p0 // len1, rows, stride=d1)

    def shared_max(c, carry):
        idx, nat = natural_rows(c)
        m = jnp.maximum(jnp.maximum(m_sc[0, nat, :], m_sc[1, idx, :]), m_sc[2, idx, :])
        m_sc[0, nat, :] = m
        m_sc[1, idx, :] = m
        return carry

    def combine(c, carry):
        idx, nat = natural_rows(c)
        o = ((acc_sc[0, nat, :] + acc_sc[1, idx, :] + acc_sc[2, idx, :])
             / (l_sc[0, nat, :] + l_sc[1, idx, :] + l_sc[2, idx, :]))
        acc_sc[0, nat, :] = o
        return carry

    def finish(sl, c, carry):
        idx = pl.ds(pl.multiple_of(c * rows, rows), rows)
        hz = 0.5 * z_ref[0, sl, idx, :].astype(F32)
        silu = hz + hz * jnp.tanh(hz)
        y_ref[0, sl, idx, :] = (acc_sc[0, idx, :] * silu).astype(BF16)
        return carry

    for sl in range(ATT_SLOTS_PER_STEP):
        all_blocks(functools.partial(scores, sl))
        lax.fori_loop(0, S // rows, shared_max, 0, unroll=True)
        all_blocks(functools.partial(weighted_values, sl))
        lax.fori_loop(0, S // rows, combine, 0, unroll=True)
        lax.fori_loop(0, S // rows, functools.partial(finish, sl), 0)


def _attn(proj):
    B, _, S, _ = proj.shape
    sps = ATT_SLOTS_PER_STEP

    def spec(base):
        return pl.BlockSpec((1, sps, S, LANES), lambda b, s: (b, base // sps + s, 0, 0))

    in_specs = []
    for g in range(N_ATT_GROUPS):
        in_specs += [spec(BLK_Q[g]), spec(BLK_K[g]), spec(BLK_V[g])]
    in_specs.append(spec(BLK_ZA))
    n_diag = sum(ATT_DILATIONS)
    n_band = N_ATT_GROUPS * S // ATT_BLOCK - n_diag
    return pl.pallas_call(
        _attn_kernel,
        grid=(B, ATT_SLOTS // sps),
        in_specs=in_specs,
        out_specs=pl.BlockSpec((1, sps, S, LANES), lambda b, s: (b, s, 0, 0)),
        out_shape=jax.ShapeDtypeStruct((B, ATT_SLOTS, S, LANES), BF16),
        scratch_shapes=[pltpu.VMEM((n_band, ATT_BLOCK, 2 * ATT_BLOCK), F32),
                        pltpu.VMEM((n_diag, ATT_BLOCK, ATT_BLOCK), F32),
                        pltpu.VMEM((N_ATT_GROUPS, S, LANES), F32),
                        pltpu.VMEM((N_ATT_GROUPS, S, LANES), F32),
                        pltpu.VMEM((N_ATT_GROUPS, S, LANES), F32)],
        compiler_params=_params("arbitrary", "arbitrary"),
        name="attn",
    )(*([proj] * 10))


def _ret_decay_tables():
    H, C = RET_HEADS, RET_CHUNK
    log_g = np.log1p(-np.exp2(-5.0 - np.arange(H, dtype=np.float64)))
    idx = np.arange(C, dtype=np.float64)
    diff = idx[:, None] - idx[None, :]
    inner = np.where(diff >= 0, np.exp(log_g[:, None, None] * np.maximum(diff, 0.0)), 0.0)
    q_decay = np.exp(log_g[:, None] * (idx + 1.0))
    k_decay = np.exp(log_g[:, None] * (C - 1.0 - idx))
    chunk_decay = np.exp(log_g * C)
    f = lambda a: jnp.asarray(a, dtype=F32)
    return f(inner), f(q_decay[:, :, None]), f(k_decay[:, :, None]), f(chunk_decay)


def _ret_kernel(cd_ref, q_ref, k_ref, v_ref, z_ref, inner_ref, qd_ref, kd_ref,
                gnw_ref, y_ref, intra_sc, u_sc, st_sc):
    S = y_ref.shape[2]
    C = RET_CHUNK
    n_chunks = S // C
    half = RET_DIM // 2
    per_head = RET_DIM // LANES

    def head(ref, hh, idx):
        return jnp.concatenate([ref[0, hh * per_head + i, idx, :] for i in range(per_head)],
                               axis=-1)

    def state_free(hh):
        for n in range(n_chunks):
            idx = pl.ds(n * C, C)
            qc, kc, vc = head(q_ref, hh, idx), head(k_ref, hh, idx), head(v_ref, hh, idx)
            att = lax.dot_general(qc, kc, (((1,), (1,)), ((), ())), preferred_element_type=F32)
            att = (att * inner_ref[hh]).astype(BF16)
            intra_sc[hh, idx, :] = jnp.dot(att, vc, preferred_element_type=F32)
            if n + 1 < n_chunks:
                kdt = (kc.astype(F32) * kd_ref[hh]).T.astype(BF16)
                u_sc[hh, n] = jnp.dot(kdt, vc, preferred_element_type=F32)

    def recurrence(hh):
        cd = cd_ref[pl.program_id(1) * RET_HEADS_PER_STEP + hh]
        slab = 64
        for r0 in range(0, RET_DIM, slab):
            state = jnp.zeros((slab, RET_DIM), F32)
            for n in range(1, n_chunks):
                state = state * cd + u_sc[hh, n - 1, r0:r0 + slab, :]
                st_sc[hh, n, r0:r0 + slab, :] = state.astype(BF16)

    def finish(hh):
        gnw = gnw_ref[:, hh * RET_DIM:(hh + 1) * RET_DIM]
        for n in range(n_chunks):
            idx = pl.ds(n * C, C)
            out = intra_sc[hh, idx, :]
            if n > 0:
                out = out + jnp.dot(head(q_ref, hh, idx), st_sc[hh, n],
                                    preferred_element_type=F32) * qd_ref[hh]
            mu = jnp.mean(out, axis=-1, keepdims=True)
            cen = out - mu
            var = jnp.mean(cen * cen, axis=-1, keepdims=True)
            o = cen * lax.rsqrt(var + EPS) * gnw
            z = head(z_ref, hh, idx).astype(F32)
            y = (o * (z * jax.nn.sigmoid(z))).astype(BF16)
            y_ref[0, hh * per_head, idx, :] = y[:, :half]
            y_ref[0, hh * per_head + 1, idx, :] = y[:, half:]

    for hh in range(RET_HEADS_PER_STEP):
        state_free(hh)
        if hh > 0:
            finish(hh - 1)
        recurrence(hh)
    finish(RET_HEADS_PER_STEP - 1)


def _ret(proj, gn_w_l):
    B, _, S, _ = proj.shape
    inner, q_decay, k_decay, chunk_decay = _ret_decay_tables()
    blocks = RET_HEADS_PER_STEP * RET_DIM // LANES
    hps = RET_HEADS_PER_STEP

    def spec(base):
        return pl.BlockSpec((1, blocks, S, LANES), lambda b, h: (b, base // blocks + h, 0, 0))

    return pl.pallas_call(
        _ret_kernel,
        grid=(B, RET_HEADS // hps),
        in_specs=[
            pl.BlockSpec(memory_space=pltpu.SMEM),
            spec(BLK_QR), spec(BLK_KR), spec(BLK_VR), spec(BLK_ZR),
            pl.BlockSpec((hps, RET_CHUNK, RET_CHUNK), lambda b, h: (h, 0, 0)),
            pl.BlockSpec((hps, RET_CHUNK, 1), lambda b, h: (h, 0, 0)),
            pl.BlockSpec((hps, RET_CHUNK, 1), lambda b, h: (h, 0, 0)),
            pl.BlockSpec((1, hps * RET_DIM), lambda b, h: (0, h)),
        ],
        out_specs=pl.BlockSpec((1, blocks, S, LANES), lambda b, h: (b, h, 0, 0)),
        out_shape=jax.ShapeDtypeStruct((B, RET_HEADS * RET_DIM // LANES, S, LANES), BF16),
        scratch_shapes=[pltpu.VMEM((hps, S, RET_DIM), F32),
                        pltpu.VMEM((hps, S // RET_CHUNK, RET_DIM, RET_DIM), F32),
                        pltpu.VMEM((hps, S // RET_CHUNK, RET_DIM, RET_DIM), BF16)],
        compiler_params=_params("arbitrary", "arbitrary"),
        name="retention",
    )(chunk_decay, proj, proj, proj, proj, inner, q_decay, k_decay,
      gn_w_l.reshape(1, RET_HEADS * RET_DIM))


def _outproj_kernel(ya_ref, yr_ref, ga_ref, gr_ref, x_ref, gate_ref, wpa_ref, wpr_ref, wo_ref,
                    fnw_ref, o_ref, *, final_norm):
    def cat(ref):
        return jnp.concatenate([ref[0, i] for i in range(ref.shape[1])], axis=-1)

    a = jnp.dot(cat(ya_ref), wpa_ref[0], preferred_element_type=F32)
    r = jnp.dot(cat(yr_ref), wpr_ref[0], preferred_element_type=F32)
    merged = (jax.nn.sigmoid(cat(ga_ref).astype(F32)) * a
              + jax.nn.sigmoid(cat(gr_ref).astype(F32)) * r)
    out = x_ref[0] + gate_ref[0] * jnp.dot(merged.astype(BF16), wo_ref[0],
                                           preferred_element_type=F32)
    if final_norm:
        ms = jnp.mean(out * out, axis=-1, keepdims=True)
        out = out * lax.rsqrt(ms + EPS) * fnw_ref[...]
    o_ref[0] = out


def _outproj(x, ya, yr, proj, mod_l, wpa, wpr, wo, final_norm_w, layer, final_norm):
    B, S, D = x.shape
    tm = 1024
    n_g = D // LANES
    weight = lambda w: pl.BlockSpec((1,) + w.shape[1:], lambda b, i: (layer, 0, 0))
    return pl.pallas_call(
        functools.partial(_outproj_kernel, final_norm=final_norm),
        grid=(B, S // tm),
        in_specs=[
            pl.BlockSpec((1, ya.shape[1], tm, LANES), lambda b, i: (b, 0, i, 0)),
            pl.BlockSpec((1, yr.shape[1], tm, LANES), lambda b, i: (b, 0, i, 0)),
            pl.BlockSpec((1, n_g, tm, LANES), lambda b, i: (b, BLK_GA // n_g, i, 0)),
            pl.BlockSpec((1, n_g, tm, LANES), lambda b, i: (b, BLK_GR // n_g, i, 0)),
            pl.BlockSpec((1, tm, D), lambda b, i: (b, i, 0)),
            pl.BlockSpec((1, 1, D), lambda b, i: (b, 0, 2)),
            weight(wpa), weight(wpr), weight(wo),
            pl.BlockSpec((1, D), lambda b, i: (0, 0)),
        ],
        out_specs=pl.BlockSpec((1, tm, D), lambda b, i: (b, i, 0)),
        out_shape=jax.ShapeDtypeStruct((B, S, D), F32),
        compiler_params=_params("arbitrary", "arbitrary"),
        name="outproj",
    )(ya, yr, proj, proj, x, mod_l, wpa, wpr, wo, final_norm_w.reshape(1, D))


def kernel(x, c, positions, norm_w, w_ada, b_ada, w_in, ret_gn_w, w_proj_attn, w_proj_ret, w_out,
           final_norm_w):
    B = x.shape[0]
    mod = _ada(c, w_ada, b_ada).reshape(DEPTH, B, 1, 3 * D_MODEL)
    cos, sin = _rope_tables(positions)
    w_in_b = jnp.concatenate([w_in[:, :, s:s + n] for s, n in _column_segments()],
                             axis=-1).astype(BF16)
    wpa_b = w_proj_attn.astype(BF16)
    wpr_b = w_proj_ret.astype(BF16)
    wo_b = w_out.astype(BF16)
    norm_w3 = norm_w.reshape(DEPTH, 1, D_MODEL)
    for l in range(DEPTH):
        proj = _inproj(x, mod[l], norm_w3, w_in_b, cos, sin, layer=l)
        ya = _attn(proj)
        yr = _ret(proj, ret_gn_w[l])
        x = _outproj(x, ya, yr, proj, mod[l], wpa_b, wpr_b, wo_b, final_norm_w, layer=l,
                     final_norm=(l == DEPTH - 1))
    return x
```

```python
import functools

import numpy as np
import jax
import jax.numpy as jnp
from jax import lax
from jax.experimental import pallas as pl
from jax.experimental.pallas import tpu as pltpu

D_MODEL = 1024
SEQ = 2048
DEPTH = 4
ATT_DILATIONS = (1, 4, 16)
N_ATT_GROUPS = 3
ATT_SLOTS = 4
ATT_HEAD_DIM = 128
ATT_BLOCK = 128
ATT_SLOTS_PER_STEP = 2
RET_HEADS = 4
RET_DIM = 256
RET_CHUNK = 512
RET_HEADS_PER_STEP = 2
ROPE_BASE = 10000.0
EPS = 1e-6
NEG_INF = -1e30

LANES = 128
COL_BLOCK = 512
BLOCKS_PER_STEP = COL_BLOCK // LANES
DOTS_PER_STEP = 2
ROW_CHUNK = 512
ATT_QKV = N_ATT_GROUPS * ATT_SLOTS * ATT_HEAD_DIM
IN_WIDTH = 3 * ATT_QKV + ATT_SLOTS * ATT_HEAD_DIM + 4 * RET_HEADS * RET_DIM + 2 * D_MODEL
N_COL_STEPS = IN_WIDTH // COL_BLOCK
N_BLOCKS = IN_WIDTH // LANES


def _column_segments():
    grp = ATT_SLOTS * ATT_HEAD_DIM
    qkv = lambda t, g: (t * ATT_QKV + g * grp, grp)
    seg = [qkv(0, 0), qkv(1, 0), qkv(2, 0), (3 * ATT_QKV, IN_WIDTH - 3 * ATT_QKV)]
    for g in range(1, N_ATT_GROUPS):
        seg += [qkv(0, g), qkv(1, g), qkv(2, g)]
    return seg


_GRP_BLOCKS = ATT_SLOTS * ATT_HEAD_DIM // LANES
_WIDE_BLOCKS = RET_HEADS * RET_DIM // LANES
BLK_ZA = 3 * _GRP_BLOCKS
BLK_QR = BLK_ZA + _GRP_BLOCKS
BLK_KR = BLK_QR + _WIDE_BLOCKS
BLK_VR = BLK_KR + _WIDE_BLOCKS
BLK_ZR = BLK_VR + _WIDE_BLOCKS
BLK_GA = BLK_ZR + _WIDE_BLOCKS
BLK_GR = BLK_GA + D_MODEL // LANES
_BLK_G1 = BLK_GR + D_MODEL // LANES
BLK_Q = (0,) + tuple(_BLK_G1 + (g - 1) * 3 * _GRP_BLOCKS for g in range(1, N_ATT_GROUPS))
BLK_K = tuple(b + _GRP_BLOCKS for b in BLK_Q)
BLK_V = tuple(b + 2 * _GRP_BLOCKS for b in BLK_Q)
N_NATURAL_STEPS = _BLK_G1 // BLOCKS_PER_STEP
STEPS_PER_GROUP = 3 * _GRP_BLOCKS // BLOCKS_PER_STEP

VMEM_LIMIT = 60 * 1024 * 1024
F32 = jnp.float32
BF16 = jnp.bfloat16


def _params(*sem):
    return pltpu.CompilerParams(dimension_semantics=sem, vmem_limit_bytes=VMEM_LIMIT)


def _ada_kernel(c_ref, w_ref, b_ref, o_ref):
    c = c_ref[...]
    c_act = (c * jax.nn.sigmoid(c)).astype(BF16)
    acc = jnp.dot(c_act, w_ref[0].astype(BF16), preferred_element_type=F32)
    o_ref[0] = acc + b_ref[0]


def _ada(c, w_ada, b_ada):
    B = c.shape[0]
    n_col = 3 * D_MODEL // D_MODEL
    return pl.pallas_call(
        _ada_kernel,
        grid=(DEPTH, n_col),
        in_specs=[
            pl.BlockSpec((B, D_MODEL), lambda l, j: (0, 0)),
            pl.BlockSpec((1, D_MODEL, D_MODEL), lambda l, j: (l, 0, j)),
            pl.BlockSpec((1, 1, D_MODEL), lambda l, j: (l, 0, j)),
        ],
        out_specs=pl.BlockSpec((1, B, D_MODEL), lambda l, j: (l, 0, j)),
        out_shape=jax.ShapeDtypeStruct((DEPTH, B, 3 * D_MODEL), F32),
        compiler_params=_params("arbitrary", "arbitrary"),
        name="ada",
    )(c, w_ada, b_ada.reshape(DEPTH, 1, 3 * D_MODEL))


def _rope_kernel(pos_ref, theta_ref, cos_ref, sin_ref):
    ang = pos_ref[0].astype(F32) * theta_ref[...]
    cos_ref[0] = jnp.cos(ang)
    sin_ref[0] = jnp.sin(ang)


def _rope_tables(positions):
    B, S = positions.shape
    half = RET_DIM // 2
    theta = ROPE_BASE ** (-jnp.arange(half, dtype=F32) / half)
    spec = pl.BlockSpec((1, S, half), lambda b: (b, 0, 0))
    return pl.pallas_call(
        _rope_kernel,
        grid=(B,),
        in_specs=[pl.BlockSpec((1, S, 1), lambda b: (b, 0, 0)),
                  pl.BlockSpec((1, half), lambda b: (0, 0))],
        out_specs=[spec, spec],
        out_shape=[jax.ShapeDtypeStruct((B, S, half), F32)] * 2,
        compiler_params=_params("arbitrary"),
        name="rope",
    )(positions.reshape(B, S, 1), theta.reshape(1, half))


def _row_order_of_column_block(cb):
    return 0 if cb < N_NATURAL_STEPS else 1 + (cb - N_NATURAL_STEPS) // STEPS_PER_GROUP


def _rotary_scale_of_column_block(cb):
    blk = cb * BLOCKS_PER_STEP
    if BLK_QR <= blk < BLK_KR:
        return 1.0
    if BLK_KR <= blk < BLK_VR:
        return RET_DIM ** -0.5
    return None


def _inproj_kernel(x_hbm, mod_ref, nw_ref, w_ref, cos_ref, sin_ref, o_ref,
                   h0_ref, h1_ref, h2_ref, hn_ref, hp_ref, x_ref, x_sem):
    b, j = pl.program_id(0), pl.program_id(1)
    S = x_ref.shape[0]
    rows = 128

    def x_copy(seq):
        return pltpu.make_async_copy(x_hbm.at[seq], x_ref, x_sem)

    @pl.when((j == 0) & (b == 0))
    def _():
        x_copy(b).start()

    @pl.when((j == 1) & (b + 1 < pl.num_programs(0)))
    def _():
        x_copy(b + 1).start()

    n_lane_blocks = D_MODEL // LANES
    h_refs = (h0_ref, h1_ref, h2_ref)
    half = RET_DIM // 2

    def project(h_ref, i, rotary_scale, mc):
        ridx = pl.ds(mc * ROW_CHUNK, ROW_CHUNK)
        res = jnp.dot(h_ref[ridx, :], w_ref[0, :, i * COL_BLOCK:(i + 1) * COL_BLOCK],
                      preferred_element_type=F32)
        blocks = [res[:, c * LANES:(c + 1) * LANES] for c in range(BLOCKS_PER_STEP)]
        if rotary_scale is not None:
            cos, sin = cos_ref[0, ridx, :], sin_ref[0, ridx, :]
            if rotary_scale != 1.0:
                cos, sin = cos * rotary_scale, sin * rotary_scale
            for c in range(0, BLOCKS_PER_STEP, RET_DIM // LANES):
                t1, t2 = blocks[c], blocks[c + half // LANES]
                blocks[c] = t1 * cos - t2 * sin
                blocks[c + half // LANES] = t2 * cos + t1 * sin
        for c, blk in enumerate(blocks):
            o_ref[0, i * BLOCKS_PER_STEP + c, ridx, :] = blk.astype(BF16)

    @pl.when(j == 0)
    def _():
        x_copy(b).wait()
        shift = mod_ref[0, :, 0:D_MODEL]
        wmul = nw_ref[0] * (1.0 + mod_ref[0, :, D_MODEL:2 * D_MODEL])

        def natural(c, carry):
            idx = pl.ds(pl.multiple_of(c * rows, rows), rows)
            xs = x_ref[idx, :]
            ms = jnp.mean(xs * xs, axis=-1, keepdims=True)
            hv = xs * lax.rsqrt(ms + EPS) * wmul + shift
            h0_ref[idx, :] = hv.astype(BF16)
            for cb in range(n_lane_blocks):
                hn_ref[cb, idx, :] = hv[:, cb * LANES:(cb + 1) * LANES]
            return carry

        lax.fori_loop(0, S // rows, natural, 0, unroll=4)

        src_ref, src_d = hn_ref, 1
        for k, (h_ref, d) in enumerate(zip(h_refs[1:], ATT_DILATIONS[1:])):
            step = d // src_d
            n_src = S // step
            last = k == len(ATT_DILATIONS) - 2

            def permuted(c, carry, h_ref=h_ref, src_ref=src_ref, step=step, n_src=n_src,
                         last=last):
                p0 = pl.multiple_of(c * rows, rows)
                start = (p0 % n_src) * step + p0 // n_src
                pieces = [src_ref[cb, pl.ds(start, rows, stride=step), :]
                          for cb in range(n_lane_blocks)]
                h_ref[pl.ds(p0, rows), :] = jnp.concatenate(pieces, axis=-1).astype(BF16)
                if not last:
                    for cb in range(n_lane_blocks):
                        hp_ref[cb, pl.ds(p0, rows), :] = pieces[cb]
                return carry

            lax.fori_loop(0, S // rows, permuted, 0, unroll=4)
            src_ref, src_d = hp_ref, d

    steps_by_kind = {}
    for s in range(N_COL_STEPS // DOTS_PER_STEP):
        cbs = [s * DOTS_PER_STEP + i for i in range(DOTS_PER_STEP)]
        kind = tuple((_row_order_of_column_block(cb), _rotary_scale_of_column_block(cb))
                     for cb in cbs)
        steps_by_kind.setdefault(kind, []).append(s)
    for kind, steps in steps_by_kind.items():
        @pl.when(functools.reduce(jnp.logical_or, [j == s for s in steps]))
        def _(kind=kind):
            for i, (order, rotary_scale) in enumerate(kind):
                for mc in range(S // ROW_CHUNK):
                    project(h_refs[order], i, rotary_scale, mc)


def _inproj(x, mod_l, norm_w, w_in, cos, sin, layer):
    B, S, D = x.shape
    tab = pl.BlockSpec((1, S, LANES), lambda b, j: (b, 0, 0))
    return pl.pallas_call(
        _inproj_kernel,
        grid=(B, N_COL_STEPS // DOTS_PER_STEP),
        in_specs=[
            pl.BlockSpec(memory_space=pl.ANY),
            pl.BlockSpec((1, 1, 3 * D), lambda b, j: (b, 0, 0)),
            pl.BlockSpec((1, 1, D), lambda b, j: (layer, 0, 0)),
            pl.BlockSpec((1, D, DOTS_PER_STEP * COL_BLOCK), lambda b, j: (layer, 0, j)),
            tab, tab,
        ],
        out_specs=pl.BlockSpec((1, DOTS_PER_STEP * BLOCKS_PER_STEP, S, LANES),
                               lambda b, j: (b, j, 0, 0)),
        out_shape=jax.ShapeDtypeStruct((B, N_BLOCKS, S, LANES), BF16),
        scratch_shapes=[pltpu.VMEM((S, D), BF16)] * N_ATT_GROUPS
                       + [pltpu.VMEM((D // LANES, S, LANES), F32)] * 2
                       + [pltpu.VMEM((S, D), F32), pltpu.SemaphoreType.DMA(())],
        compiler_params=_params("arbitrary", "arbitrary"),
        name="inproj",
    )(x, mod_l, norm_w, w_in, cos, sin)


def _attn_kernel(q0, k0, v0, q1, k1, v1, q2, k2, v2, z_ref, y_ref,
                 sb_sc, sd_sc, m_sc, acc_sc, l_sc):
    S = y_ref.shape[2]
    T = ATT_BLOCK
    c_exp2 = ATT_HEAD_DIM ** -0.5 * np.log2(np.e)
    row = lax.broadcasted_iota(jnp.int32, (T, 2 * T), 0)
    col = lax.broadcasted_iota(jnp.int32, (T, 2 * T), 1)
    mask_band = (col >= row) & (col <= row + T)
    mask_diag = (lax.broadcasted_iota(jnp.int32, (T, T), 1)
                 <= lax.broadcasted_iota(jnp.int32, (T, T), 0))
    groups = ((q0, k0, v0), (q1, k1, v1), (q2, k2, v2))
    d1, d2 = ATT_DILATIONS[1], ATT_DILATIONS[2]
    len1 = S // d1
    nb1 = len1 // T
    n_band0 = S // T - 1

    def scores(sl, g, q_start, band, blk, out_idx, m_idx):
        q_ref, k_ref, _ = groups[g]
        q = q_ref[0, sl, pl.ds(q_start, T), :]
        if band:
            k = k_ref[0, sl, pl.ds(q_start - T, 2 * T), :]
        else:
            k = k_ref[0, sl, pl.ds(q_start, T), :]
        s = lax.dot_general(q, k, (((1,), (1,)), ((), ())), preferred_element_type=F32) * c_exp2
        s = jnp.where(mask_band if band else mask_diag, s, NEG_INF)
        (sb_sc if band else sd_sc)[blk] = s
        m_sc[g, out_idx, :] = jnp.broadcast_to(jnp.max(s, axis=-1, keepdims=True), (T, LANES))

    def weighted_values(sl, g, q_start, band, blk, out_idx, m_idx):
        v_ref = groups[g][2]
        m = m_sc[min(g, 1), m_idx, :]
        if band:
            p = jnp.exp2(sb_sc[blk] - jnp.concatenate([m, m], axis=-1))
            v = v_ref[0, sl, pl.ds(q_start - T, 2 * T), :]
        else:
            p = jnp.exp2(sd_sc[blk] - m)
            v = v_ref[0, sl, pl.ds(q_start, T), :]
        v1 = jnp.concatenate([v, jnp.ones_like(v)], axis=-1)
        r = jnp.dot(p.astype(BF16), v1, preferred_element_type=F32)
        acc_sc[g, out_idx, :] = r[:, :LANES]
        l_sc[g, out_idx, :] = r[:, LANES:]

    blocks = []
    for i in range(S // T):
        blocks.append((0, i * T, i > 0, max(i - 1, 0), pl.ds(i * T, T), pl.ds(i * T, T)))
    for r in range(d1):
        for n in range(nb1):
            idx = pl.ds(r * len1 + n * T, T)
            blk = 1 + r if n == 0 else n_band0 + r * (nb1 - 1) + n - 1
            blocks.append((1, r * len1 + n * T, n > 0, blk, idx, idx))
    ratio = d2 // d1
    for r in range(d2):
        idx = pl.ds((r % d1) * len1 + r // d1, T, stride=ratio)
        blocks.append((2, r * T, False, 1 + d1 + r, idx, idx))

    def all_blocks(fn):
        for blk in blocks:
            fn(*blk)

    rows = 256

    def natural_rows(c):
        p0 = pl.multiple_of(c * rows, rows)
        return pl.ds(p0, rows), pl.ds((p0 % len1) * d1 + p0 // len1, rows, stride=d1)

    def shared_max(c, carry):
        idx, nat = natural_rows(c)
        m = jnp.maximum(jnp.maximum(m_sc[0, nat, :], m_sc[1, idx, :]), m_sc[2, idx, :])
        m_sc[0, nat, :] = m
        m_sc[1, idx, :] = m
        return carry

    def combine(c, carry):
        idx, nat = natural_rows(c)
        o = ((acc_sc[0, nat, :] + acc_sc[1, idx, :] + acc_sc[2, idx, :])
             / (l_sc[0, nat, :] + l_sc[1, idx, :] + l_sc[2, idx, :]))
        acc_sc[0, nat, :] = o
        return carry

    def finish(sl, c, carry):
        idx = pl.ds(pl.multiple_of(c * rows, rows), rows)
        hz = 0.5 * z_ref[0, sl, idx, :].astype(F32)
        silu = hz + hz * jnp.tanh(hz)
        y_ref[0, sl, idx, :] = (acc_sc[0, idx, :] * silu).astype(BF16)
        return carry

    for sl in range(ATT_SLOTS_PER_STEP):
        all_blocks(functools.partial(scores, sl))
        lax.fori_loop(0, S // rows, shared_max, 0, unroll=True)
        all_blocks(functools.partial(weighted_values, sl))
        lax.fori_loop(0, S // rows, combine, 0, unroll=True)
        lax.fori_loop(0, S // rows, functools.partial(finish, sl), 0)


def _attn(proj):
    B, _, S, _ = proj.shape
    sps = ATT_SLOTS_PER_STEP

    def spec(base):
        return pl.BlockSpec((1, sps, S, LANES), lambda b, s: (b, base // sps + s, 0, 0))

    in_specs = []
    for g in range(N_ATT_GROUPS):
        in_specs += [spec(BLK_Q[g]), spec(BLK_K[g]), spec(BLK_V[g])]
    in_specs.append(spec(BLK_ZA))
    n_diag = sum(ATT_DILATIONS)
    n_band = N_ATT_GROUPS * S // ATT_BLOCK - n_diag
    return pl.pallas_call(
        _attn_kernel,
        grid=(B, ATT_SLOTS // sps),
        in_specs=in_specs,
        out_specs=pl.BlockSpec((1, sps, S, LANES), lambda b, s: (b, s, 0, 0)),
        out_shape=jax.ShapeDtypeStruct((B, ATT_SLOTS, S, LANES), BF16),
        scratch_shapes=[pltpu.VMEM((n_band, ATT_BLOCK, 2 * ATT_BLOCK), F32),
                        pltpu.VMEM((n_diag, ATT_BLOCK, ATT_BLOCK), F32),
                        pltpu.VMEM((N_ATT_GROUPS, S, LANES), F32),
                        pltpu.VMEM((N_ATT_GROUPS, S, LANES), F32),
                        pltpu.VMEM((N_ATT_GROUPS, S, LANES), F32)],
        compiler_params=_params("arbitrary", "arbitrary"),
        name="attn",
    )(*([proj] * 10))


def _ret_decay_tables():
    H, C = RET_HEADS, RET_CHUNK
    log_g = np.log1p(-np.exp2(-5.0 - np.arange(H, dtype=np.float64)))
    idx = np.arange(C, dtype=np.float64)
    diff = idx[:, None] - idx[None, :]
    inner = np.where(diff >= 0, np.exp(log_g[:, None, None] * np.maximum(diff, 0.0)), 0.0)
    q_decay = np.exp(log_g[:, None] * (idx + 1.0))
    k_decay = np.exp(log_g[:, None] * (C - 1.0 - idx))
    chunk_decay = np.exp(log_g * C)
    f = lambda a: jnp.asarray(a, dtype=F32)
    return f(inner), f(q_decay[:, :, None]), f(k_decay[:, :, None]), f(chunk_decay)


def _ret_kernel(cd_ref, q_ref, k_ref, v_ref, z_ref, inner_ref, qd_ref, kd_ref,
                gnw_ref, y_ref, intra_sc, u_sc, st_sc):
    S = y_ref.shape[2]
    C = RET_CHUNK
    n_chunks = S // C
    half = RET_DIM // 2
    per_head = RET_DIM // LANES

    def head(ref, hh, idx):
        return jnp.concatenate([ref[0, hh * per_head + i, idx, :] for i in range(per_head)],
                               axis=-1)

    def state_free(hh):
        for n in range(n_chunks):
            idx = pl.ds(n * C, C)
            qc, kc, vc = head(q_ref, hh, idx), head(k_ref, hh, idx), head(v_ref, hh, idx)
            att = lax.dot_general(qc, kc, (((1,), (1,)), ((), ())), preferred_element_type=F32)
            att = (att * inner_ref[hh]).astype(BF16)
            intra_sc[hh, idx, :] = jnp.dot(att, vc, preferred_element_type=F32)
            if n + 1 < n_chunks:
                kdt = (kc.astype(F32) * kd_ref[hh]).T.astype(BF16)
                u_sc[hh, n] = jnp.dot(kdt, vc, preferred_element_type=F32)

    def recurrence(hh):
        cd = cd_ref[pl.program_id(1) * RET_HEADS_PER_STEP + hh]
        slab = 64
        for r0 in range(0, RET_DIM, slab):
            state = jnp.zeros((slab, RET_DIM), F32)
            for n in range(1, n_chunks):
                state = state * cd + u_sc[hh, n - 1, r0:r0 + slab, :]
                st_sc[hh, n, r0:r0 + slab, :] = state.astype(BF16)

    def finish(hh):
        gnw = gnw_ref[:, hh * RET_DIM:(hh + 1) * RET_DIM]
        for n in range(n_chunks):
            idx = pl.ds(n * C, C)
            out = intra_sc[hh, idx, :]
            if n > 0:
                out = out + jnp.dot(head(q_ref, hh, idx), st_sc[hh, n],
                                    preferred_element_type=F32) * qd_ref[hh]
            mu = jnp.mean(out, axis=-1, keepdims=True)
            cen = out - mu
            var = jnp.mean(cen * cen, axis=-1, keepdims=True)
            o = cen * lax.rsqrt(var + EPS) * gnw
            z = head(z_ref, hh, idx).astype(F32)
            y = (o * (z * jax.nn.sigmoid(z))).astype(BF16)
            y_ref[0, hh * per_head, idx, :] = y[:, :half]
            y_ref[0, hh * per_head + 1, idx, :] = y[:, half:]

    for hh in range(RET_HEADS_PER_STEP):
        state_free(hh)
        if hh > 0:
            finish(hh - 1)
        recurrence(hh)
    finish(RET_HEADS_PER_STEP - 1)


def _ret(proj, gn_w_l):
    B, _, S, _ = proj.shape
    inner, q_decay, k_decay, chunk_decay = _ret_decay_tables()
    blocks = RET_HEADS_PER_STEP * RET_DIM // LANES
    hps = RET_HEADS_PER_STEP

    def spec(base):
        return pl.BlockSpec((1, blocks, S, LANES), lambda b, h: (b, base // blocks + h, 0, 0))

    return pl.pallas_call(
        _ret_kernel,
        grid=(B, RET_HEADS // hps),
        in_specs=[
            pl.BlockSpec(memory_space=pltpu.SMEM),
            spec(BLK_QR), spec(BLK_KR), spec(BLK_VR), spec(BLK_ZR),
            pl.BlockSpec((hps, RET_CHUNK, RET_CHUNK), lambda b, h: (h, 0, 0)),
            pl.BlockSpec((hps, RET_CHUNK, 1), lambda b, h: (h, 0, 0)),
            pl.BlockSpec((hps, RET_CHUNK, 1), lambda b, h: (h, 0, 0)),
            pl.BlockSpec((1, hps * RET_DIM), lambda b, h: (0, h)),
        ],
        out_specs=pl.BlockSpec((1, blocks, S, LANES), lambda b, h: (b, h, 0, 0)),
        out_shape=jax.ShapeDtypeStruct((B, RET_HEADS * RET_DIM // LANES, S, LANES), BF16),
        scratch_shapes=[pltpu.VMEM((hps, S, RET_DIM), F32),
                        pltpu.VMEM((hps, S // RET_CHUNK, RET_DIM, RET_DIM), F32),
                        pltpu.VMEM((hps, S // RET_CHUNK, RET_DIM, RET_DIM), BF16)],
        compiler_params=_params("arbitrary", "arbitrary"),
        name="retention",
    )(chunk_decay, proj, proj, proj, proj, inner, q_decay, k_decay,
      gn_w_l.reshape(1, RET_HEADS * RET_DIM))


def _outproj_kernel(ya_ref, yr_ref, ga_ref, gr_ref, x_ref, gate_ref, wpa_ref, wpr_ref, wo_ref,
                    fnw_ref, o_ref, *, final_norm):
    def cat(ref):
        return jnp.concatenate([ref[0, i] for i in range(ref.shape[1])], axis=-1)

    a = jnp.dot(cat(ya_ref), wpa_ref[0], preferred_element_type=F32)
    r = jnp.dot(cat(yr_ref), wpr_ref[0], preferred_element_type=F32)
    merged = (jax.nn.sigmoid(cat(ga_ref).astype(F32)) * a
              + jax.nn.sigmoid(cat(gr_ref).astype(F32)) * r)
    out = x_ref[0] + gate_ref[0] * jnp.dot(merged.astype(BF16), wo_ref[0],
                                           preferred_element_type=F32)
    if final_norm:
        ms = jnp.mean(out * out, axis=-1, keepdims=True)
        out = out * lax.rsqrt(ms + EPS) * fnw_ref[...]
    o_ref[0] = out


def _outproj(x, ya, yr, proj, mod_l, wpa, wpr, wo, final_norm_w, layer, final_norm):
    B, S, D = x.shape
    tm = 1024
    n_g = D // LANES
    weight = lambda w: pl.BlockSpec((1,) + w.shape[1:], lambda b, i: (layer, 0, 0))
    return pl.pallas_call(
        functools.partial(_outproj_kernel, final_norm=final_norm),
        grid=(B, S // tm),
        in_specs=[
            pl.BlockSpec((1, ya.shape[1], tm, LANES), lambda b, i: (b, 0, i, 0)),
            pl.BlockSpec((1, yr.shape[1], tm, LANES), lambda b, i: (b, 0, i, 0)),
            pl.BlockSpec((1, n_g, tm, LANES), lambda b, i: (b, BLK_GA // n_g, i, 0)),
            pl.BlockSpec((1, n_g, tm, LANES), lambda b, i: (b, BLK_GR // n_g, i, 0)),
            pl.BlockSpec((1, tm, D), lambda b, i: (b, i, 0)),
            pl.BlockSpec((1, 1, D), lambda b, i: (b, 0, 2)),
            weight(wpa), weight(wpr), weight(wo),
            pl.BlockSpec((1, D), lambda b, i: (0, 0)),
        ],
        out_specs=pl.BlockSpec((1, tm, D), lambda b, i: (b, i, 0)),
        out_shape=jax.ShapeDtypeStruct((B, S, D), F32),
        compiler_params=_params("arbitrary", "arbitrary"),
        name="outproj",
    )(ya, yr, proj, proj, x, mod_l, wpa, wpr, wo, final_norm_w.reshape(1, D))


def kernel(x, c, positions, norm_w, w_ada, b_ada, w_in, ret_gn_w, w_proj_attn, w_proj_ret, w_out,
           final_norm_w):
    B = x.shape[0]
    mod = _ada(c, w_ada, b_ada).reshape(DEPTH, B, 1, 3 * D_MODEL)
    cos, sin = _rope_tables(positions)
    w_in_b = jnp.concatenate([w_in[:, :, s:s + n] for s, n in _column_segments()],
                             axis=-1).astype(BF16)
    wpa_b = w_proj_attn.astype(BF16)
    wpr_b = w_proj_ret.astype(BF16)
    wo_b = w_out.astype(BF16)
    norm_w3 = norm_w.reshape(DEPTH, 1, D_MODEL)
    for l in range(DEPTH):
        proj = _inproj(x, mod[l], norm_w3, w_in_b, cos, sin, layer=l)
        ya = _attn(proj)
        yr = _ret(proj, ret_gn_w[l])
        x = _outproj(x, ya, yr, proj, mod[l], wpa_b, wpr_b, wo_b, final_norm_w, layer=l,
                     final_norm=(l == DEPTH - 1))
    return x
```

```python
import functools

import numpy as np
import jax
import jax.numpy as jnp
from jax import lax
from jax.experimental import pallas as pl
from jax.experimental.pallas import tpu as pltpu

D_MODEL = 1024
SEQ = 2048
DEPTH = 4
ATT_DILATIONS = (1, 4, 16)
N_ATT_GROUPS = 3
ATT_SLOTS = 4
ATT_HEAD_DIM = 128
ATT_BLOCK = 128
ATT_SLOTS_PER_STEP = 2
RET_HEADS = 4
RET_DIM = 256
RET_CHUNK = 512
RET_HEADS_PER_STEP = 2
ROPE_BASE = 10000.0
EPS = 1e-6
NEG_INF = -1e30

LANES = 128
COL_BLOCK = 512
BLOCKS_PER_STEP = COL_BLOCK // LANES
DOTS_PER_STEP = 2
ROW_CHUNK = 512
ATT_QKV = N_ATT_GROUPS * ATT_SLOTS * ATT_HEAD_DIM
IN_WIDTH = 3 * ATT_QKV + ATT_SLOTS * ATT_HEAD_DIM + 4 * RET_HEADS * RET_DIM + 2 * D_MODEL
N_COL_STEPS = IN_WIDTH // COL_BLOCK
N_BLOCKS = IN_WIDTH // LANES


def _column_segments():
    grp = ATT_SLOTS * ATT_HEAD_DIM
    qkv = lambda t, g: (t * ATT_QKV + g * grp, grp)
    seg = [qkv(0, 0), qkv(1, 0), qkv(2, 0), (3 * ATT_QKV, IN_WIDTH - 3 * ATT_QKV)]
    for g in range(1, N_ATT_GROUPS):
        seg += [qkv(0, g), qkv(1, g), qkv(2, g)]
    return seg


_GRP_BLOCKS = ATT_SLOTS * ATT_HEAD_DIM // LANES
_WIDE_BLOCKS = RET_HEADS * RET_DIM // LANES
BLK_ZA = 3 * _GRP_BLOCKS
BLK_QR = BLK_ZA + _GRP_BLOCKS
BLK_KR = BLK_QR + _WIDE_BLOCKS
BLK_VR = BLK_KR + _WIDE_BLOCKS
BLK_ZR = BLK_VR + _WIDE_BLOCKS
BLK_GA = BLK_ZR + _WIDE_BLOCKS
BLK_GR = BLK_GA + D_MODEL // LANES
_BLK_G1 = BLK_GR + D_MODEL // LANES
BLK_Q = (0,) + tuple(_BLK_G1 + (g - 1) * 3 * _GRP_BLOCKS for g in range(1, N_ATT_GROUPS))
BLK_K = tuple(b + _GRP_BLOCKS for b in BLK_Q)
BLK_V = tuple(b + 2 * _GRP_BLOCKS for b in BLK_Q)
N_NATURAL_STEPS = _BLK_G1 // BLOCKS_PER_STEP
STEPS_PER_GROUP = 3 * _GRP_BLOCKS // BLOCKS_PER_STEP

VMEM_LIMIT = 60 * 1024 * 1024
F32 = jnp.float32
BF16 = jnp.bfloat16


def _params(*sem):
    return pltpu.CompilerParams(dimension_semantics=sem, vmem_limit_bytes=VMEM_LIMIT)


def _ada_kernel(c_ref, w_ref, b_ref, o_ref):
    c = c_ref[...]
    c_act = (c * jax.nn.sigmoid(c)).astype(BF16)
    acc = jnp.dot(c_act, w_ref[0].astype(BF16), preferred_element_type=F32)
    o_ref[0] = acc + b_ref[0]


def _ada(c, w_ada, b_ada):
    B = c.shape[0]
    n_col = 3 * D_MODEL // D_MODEL
    return pl.pallas_call(
        _ada_kernel,
        grid=(DEPTH, n_col),
        in_specs=[
            pl.BlockSpec((B, D_MODEL), lambda l, j: (0, 0)),
            pl.BlockSpec((1, D_MODEL, D_MODEL), lambda l, j: (l, 0, j)),
            pl.BlockSpec((1, 1, D_MODEL), lambda l, j: (l, 0, j)),
        ],
        out_specs=pl.BlockSpec((1, B, D_MODEL), lambda l, j: (l, 0, j)),
        out_shape=jax.ShapeDtypeStruct((DEPTH, B, 3 * D_MODEL), F32),
        compiler_params=_params("arbitrary", "arbitrary"),
        name="ada",
    )(c, w_ada, b_ada.reshape(DEPTH, 1, 3 * D_MODEL))


def _rope_kernel(pos_ref, theta_ref, cos_ref, sin_ref):
    ang = pos_ref[0].astype(F32) * theta_ref[...]
    cos_ref[0] = jnp.cos(ang)
    sin_ref[0] = jnp.sin(ang)


def _rope_tables(positions):
    B, S = positions.shape
    half = RET_DIM // 2
    theta = ROPE_BASE ** (-jnp.arange(half, dtype=F32) / half)
    spec = pl.BlockSpec((1, S, half), lambda b: (b, 0, 0))
    return pl.pallas_call(
        _rope_kernel,
        grid=(B,),
        in_specs=[pl.BlockSpec((1, S, 1), lambda b: (b, 0, 0)),
                  pl.BlockSpec((1, half), lambda b: (0, 0))],
        out_specs=[spec, spec],
        out_shape=[jax.ShapeDtypeStruct((B, S, half), F32)] * 2,
        compiler_params=_params("arbitrary"),
        name="rope",
    )(positions.reshape(B, S, 1), theta.reshape(1, half))


def _row_order_of_column_block(cb):
    return 0 if cb < N_NATURAL_STEPS else 1 + (cb - N_NATURAL_STEPS) // STEPS_PER_GROUP


def _rotary_scale_of_column_block(cb):
    blk = cb * BLOCKS_PER_STEP
    if BLK_QR <= blk < BLK_KR:
        return 1.0
    if BLK_KR <= blk < BLK_VR:
        return RET_DIM ** -0.5
    return None


def _inproj_kernel(x_hbm, mod_ref, nw_ref, w_ref, cos_ref, sin_ref, o_ref,
                   h0_ref, h1_ref, h2_ref, hn_ref, hp_ref, x_ref, x_sem):
    b, j = pl.program_id(0), pl.program_id(1)
    S = x_ref.shape[0]
    rows = 128

    def x_copy(seq):
        return pltpu.make_async_copy(x_hbm.at[seq], x_ref, x_sem)

    @pl.when((j == 0) & (b == 0))
    def _():
        x_copy(b).start()

    @pl.when((j == 1) & (b + 1 < pl.num_programs(0)))
    def _():
        x_copy(b + 1).start()

    n_lane_blocks = D_MODEL // LANES
    h_refs = (h0_ref, h1_ref, h2_ref)
    half = RET_DIM // 2

    def project(h_ref, i, rotary_scale, mc):
        ridx = pl.ds(mc * ROW_CHUNK, ROW_CHUNK)
        res = jnp.dot(h_ref[ridx, :], w_ref[0, :, i * COL_BLOCK:(i + 1) * COL_BLOCK],
                      preferred_element_type=F32)
        blocks = [res[:, c * LANES:(c + 1) * LANES] for c in range(BLOCKS_PER_STEP)]
        if rotary_scale is not None:
            cos, sin = cos_ref[0, ridx, :], sin_ref[0, ridx, :]
            if rotary_scale != 1.0:
                cos, sin = cos * rotary_scale, sin * rotary_scale
            for c in range(0, BLOCKS_PER_STEP, RET_DIM // LANES):
                t1, t2 = blocks[c], blocks[c + half // LANES]
                blocks[c] = t1 * cos - t2 * sin
                blocks[c + half // LANES] = t2 * cos + t1 * sin
        for c, blk in enumerate(blocks):
            o_ref[0, i * BLOCKS_PER_STEP + c, ridx, :] = blk.astype(BF16)

    def natural_slice(r0, wmul, shift):
        idx = pl.ds(r0, rows)
        xs = x_ref[idx, :]
        ms = jnp.mean(xs * xs, axis=-1, keepdims=True)
        hv = xs * lax.rsqrt(ms + EPS) * wmul + shift
        h0_ref[idx, :] = hv.astype(BF16)
        for cb in range(n_lane_blocks):
            hn_ref[cb, idx, :] = hv[:, cb * LANES:(cb + 1) * LANES]

    def hop_slice(hop, c):
        step = ATT_DILATIONS[hop + 1] // ATT_DILATIONS[hop]
        n_src = S // step
        src_ref = hp_ref if hop else hn_ref
        p0 = pl.multiple_of(c * rows, rows)
        start = (p0 % n_src) * step + p0 // n_src
        pieces = [src_ref[cb, pl.ds(start, rows, stride=step), :] for cb in range(n_lane_blocks)]
        h_refs[hop + 1][pl.ds(p0, rows), :] = jnp.concatenate(pieces, axis=-1).astype(BF16)
        if hop + 2 < len(ATT_DILATIONS):
            for cb in range(n_lane_blocks):
                hp_ref[cb, pl.ds(p0, rows), :] = pieces[cb]

    n_row_chunks = S // ROW_CHUNK
    slices_per_step = n_row_chunks
    hop_steps = S // rows // slices_per_step
    for hop in range(len(ATT_DILATIONS) - 1):
        first_use = (N_NATURAL_STEPS + hop * STEPS_PER_GROUP) // DOTS_PER_STEP
        assert 1 + (hop + 1) * hop_steps <= first_use

    def phase_of_step(s):
        if s == 0:
            return "natural"
        hop = (s - 1) // hop_steps
        return hop if hop + 1 < len(ATT_DILATIONS) else None

    steps_by_kind = {}
    for s in range(N_COL_STEPS // DOTS_PER_STEP):
        cbs = [s * DOTS_PER_STEP + i for i in range(DOTS_PER_STEP)]
        kind = tuple((_row_order_of_column_block(cb), _rotary_scale_of_column_block(cb))
                     for cb in cbs)
        steps_by_kind.setdefault((kind, phase_of_step(s)), []).append(s)
    for (kind, phase), steps in steps_by_kind.items():
        @pl.when(functools.reduce(jnp.logical_or, [j == s for s in steps]))
        def _(kind=kind, phase=phase):
            if phase == "natural":
                x_copy(b).wait()
                shift = mod_ref[0, :, 0:D_MODEL]
                wmul = nw_ref[0] * (1.0 + mod_ref[0, :, D_MODEL:2 * D_MODEL])
            for mc in range(n_row_chunks):
                if phase == "natural":
                    for r0 in range(mc * ROW_CHUNK, (mc + 1) * ROW_CHUNK, rows):
                        natural_slice(r0, wmul, shift)
                for i, (order, rotary_scale) in enumerate(kind):
                    project(h_refs[order], i, rotary_scale, mc)
                if phase not in ("natural", None):
                    first = 1 + phase * hop_steps
                    hop_slice(phase, (j - first) * slices_per_step + mc)


def _inproj(x, mod_l, norm_w, w_in, cos, sin, layer):
    B, S, D = x.shape
    tab = pl.BlockSpec((1, S, LANES), lambda b, j: (b, 0, 0))
    return pl.pallas_call(
        _inproj_kernel,
        grid=(B, N_COL_STEPS // DOTS_PER_STEP),
        in_specs=[
            pl.BlockSpec(memory_space=pl.ANY),
            pl.BlockSpec((1, 1, 3 * D), lambda b, j: (b, 0, 0)),
            pl.BlockSpec((1, 1, D), lambda b, j: (layer, 0, 0)),
            pl.BlockSpec((1, D, DOTS_PER_STEP * COL_BLOCK), lambda b, j: (layer, 0, j)),
            tab, tab,
        ],
        out_specs=pl.BlockSpec((1, DOTS_PER_STEP * BLOCKS_PER_STEP, S, LANES),
                               lambda b, j: (b, j, 0, 0)),
        out_shape=jax.ShapeDtypeStruct((B, N_BLOCKS, S, LANES), BF16),
        scratch_shapes=[pltpu.VMEM((S, D), BF16)] * N_ATT_GROUPS
                       + [pltpu.VMEM((D // LANES, S, LANES), F32)] * 2
                       + [pltpu.VMEM((S, D), F32), pltpu.SemaphoreType.DMA(())],
        compiler_params=_params("arbitrary", "arbitrary"),
        name="inproj",
    )(x, mod_l, norm_w, w_in, cos, sin)


def _attn_kernel(q0, k0, v0, q1, k1, v1, q2, k2, v2, z_ref, y_ref,
                 sb_sc, sd_sc, m_sc, acc_sc, l_sc):
    S = y_ref.shape[2]
    T = ATT_BLOCK
    c_exp2 = ATT_HEAD_DIM ** -0.5 * np.log2(np.e)
    row = lax.broadcasted_iota(jnp.int32, (T, 2 * T), 0)
    col = lax.broadcasted_iota(jnp.int32, (T, 2 * T), 1)
    mask_band = (col >= row) & (col <= row + T)
    mask_diag = (lax.broadcasted_iota(jnp.int32, (T, T), 1)
                 <= lax.broadcasted_iota(jnp.int32, (T, T), 0))
    groups = ((q0, k0, v0), (q1, k1, v1), (q2, k2, v2))
    d1, d2 = ATT_DILATIONS[1], ATT_DILATIONS[2]
    len1 = S // d1
    nb1 = len1 // T
    n_band0 = S // T - 1

    def scores(sl, g, q_start, band, blk, out_idx, m_idx):
        q_ref, k_ref, _ = groups[g]
        q = q_ref[0, sl, pl.ds(q_start, T), :]
        if band:
            k = k_ref[0, sl, pl.ds(q_start - T, 2 * T), :]
        else:
            k = k_ref[0, sl, pl.ds(q_start, T), :]
        s = lax.dot_general(q, k, (((1,), (1,)), ((), ())), preferred_element_type=F32) * c_exp2
        s = jnp.where(mask_band if band else mask_diag, s, NEG_INF)
        (sb_sc if band else sd_sc)[blk] = s
        m_sc[g, out_idx, :] = jnp.broadcast_to(jnp.max(s, axis=-1, keepdims=True), (T, LANES))

    def weighted_values(sl, g, q_start, band, blk, out_idx, m_idx):
        v_ref = groups[g][2]
        m = m_sc[min(g, 1), m_idx, :]
        if band:
            p = jnp.exp2(sb_sc[blk] - jnp.concatenate([m, m], axis=-1))
            v = v_ref[0, sl, pl.ds(q_start - T, 2 * T), :]
        else:
            p = jnp.exp2(sd_sc[blk] - m)
            v = v_ref[0, sl, pl.ds(q_start, T), :]
        v1 = jnp.concatenate([v, jnp.ones_like(v)], axis=-1)
        r = jnp.dot(p.astype(BF16), v1, preferred_element_type=F32)
        acc_sc[g, out_idx, :] = r[:, :LANES]
        l_sc[g, out_idx, :] = r[:, LANES:]

    blocks = []
    for i in range(S // T):
        blocks.append((0, i * T, i > 0, max(i - 1, 0), pl.ds(i * T, T), pl.ds(i * T, T)))
    for r in range(d1):
        for n in range(nb1):
            idx = pl.ds(r * len1 + n * T, T)
            blk = 1 + r if n == 0 else n_band0 + r * (nb1 - 1) + n - 1
            blocks.append((1, r * len1 + n * T, n > 0, blk, idx, idx))
    ratio = d2 // d1
    for r in range(d2):
        idx = pl.ds((r % d1) * len1 + r // d1, T, stride=ratio)
        blocks.append((2, r * T, False, 1 + d1 + r, idx, idx))

    def all_blocks(fn):
        for blk in blocks:
            fn(*blk)

    rows = 256

    def natural_rows(c):
        p0 = pl.multiple_of(c * rows, rows)
        return pl.ds(p0, rows), pl.ds((p0 % len1) * d1 + p0 // len1, rows, stride=d1)

    def shared_max(c, carry):
        idx, nat = natural_rows(c)
        m = jnp.maximum(jnp.maximum(m_sc[0, nat, :], m_sc[1, idx, :]), m_sc[2, idx, :])
        m_sc[0, nat, :] = m
        m_sc[1, idx, :] = m
        return carry

    def combine(c, carry):
        idx, nat = natural_rows(c)
        o = ((acc_sc[0, nat, :] + acc_sc[1, idx, :] + acc_sc[2, idx, :])
             / (l_sc[0, nat, :] + l_sc[1, idx, :] + l_sc[2, idx, :]))
        acc_sc[0, nat, :] = o
        return carry

    def finish(sl, c, carry):
        idx = pl.ds(pl.multiple_of(c * rows, rows), rows)
        hz = 0.5 * z_ref[0, sl, idx, :].astype(F32)
        silu = hz + hz * jnp.tanh(hz)
        y_ref[0, sl, idx, :] = (acc_sc[0, idx, :] * silu).astype(BF16)
        return carry

    for sl in range(ATT_SLOTS_PER_STEP):
        all_blocks(functools.partial(scores, sl))
        lax.fori_loop(0, S // rows, shared_max, 0, unroll=True)
        all_blocks(functools.partial(weighted_values, sl))
        lax.fori_loop(0, S // rows, combine, 0, unroll=True)
        lax.fori_loop(0, S // rows, functools.partial(finish, sl), 0)


def _attn(proj):
    B, _, S, _ = proj.shape
    sps = ATT_SLOTS_PER_STEP

    def spec(base):
        return pl.BlockSpec((1, sps, S, LANES), lambda b, s: (b, base // sps + s, 0, 0))

    in_specs = []
    for g in range(N_ATT_GROUPS):
        in_specs += [spec(BLK_Q[g]), spec(BLK_K[g]), spec(BLK_V[g])]
    in_specs.append(spec(BLK_ZA))
    n_diag = sum(ATT_DILATIONS)
    n_band = N_ATT_GROUPS * S // ATT_BLOCK - n_diag
    return pl.pallas_call(
        _attn_kernel,
        grid=(B, ATT_SLOTS // sps),
        in_specs=in_specs,
        out_specs=pl.BlockSpec((1, sps, S, LANES), lambda b, s: (b, s, 0, 0)),
        out_shape=jax.ShapeDtypeStruct((B, ATT_SLOTS, S, LANES), BF16),
        scratch_shapes=[pltpu.VMEM((n_band, ATT_BLOCK, 2 * ATT_BLOCK), F32),
                        pltpu.VMEM((n_diag, ATT_BLOCK, ATT_BLOCK), F32),
                        pltpu.VMEM((N_ATT_GROUPS, S, LANES), F32),
                        pltpu.VMEM((N_ATT_GROUPS, S, LANES), F32),
                        pltpu.VMEM((N_ATT_GROUPS, S, LANES), F32)],
        compiler_params=_params("arbitrary", "arbitrary"),
        name="attn",
    )(*([proj] * 10))


def _ret_decay_tables():
    H, C = RET_HEADS, RET_CHUNK
    log_g = np.log1p(-np.exp2(-5.0 - np.arange(H, dtype=np.float64)))
    idx = np.arange(C, dtype=np.float64)
    diff = idx[:, None] - idx[None, :]
    inner = np.where(diff >= 0, np.exp(log_g[:, None, None] * np.maximum(diff, 0.0)), 0.0)
    q_decay = np.exp(log_g[:, None] * (idx + 1.0))
    k_decay = np.exp(log_g[:, None] * (C - 1.0 - idx))
    chunk_decay = np.exp(log_g * C)
    f = lambda a: jnp.asarray(a, dtype=F32)
    return f(inner), f(q_decay[:, :, None]), f(k_decay[:, :, None]), f(chunk_decay)


def _ret_kernel(cd_ref, q_ref, k_ref, v_ref, z_ref, inner_ref, qd_ref, kd_ref,
                gnw_ref, y_ref, intra_sc, u_sc, st_sc):
    S = y_ref.shape[2]
    C = RET_CHUNK
    n_chunks = S // C
    half = RET_DIM // 2
    per_head = RET_DIM // LANES

    def head(ref, hh, idx):
        return jnp.concatenate([ref[0, hh * per_head + i, idx, :] for i in range(per_head)],
                               axis=-1)

    def state_free(hh):
        for n in range(n_chunks):
            idx = pl.ds(n * C, C)
            qc, kc, vc = head(q_ref, hh, idx), head(k_ref, hh, idx), head(v_ref, hh, idx)
            att = lax.dot_general(qc, kc, (((1,), (1,)), ((), ())), preferred_element_type=F32)
            att = (att * inner_ref[hh]).astype(BF16)
            intra_sc[hh, idx, :] = jnp.dot(att, vc, preferred_element_type=F32)
            if n + 1 < n_chunks:
                kdt = (kc.astype(F32) * kd_ref[hh]).T.astype(BF16)
                u_sc[hh, n] = jnp.dot(kdt, vc, preferred_element_type=F32)

    def recurrence(hh):
        cd = cd_ref[pl.program_id(1) * RET_HEADS_PER_STEP + hh]
        slab = 64
        for r0 in range(0, RET_DIM, slab):
            state = jnp.zeros((slab, RET_DIM), F32)
            for n in range(1, n_chunks):
                state = state * cd + u_sc[hh, n - 1, r0:r0 + slab, :]
                st_sc[hh, n, r0:r0 + slab, :] = state.astype(BF16)

    def finish(hh):
        gnw = gnw_ref[:, hh * RET_DIM:(hh + 1) * RET_DIM]
        for n in range(n_chunks):
            idx = pl.ds(n * C, C)
            out = intra_sc[hh, idx, :]
            if n > 0:
                out = out + jnp.dot(head(q_ref, hh, idx), st_sc[hh, n],
                                    preferred_element_type=F32) * qd_ref[hh]
            mu = jnp.mean(out, axis=-1, keepdims=True)
            cen = out - mu
            var = jnp.mean(cen * cen, axis=-1, keepdims=True)
            o = cen * lax.rsqrt(var + EPS) * gnw
            z = head(z_ref, hh, idx).astype(F32)
            y = (o * (z * jax.nn.sigmoid(z))).astype(BF16)
            y_ref[0, hh * per_head, idx, :] = y[:, :half]
            y_ref[0, hh * per_head + 1, idx, :] = y[:, half:]

    for hh in range(RET_HEADS_PER_STEP):
        state_free(hh)
        if hh > 0:
            finish(hh - 1)
        recurrence(hh)
    finish(RET_HEADS_PER_STEP - 1)


def _ret(proj, gn_w_l):
    B, _, S, _ = proj.shape
    inner, q_decay, k_decay, chunk_decay = _ret_decay_tables()
    blocks = RET_HEADS_PER_STEP * RET_DIM // LANES
    hps = RET_HEADS_PER_STEP

    def spec(base):
        return pl.BlockSpec((1, blocks, S, LANES), lambda b, h: (b, base // blocks + h, 0, 0))

    return pl.pallas_call(
        _ret_kernel,
        grid=(B, RET_HEADS // hps),
        in_specs=[
            pl.BlockSpec(memory_space=pltpu.SMEM),
            spec(BLK_QR), spec(BLK_KR), spec(BLK_VR), spec(BLK_ZR),
            pl.BlockSpec((hps, RET_CHUNK, RET_CHUNK), lambda b, h: (h, 0, 0)),
            pl.BlockSpec((hps, RET_CHUNK, 1), lambda b, h: (h, 0, 0)),
            pl.BlockSpec((hps, RET_CHUNK, 1), lambda b, h: (h, 0, 0)),
            pl.BlockSpec((1, hps * RET_DIM), lambda b, h: (0, h)),
        ],
        out_specs=pl.BlockSpec((1, blocks, S, LANES), lambda b, h: (b, h, 0, 0)),
        out_shape=jax.ShapeDtypeStruct((B, RET_HEADS * RET_DIM // LANES, S, LANES), BF16),
        scratch_shapes=[pltpu.VMEM((hps, S, RET_DIM), F32),
                        pltpu.VMEM((hps, S // RET_CHUNK, RET_DIM, RET_DIM), F32),
                        pltpu.VMEM((hps, S // RET_CHUNK, RET_DIM, RET_DIM), BF16)],
        compiler_params=_params("arbitrary", "arbitrary"),
        name="retention",
    )(chunk_decay, proj, proj, proj, proj, inner, q_decay, k_decay,
      gn_w_l.reshape(1, RET_HEADS * RET_DIM))


def _outproj_kernel(ya_ref, yr_ref, ga_ref, gr_ref, x_ref, gate_ref, wpa_ref, wpr_ref, wo_ref,
                    fnw_ref, o_ref, *, final_norm):
    for r0 in range(0, x_ref.shape[1], ROW_CHUNK):
        ridx = pl.ds(r0, ROW_CHUNK)

        def cat(ref):
            return jnp.concatenate([ref[0, i, ridx, :] for i in range(ref.shape[1])], axis=-1)

        a = jnp.dot(cat(ya_ref), wpa_ref[0], preferred_element_type=F32)
        r = jnp.dot(cat(yr_ref), wpr_ref[0], preferred_element_type=F32)
        merged = (jax.nn.sigmoid(cat(ga_ref).astype(F32)) * a
                  + jax.nn.sigmoid(cat(gr_ref).astype(F32)) * r)
        out = x_ref[0, ridx, :] + gate_ref[0] * jnp.dot(merged.astype(BF16), wo_ref[0],
                                                        preferred_element_type=F32)
        if final_norm:
            ms = jnp.mean(out * out, axis=-1, keepdims=True)
            out = out * lax.rsqrt(ms + EPS) * fnw_ref[...]
        o_ref[0, ridx, :] = out


def _outproj(x, ya, yr, proj, mod_l, wpa, wpr, wo, final_norm_w, layer, final_norm):
    B, S, D = x.shape
    tm = 1024
    n_g = D // LANES
    weight = lambda w: pl.BlockSpec((1,) + w.shape[1:], lambda b, i: (layer, 0, 0))
    return pl.pallas_call(
        functools.partial(_outproj_kernel, final_norm=final_norm),
        grid=(B, S // tm),
        in_specs=[
            pl.BlockSpec((1, ya.shape[1], tm, LANES), lambda b, i: (b, 0, i, 0)),
            pl.BlockSpec((1, yr.shape[1], tm, LANES), lambda b, i: (b, 0, i, 0)),
            pl.BlockSpec((1, n_g, tm, LANES), lambda b, i: (b, BLK_GA // n_g, i, 0)),
            pl.BlockSpec((1, n_g, tm, LANES), lambda b, i: (b, BLK_GR // n_g, i, 0)),
            pl.BlockSpec((1, tm, D), lambda b, i: (b, i, 0)),
            pl.BlockSpec((1, 1, D), lambda b, i: (b, 0, 2)),
            weight(wpa), weight(wpr), weight(wo),
            pl.BlockSpec((1, D), lambda b, i: (0, 0)),
        ],
        out_specs=pl.BlockSpec((1, tm, D), lambda b, i: (b, i, 0)),
        out_shape=jax.ShapeDtypeStruct((B, S, D), F32),
        compiler_params=_params("arbitrary", "arbitrary"),
        name="outproj",
    )(ya, yr, proj, proj, x, mod_l, wpa, wpr, wo, final_norm_w.reshape(1, D))


def kernel(x, c, positions, norm_w, w_ada, b_ada, w_in, ret_gn_w, w_proj_attn, w_proj_ret, w_out,
           final_norm_w):
    B = x.shape[0]
    mod = _ada(c, w_ada, b_ada).reshape(DEPTH, B, 1, 3 * D_MODEL)
    cos, sin = _rope_tables(positions)
    w_in_b = jnp.concatenate([w_in[:, :, s:s + n] for s, n in _column_segments()],
                             axis=-1).astype(BF16)
    wpa_b = w_proj_attn.astype(BF16)
    wpr_b = w_proj_ret.astype(BF16)
    wo_b = w_out.astype(BF16)
    norm_w3 = norm_w.reshape(DEPTH, 1, D_MODEL)
    for l in range(DEPTH):
        proj = _inproj(x, mod[l], norm_w3, w_in_b, cos, sin, layer=l)
        ya = _attn(proj)
        yr = _ret(proj, ret_gn_w[l])
        x = _outproj(x, ya, yr, proj, mod[l], wpa_b, wpr_b, wo_b, final_norm_w, layer=l,
                     final_norm=(l == DEPTH - 1))
    return x
```

```python
import functools

import numpy as np
import jax
import jax.numpy as jnp
from jax import lax
from jax.experimental import pallas as pl
from jax.experimental.pallas import tpu as pltpu

D_MODEL = 1024
SEQ = 2048
DEPTH = 4
ATT_DILATIONS = (1, 4, 16)
N_ATT_GROUPS = 3
ATT_SLOTS = 4
ATT_HEAD_DIM = 128
ATT_BLOCK = 128
ATT_SLOTS_PER_STEP = 2
RET_HEADS = 4
RET_DIM = 256
RET_CHUNK = 512
RET_HEADS_PER_STEP = 4
ROPE_BASE = 10000.0
EPS = 1e-6
NEG_INF = -1e30

LANES = 128
COL_BLOCK = 512
BLOCKS_PER_STEP = COL_BLOCK // LANES
DOTS_PER_STEP = 2
ROW_CHUNK = 512
ATT_QKV = N_ATT_GROUPS * ATT_SLOTS * ATT_HEAD_DIM
IN_WIDTH = 3 * ATT_QKV + ATT_SLOTS * ATT_HEAD_DIM + 4 * RET_HEADS * RET_DIM + 2 * D_MODEL
N_COL_STEPS = IN_WIDTH // COL_BLOCK
N_BLOCKS = IN_WIDTH // LANES


def _column_segments():
    grp = ATT_SLOTS * ATT_HEAD_DIM
    qkv = lambda t, g: (t * ATT_QKV + g * grp, grp)
    seg = [qkv(0, 0), qkv(1, 0), qkv(2, 0), (3 * ATT_QKV, IN_WIDTH - 3 * ATT_QKV)]
    for g in range(1, N_ATT_GROUPS):
        seg += [qkv(0, g), qkv(1, g), qkv(2, g)]
    return seg


_GRP_BLOCKS = ATT_SLOTS * ATT_HEAD_DIM // LANES
_WIDE_BLOCKS = RET_HEADS * RET_DIM // LANES
BLK_ZA = 3 * _GRP_BLOCKS
BLK_QR = BLK_ZA + _GRP_BLOCKS
BLK_KR = BLK_QR + _WIDE_BLOCKS
BLK_VR = BLK_KR + _WIDE_BLOCKS
BLK_ZR = BLK_VR + _WIDE_BLOCKS
BLK_GA = BLK_ZR + _WIDE_BLOCKS
BLK_GR = BLK_GA + D_MODEL // LANES
_BLK_G1 = BLK_GR + D_MODEL // LANES
BLK_Q = (0,) + tuple(_BLK_G1 + (g - 1) * 3 * _GRP_BLOCKS for g in range(1, N_ATT_GROUPS))
BLK_K = tuple(b + _GRP_BLOCKS for b in BLK_Q)
BLK_V = tuple(b + 2 * _GRP_BLOCKS for b in BLK_Q)
N_NATURAL_STEPS = _BLK_G1 // BLOCKS_PER_STEP
STEPS_PER_GROUP = 3 * _GRP_BLOCKS // BLOCKS_PER_STEP

VMEM_LIMIT = 60 * 1024 * 1024
F32 = jnp.float32
BF16 = jnp.bfloat16


def _params(*sem):
    return pltpu.CompilerParams(dimension_semantics=sem, vmem_limit_bytes=VMEM_LIMIT)


def _ada_kernel(c_ref, w_ref, b_ref, o_ref):
    c = c_ref[...]
    c_act = (c * jax.nn.sigmoid(c)).astype(BF16)
    acc = jnp.dot(c_act, w_ref[0].astype(BF16), preferred_element_type=F32)
    o_ref[0] = acc + b_ref[0]


def _ada(c, w_ada, b_ada):
    B = c.shape[0]
    n_col = 3 * D_MODEL // D_MODEL
    return pl.pallas_call(
        _ada_kernel,
        grid=(DEPTH, n_col),
        in_specs=[
            pl.BlockSpec((B, D_MODEL), lambda l, j: (0, 0)),
            pl.BlockSpec((1, D_MODEL, D_MODEL), lambda l, j: (l, 0, j)),
            pl.BlockSpec((1, 1, D_MODEL), lambda l, j: (l, 0, j)),
        ],
        out_specs=pl.BlockSpec((1, B, D_MODEL), lambda l, j: (l, 0, j)),
        out_shape=jax.ShapeDtypeStruct((DEPTH, B, 3 * D_MODEL), F32),
        compiler_params=_params("arbitrary", "arbitrary"),
        name="ada",
    )(c, w_ada, b_ada.reshape(DEPTH, 1, 3 * D_MODEL))


def _rope_kernel(pos_ref, theta_ref, cos_ref, sin_ref):
    ang = pos_ref[0].astype(F32) * theta_ref[...]
    cos_ref[0] = jnp.cos(ang)
    sin_ref[0] = jnp.sin(ang)


def _rope_tables(positions):
    B, S = positions.shape
    half = RET_DIM // 2
    theta = ROPE_BASE ** (-jnp.arange(half, dtype=F32) / half)
    spec = pl.BlockSpec((1, S, half), lambda b: (b, 0, 0))
    return pl.pallas_call(
        _rope_kernel,
        grid=(B,),
        in_specs=[pl.BlockSpec((1, S, 1), lambda b: (b, 0, 0)),
                  pl.BlockSpec((1, half), lambda b: (0, 0))],
        out_specs=[spec, spec],
        out_shape=[jax.ShapeDtypeStruct((B, S, half), F32)] * 2,
        compiler_params=_params("arbitrary"),
        name="rope",
    )(positions.reshape(B, S, 1), theta.reshape(1, half))


def _row_order_of_column_block(cb):
    return 0 if cb < N_NATURAL_STEPS else 1 + (cb - N_NATURAL_STEPS) // STEPS_PER_GROUP


def _rotary_scale_of_column_block(cb):
    blk = cb * BLOCKS_PER_STEP
    if BLK_QR <= blk < BLK_KR:
        return 1.0
    if BLK_KR <= blk < BLK_VR:
        return RET_DIM ** -0.5
    return None


def _inproj_kernel(x_hbm, mod_ref, nw_ref, w_ref, cos_ref, sin_ref, o_ref,
                   h0_ref, h1_ref, h2_ref, hn_ref, hp_ref, x_ref, x_sem):
    b, j = pl.program_id(0), pl.program_id(1)
    S = x_ref.shape[0]
    rows = 128

    def x_copy(seq):
        return pltpu.make_async_copy(x_hbm.at[seq], x_ref, x_sem)

    @pl.when((j == 0) & (b == 0))
    def _():
        x_copy(b).start()

    @pl.when((j == 1) & (b + 1 < pl.num_programs(0)))
    def _():
        x_copy(b + 1).start()

    n_lane_blocks = D_MODEL // LANES
    h_refs = (h0_ref, h1_ref, h2_ref)
    half = RET_DIM // 2

    def project(h_ref, i, rotary_scale, mc):
        ridx = pl.ds(mc * ROW_CHUNK, ROW_CHUNK)
        res = jnp.dot(h_ref[ridx, :], w_ref[0, :, i * COL_BLOCK:(i + 1) * COL_BLOCK],
                      preferred_element_type=F32)
        blocks = [res[:, c * LANES:(c + 1) * LANES] for c in range(BLOCKS_PER_STEP)]
        if rotary_scale is not None:
            cos, sin = cos_ref[0, ridx, :], sin_ref[0, ridx, :]
            if rotary_scale != 1.0:
                cos, sin = cos * rotary_scale, sin * rotary_scale
            for c in range(0, BLOCKS_PER_STEP, RET_DIM // LANES):
                t1, t2 = blocks[c], blocks[c + half // LANES]
                blocks[c] = t1 * cos - t2 * sin
                blocks[c + half // LANES] = t2 * cos + t1 * sin
        for c, blk in enumerate(blocks):
            o_ref[0, i * BLOCKS_PER_STEP + c, ridx, :] = blk.astype(BF16)

    def natural_slice(r0, wmul, shift):
        idx = pl.ds(r0, rows)
        xs = x_ref[idx, :]
        ms = jnp.mean(xs * xs, axis=-1, keepdims=True)
        hv = xs * lax.rsqrt(ms + EPS) * wmul + shift
        h0_ref[idx, :] = hv.astype(BF16)
        for cb in range(n_lane_blocks):
            hn_ref[cb, idx, :] = hv[:, cb * LANES:(cb + 1) * LANES]

    def hop_slice(hop, c):
        step = ATT_DILATIONS[hop + 1] // ATT_DILATIONS[hop]
        n_src = S // step
        src_ref = hp_ref if hop else hn_ref
        p0 = pl.multiple_of(c * rows, rows)
        start = (p0 % n_src) * step + p0 // n_src
        pieces = [src_ref[cb, pl.ds(start, rows, stride=step), :] for cb in range(n_lane_blocks)]
        h_refs[hop + 1][pl.ds(p0, rows), :] = jnp.concatenate(pieces, axis=-1).astype(BF16)
        if hop + 2 < len(ATT_DILATIONS):
            for cb in range(n_lane_blocks):
                hp_ref[cb, pl.ds(p0, rows), :] = pieces[cb]

    n_row_chunks = S // ROW_CHUNK
    slices_per_step = n_row_chunks
    hop_steps = S // rows // slices_per_step
    for hop in range(len(ATT_DILATIONS) - 1):
        first_use = (N_NATURAL_STEPS + hop * STEPS_PER_GROUP) // DOTS_PER_STEP
        assert 1 + (hop + 1) * hop_steps <= first_use

    def phase_of_step(s):
        if s == 0:
            return "natural"
        hop = (s - 1) // hop_steps
        return hop if hop + 1 < len(ATT_DILATIONS) else None

    steps_by_kind = {}
    for s in range(N_COL_STEPS // DOTS_PER_STEP):
        cbs = [s * DOTS_PER_STEP + i for i in range(DOTS_PER_STEP)]
        kind = tuple((_row_order_of_column_block(cb), _rotary_scale_of_column_block(cb))
                     for cb in cbs)
        steps_by_kind.setdefault((kind, phase_of_step(s)), []).append(s)
    for (kind, phase), steps in steps_by_kind.items():
        @pl.when(functools.reduce(jnp.logical_or, [j == s for s in steps]))
        def _(kind=kind, phase=phase):
            if phase == "natural":
                x_copy(b).wait()
                shift = mod_ref[0, :, 0:D_MODEL]
                wmul = nw_ref[0] * (1.0 + mod_ref[0, :, D_MODEL:2 * D_MODEL])
            for mc in range(n_row_chunks):
                if phase == "natural":
                    for r0 in range(mc * ROW_CHUNK, (mc + 1) * ROW_CHUNK, rows):
                        natural_slice(r0, wmul, shift)
                for i, (order, rotary_scale) in enumerate(kind):
                    project(h_refs[order], i, rotary_scale, mc)
                if phase not in ("natural", None):
                    first = 1 + phase * hop_steps
                    hop_slice(phase, (j - first) * slices_per_step + mc)


def _inproj(x, mod_l, norm_w, w_in, cos, sin, layer):
    B, S, D = x.shape
    tab = pl.BlockSpec((1, S, LANES), lambda b, j: (b, 0, 0))
    return pl.pallas_call(
        _inproj_kernel,
        grid=(B, N_COL_STEPS // DOTS_PER_STEP),
        in_specs=[
            pl.BlockSpec(memory_space=pl.ANY),
            pl.BlockSpec((1, 1, 3 * D), lambda b, j: (b, 0, 0)),
            pl.BlockSpec((1, 1, D), lambda b, j: (layer, 0, 0)),
            pl.BlockSpec((1, D, DOTS_PER_STEP * COL_BLOCK), lambda b, j: (layer, 0, j)),
            tab, tab,
        ],
        out_specs=pl.BlockSpec((1, DOTS_PER_STEP * BLOCKS_PER_STEP, S, LANES),
                               lambda b, j: (b, j, 0, 0)),
        out_shape=jax.ShapeDtypeStruct((B, N_BLOCKS, S, LANES), BF16),
        scratch_shapes=[pltpu.VMEM((S, D), BF16)] * N_ATT_GROUPS
                       + [pltpu.VMEM((D // LANES, S, LANES), F32)] * 2
                       + [pltpu.VMEM((S, D), F32), pltpu.SemaphoreType.DMA(())],
        compiler_params=_params("arbitrary", "arbitrary"),
        name="inproj",
    )(x, mod_l, norm_w, w_in, cos, sin)


def _attn_kernel(q0, k0, v0, q1, k1, v1, q2, k2, v2, z_ref, y_ref,
                 sb_sc, sd_sc, m_sc, acc_sc, l_sc):
    S = y_ref.shape[2]
    T = ATT_BLOCK
    c_exp2 = ATT_HEAD_DIM ** -0.5 * np.log2(np.e)
    row = lax.broadcasted_iota(jnp.int32, (T, 2 * T), 0)
    col = lax.broadcasted_iota(jnp.int32, (T, 2 * T), 1)
    mask_band = (col >= row) & (col <= row + T)
    mask_diag = (lax.broadcasted_iota(jnp.int32, (T, T), 1)
                 <= lax.broadcasted_iota(jnp.int32, (T, T), 0))
    groups = ((q0, k0, v0), (q1, k1, v1), (q2, k2, v2))
    d1, d2 = ATT_DILATIONS[1], ATT_DILATIONS[2]
    len1 = S // d1
    nb1 = len1 // T
    n_band0 = S // T - 1

    def scores(sl, g, q_start, band, blk, out_idx, m_idx):
        q_ref, k_ref, _ = groups[g]
        q = q_ref[0, sl, pl.ds(q_start, T), :]
        if band:
            k = k_ref[0, sl, pl.ds(q_start - T, 2 * T), :]
        else:
            k = k_ref[0, sl, pl.ds(q_start, T), :]
        s = lax.dot_general(q, k, (((1,), (1,)), ((), ())), preferred_element_type=F32) * c_exp2
        s = jnp.where(mask_band if band else mask_diag, s, NEG_INF)
        (sb_sc if band else sd_sc)[blk] = s
        m_sc[g, out_idx, :] = jnp.broadcast_to(jnp.max(s, axis=-1, keepdims=True), (T, LANES))

    def weighted_values(sl, g, q_start, band, blk, out_idx, m_idx):
        v_ref = groups[g][2]
        m = m_sc[min(g, 1), m_idx, :]
        if band:
            p = jnp.exp2(sb_sc[blk] - jnp.concatenate([m, m], axis=-1))
            v = v_ref[0, sl, pl.ds(q_start - T, 2 * T), :]
        else:
            p = jnp.exp2(sd_sc[blk] - m)
            v = v_ref[0, sl, pl.ds(q_start, T), :]
        v1 = jnp.concatenate([v, jnp.ones_like(v)], axis=-1)
        r = jnp.dot(p.astype(BF16), v1, preferred_element_type=F32)
        acc_sc[g, out_idx, :] = r[:, :LANES]
        l_sc[g, out_idx, :] = r[:, LANES:]

    blocks = []
    for i in range(S // T):
        blocks.append((0, i * T, i > 0, max(i - 1, 0), pl.ds(i * T, T), pl.ds(i * T, T)))
    for r in range(d1):
        for n in range(nb1):
            idx = pl.ds(r * len1 + n * T, T)
            blk = 1 + r if n == 0 else n_band0 + r * (nb1 - 1) + n - 1
            blocks.append((1, r * len1 + n * T, n > 0, blk, idx, idx))
    ratio = d2 // d1
    for r in range(d2):
        idx = pl.ds((r % d1) * len1 + r // d1, T, stride=ratio)
        blocks.append((2, r * T, False, 1 + d1 + r, idx, idx))

    def all_blocks(fn):
        for blk in blocks:
            fn(*blk)

    rows = 256

    def natural_rows(c):
        p0 = pl.multiple_of(c * rows, rows)
        return pl.ds(p0, rows), pl.ds((p0 % len1) * d1 + p0 // len1, rows, stride=d1)

    def shared_max(c, carry):
        idx, nat = natural_rows(c)
        m = jnp.maximum(jnp.maximum(m_sc[0, nat, :], m_sc[1, idx, :]), m_sc[2, idx, :])
        m_sc[0, nat, :] = m
        m_sc[1, idx, :] = m
        return carry

    def combine(c, carry):
        idx, nat = natural_rows(c)
        o = ((acc_sc[0, nat, :] + acc_sc[1, idx, :] + acc_sc[2, idx, :])
             / (l_sc[0, nat, :] + l_sc[1, idx, :] + l_sc[2, idx, :]))
        acc_sc[0, nat, :] = o
        return carry

    def finish(sl, c, carry):
        idx = pl.ds(pl.multiple_of(c * rows, rows), rows)
        hz = 0.5 * z_ref[0, sl, idx, :].astype(F32)
        silu = hz + hz * jnp.tanh(hz)
        y_ref[0, sl, idx, :] = (acc_sc[0, idx, :] * silu).astype(BF16)
        return carry

    for sl in range(ATT_SLOTS_PER_STEP):
        all_blocks(functools.partial(scores, sl))
        lax.fori_loop(0, S // rows, shared_max, 0, unroll=True)
        all_blocks(functools.partial(weighted_values, sl))
        lax.fori_loop(0, S // rows, combine, 0, unroll=True)
        lax.fori_loop(0, S // rows, functools.partial(finish, sl), 0)


def _attn(proj):
    B, _, S, _ = proj.shape
    sps = ATT_SLOTS_PER_STEP

    def spec(base):
        return pl.BlockSpec((1, sps, S, LANES), lambda b, s: (b, base // sps + s, 0, 0))

    in_specs = []
    for g in range(N_ATT_GROUPS):
        in_specs += [spec(BLK_Q[g]), spec(BLK_K[g]), spec(BLK_V[g])]
    in_specs.append(spec(BLK_ZA))
    n_diag = sum(ATT_DILATIONS)
    n_band = N_ATT_GROUPS * S // ATT_BLOCK - n_diag
    return pl.pallas_call(
        _attn_kernel,
        grid=(B, ATT_SLOTS // sps),
        in_specs=in_specs,
        out_specs=pl.BlockSpec((1, sps, S, LANES), lambda b, s: (b, s, 0, 0)),
        out_shape=jax.ShapeDtypeStruct((B, ATT_SLOTS, S, LANES), BF16),
        scratch_shapes=[pltpu.VMEM((n_band, ATT_BLOCK, 2 * ATT_BLOCK), F32),
                        pltpu.VMEM((n_diag, ATT_BLOCK, ATT_BLOCK), F32),
                        pltpu.VMEM((N_ATT_GROUPS, S, LANES), F32),
                        pltpu.VMEM((N_ATT_GROUPS, S, LANES), F32),
                        pltpu.VMEM((N_ATT_GROUPS, S, LANES), F32)],
        compiler_params=_params("arbitrary", "arbitrary"),
        name="attn",
    )(*([proj] * 10))


def _ret_decay_tables():
    H, C = RET_HEADS, RET_CHUNK
    log_g = np.log1p(-np.exp2(-5.0 - np.arange(H, dtype=np.float64)))
    idx = np.arange(C, dtype=np.float64)
    diff = idx[:, None] - idx[None, :]
    inner = np.where(diff >= 0, np.exp(log_g[:, None, None] * np.maximum(diff, 0.0)), 0.0)
    q_decay = np.exp(log_g[:, None] * (idx + 1.0))
    k_decay = np.exp(log_g[:, None] * (C - 1.0 - idx))
    chunk_decay = np.exp(log_g * C)
    f = lambda a: jnp.asarray(a, dtype=F32)
    return f(inner), f(q_decay[:, :, None]), f(k_decay[:, :, None]), f(chunk_decay)


def _ret_kernel(cd_ref, q_ref, k_ref, v_ref, z_ref, inner_ref, qd_ref, kd_ref,
                gnw_ref, y_ref, intra_sc, u_sc, st_sc):
    S = y_ref.shape[2]
    C = RET_CHUNK
    n_chunks = S // C
    half = RET_DIM // 2
    per_head = RET_DIM // LANES

    def head(ref, hh, idx):
        return jnp.concatenate([ref[0, hh * per_head + i, idx, :] for i in range(per_head)],
                               axis=-1)

    def state_free(hh):
        for n in range(n_chunks):
            idx = pl.ds(n * C, C)
            qc, kc, vc = head(q_ref, hh, idx), head(k_ref, hh, idx), head(v_ref, hh, idx)
            att = lax.dot_general(qc, kc, (((1,), (1,)), ((), ())), preferred_element_type=F32)
            att = (att * inner_ref[hh]).astype(BF16)
            intra_sc[hh % 2, idx, :] = jnp.dot(att, vc, preferred_element_type=F32)
            if n + 1 < n_chunks:
                kdt = (kc.astype(F32) * kd_ref[hh]).T.astype(BF16)
                u_sc[hh % 2, n] = jnp.dot(kdt, vc, preferred_element_type=F32)

    def recurrence(hh):
        cd = cd_ref[pl.program_id(1) * RET_HEADS_PER_STEP + hh]
        slab = 64
        for r0 in range(0, RET_DIM, slab):
            state = jnp.zeros((slab, RET_DIM), F32)
            for n in range(1, n_chunks):
                state = state * cd + u_sc[hh % 2, n - 1, r0:r0 + slab, :]
                st_sc[hh % 2, n, r0:r0 + slab, :] = state.astype(BF16)

    def finish(hh):
        gnw = gnw_ref[:, hh * RET_DIM:(hh + 1) * RET_DIM]
        for n in range(n_chunks):
            idx = pl.ds(n * C, C)
            out = intra_sc[hh % 2, idx, :]
            if n > 0:
                out = out + jnp.dot(head(q_ref, hh, idx), st_sc[hh % 2, n],
                                    preferred_element_type=F32) * qd_ref[hh]
            mu = jnp.mean(out, axis=-1, keepdims=True)
            cen = out - mu
            var = jnp.mean(cen * cen, axis=-1, keepdims=True)
            o = cen * lax.rsqrt(var + EPS) * gnw
            z = head(z_ref, hh, idx).astype(F32)
            y = (o * (z * jax.nn.sigmoid(z))).astype(BF16)
            y_ref[0, hh * per_head, idx, :] = y[:, :half]
            y_ref[0, hh * per_head + 1, idx, :] = y[:, half:]

    for hh in range(RET_HEADS_PER_STEP):
        state_free(hh)
        if hh > 0:
            finish(hh - 1)
        recurrence(hh)
    finish(RET_HEADS_PER_STEP - 1)


def _ret(proj, gn_w_l):
    B, _, S, _ = proj.shape
    inner, q_decay, k_decay, chunk_decay = _ret_decay_tables()
    blocks = RET_HEADS_PER_STEP * RET_DIM // LANES
    hps = RET_HEADS_PER_STEP

    def spec(base):
        return pl.BlockSpec((1, blocks, S, LANES), lambda b, h: (b, base // blocks + h, 0, 0))

    return pl.pallas_call(
        _ret_kernel,
        grid=(B, RET_HEADS // hps),
        in_specs=[
            pl.BlockSpec(memory_space=pltpu.SMEM),
            spec(BLK_QR), spec(BLK_KR), spec(BLK_VR), spec(BLK_ZR),
            pl.BlockSpec((hps, RET_CHUNK, RET_CHUNK), lambda b, h: (h, 0, 0)),
            pl.BlockSpec((hps, RET_CHUNK, 1), lambda b, h: (h, 0, 0)),
            pl.BlockSpec((hps, RET_CHUNK, 1), lambda b, h: (h, 0, 0)),
            pl.BlockSpec((1, hps * RET_DIM), lambda b, h: (0, h)),
        ],
        out_specs=pl.BlockSpec((1, blocks, S, LANES), lambda b, h: (b, h, 0, 0)),
        out_shape=jax.ShapeDtypeStruct((B, RET_HEADS * RET_DIM // LANES, S, LANES), BF16),
        scratch_shapes=[pltpu.VMEM((2, S, RET_DIM), F32),
                        pltpu.VMEM((2, S // RET_CHUNK, RET_DIM, RET_DIM), F32),
                        pltpu.VMEM((2, S // RET_CHUNK, RET_DIM, RET_DIM), BF16)],
        compiler_params=_params("arbitrary", "arbitrary"),
        name="retention",
    )(chunk_decay, proj, proj, proj, proj, inner, q_decay, k_decay,
      gn_w_l.reshape(1, RET_HEADS * RET_DIM))


def _outproj_kernel(ya_ref, yr_ref, ga_ref, gr_ref, x_ref, gate_ref, wpa_ref, wpr_ref, wo_ref,
                    fnw_ref, o_ref, *, final_norm):
    for r0 in range(0, x_ref.shape[1], ROW_CHUNK):
        ridx = pl.ds(r0, ROW_CHUNK)

        def cat(ref):
            return jnp.concatenate([ref[0, i, ridx, :] for i in range(ref.shape[1])], axis=-1)

        a = jnp.dot(cat(ya_ref), wpa_ref[0], preferred_element_type=F32)
        r = jnp.dot(cat(yr_ref), wpr_ref[0], preferred_element_type=F32)
        merged = (jax.nn.sigmoid(cat(ga_ref).astype(F32)) * a
                  + jax.nn.sigmoid(cat(gr_ref).astype(F32)) * r)
        out = x_ref[0, ridx, :] + gate_ref[0] * jnp.dot(merged.astype(BF16), wo_ref[0],
                                                        preferred_element_type=F32)
        if final_norm:
            ms = jnp.mean(out * out, axis=-1, keepdims=True)
            out = out * lax.rsqrt(ms + EPS) * fnw_ref[...]
        o_ref[0, ridx, :] = out


def _outproj(x, ya, yr, proj, mod_l, wpa, wpr, wo, final_norm_w, layer, final_norm):
    B, S, D = x.shape
    tm = 1024
    n_g = D // LANES
    weight = lambda w: pl.BlockSpec((1,) + w.shape[1:], lambda b, i: (layer, 0, 0))
    return pl.pallas_call(
        functools.partial(_outproj_kernel, final_norm=final_norm),
        grid=(B, S // tm),
        in_specs=[
            pl.BlockSpec((1, ya.shape[1], tm, LANES), lambda b, i: (b, 0, i, 0)),
            pl.BlockSpec((1, yr.shape[1], tm, LANES), lambda b, i: (b, 0, i, 0)),
            pl.BlockSpec((1, n_g, tm, LANES), lambda b, i: (b, BLK_GA // n_g, i, 0)),
            pl.BlockSpec((1, n_g, tm, LANES), lambda b, i: (b, BLK_GR // n_g, i, 0)),
            pl.BlockSpec((1, tm, D), lambda b, i: (b, i, 0)),
            pl.BlockSpec((1, 1, D), lambda b, i: (b, 0, 2)),
            weight(wpa), weight(wpr), weight(wo),
            pl.BlockSpec((1, D), lambda b, i: (0, 0)),
        ],
        out_specs=pl.BlockSpec((1, tm, D), lambda b, i: (b, i, 0)),
        out_shape=jax.ShapeDtypeStruct((B, S, D), F32),
        compiler_params=_params("arbitrary", "arbitrary"),
        name="outproj",
    )(ya, yr, proj, proj, x, mod_l, wpa, wpr, wo, final_norm_w.reshape(1, D))


def kernel(x, c, positions, norm_w, w_ada, b_ada, w_in, ret_gn_w, w_proj_attn, w_proj_ret, w_out,
           final_norm_w):
    B = x.shape[0]
    mod = _ada(c, w_ada, b_ada).reshape(DEPTH, B, 1, 3 * D_MODEL)
    cos, sin = _rope_tables(positions)
    w_in_b = jnp.concatenate([w_in[:, :, s:s + n] for s, n in _column_segments()],
                             axis=-1).astype(BF16)
    wpa_b = w_proj_attn.astype(BF16)
    wpr_b = w_proj_ret.astype(BF16)
    wo_b = w_out.astype(BF16)
    norm_w3 = norm_w.reshape(DEPTH, 1, D_MODEL)
    for l in range(DEPTH):
        proj = _inproj(x, mod[l], norm_w3, w_in_b, cos, sin, layer=l)
        ya = _attn(proj)
        yr = _ret(proj, ret_gn_w[l])
        x = _outproj(x, ya, yr, proj, mod[l], wpa_b, wpr_b, wo_b, final_norm_w, layer=l,
                     final_norm=(l == DEPTH - 1))
    return x
```

```python
import functools

import numpy as np
import jax
import jax.numpy as jnp
from jax import lax
from jax.experimental import pallas as pl
from jax.experimental.pallas import tpu as pltpu

D_MODEL = 1024
SEQ = 2048
DEPTH = 4
ATT_DILATIONS = (1, 4, 16)
N_ATT_GROUPS = 3
ATT_SLOTS = 4
ATT_HEAD_DIM = 128
ATT_BLOCK = 128
ATT_SLOTS_PER_STEP = 2
RET_HEADS = 4
RET_DIM = 256
RET_CHUNK = 512
RET_HEADS_PER_STEP = 4
ROPE_BASE = 10000.0
EPS = 1e-6
NEG_INF = -1e30

LANES = 128
COL_BLOCK = 512
BLOCKS_PER_STEP = COL_BLOCK // LANES
DOTS_PER_STEP = 2
ROW_CHUNK = 512
ATT_QKV = N_ATT_GROUPS * ATT_SLOTS * ATT_HEAD_DIM
IN_WIDTH = 3 * ATT_QKV + ATT_SLOTS * ATT_HEAD_DIM + 4 * RET_HEADS * RET_DIM + 2 * D_MODEL
N_COL_STEPS = IN_WIDTH // COL_BLOCK
N_BLOCKS = IN_WIDTH // LANES


def _column_segments():
    grp = ATT_SLOTS * ATT_HEAD_DIM
    qkv = lambda t, g: (t * ATT_QKV + g * grp, grp)
    seg = [qkv(0, 0), qkv(1, 0), qkv(2, 0), (3 * ATT_QKV, IN_WIDTH - 3 * ATT_QKV)]
    for g in range(1, N_ATT_GROUPS):
        seg += [qkv(0, g), qkv(1, g), qkv(2, g)]
    return seg


_GRP_BLOCKS = ATT_SLOTS * ATT_HEAD_DIM // LANES
_WIDE_BLOCKS = RET_HEADS * RET_DIM // LANES
BLK_ZA = 3 * _GRP_BLOCKS
BLK_QR = BLK_ZA + _GRP_BLOCKS
BLK_KR = BLK_QR + _WIDE_BLOCKS
BLK_VR = BLK_KR + _WIDE_BLOCKS
BLK_ZR = BLK_VR + _WIDE_BLOCKS
BLK_GA = BLK_ZR + _WIDE_BLOCKS
BLK_GR = BLK_GA + D_MODEL // LANES
_BLK_G1 = BLK_GR + D_MODEL // LANES
BLK_Q = (0,) + tuple(_BLK_G1 + (g - 1) * 3 * _GRP_BLOCKS for g in range(1, N_ATT_GROUPS))
BLK_K = tuple(b + _GRP_BLOCKS for b in BLK_Q)
BLK_V = tuple(b + 2 * _GRP_BLOCKS for b in BLK_Q)
N_NATURAL_STEPS = _BLK_G1 // BLOCKS_PER_STEP
STEPS_PER_GROUP = 3 * _GRP_BLOCKS // BLOCKS_PER_STEP

VMEM_LIMIT = 60 * 1024 * 1024
F32 = jnp.float32
BF16 = jnp.bfloat16


def _params(*sem):
    return pltpu.CompilerParams(dimension_semantics=sem, vmem_limit_bytes=VMEM_LIMIT)


def _ada_kernel(c_ref, w_ref, b_ref, o_ref):
    c = c_ref[...]
    c_act = (c * jax.nn.sigmoid(c)).astype(BF16)
    acc = jnp.dot(c_act, w_ref[0].astype(BF16), preferred_element_type=F32)
    o_ref[0] = acc + b_ref[0]


def _ada(c, w_ada, b_ada):
    B = c.shape[0]
    n_col = 3 * D_MODEL // D_MODEL
    return pl.pallas_call(
        _ada_kernel,
        grid=(DEPTH, n_col),
        in_specs=[
            pl.BlockSpec((B, D_MODEL), lambda l, j: (0, 0)),
            pl.BlockSpec((1, D_MODEL, D_MODEL), lambda l, j: (l, 0, j)),
            pl.BlockSpec((1, 1, D_MODEL), lambda l, j: (l, 0, j)),
        ],
        out_specs=pl.BlockSpec((1, B, D_MODEL), lambda l, j: (l, 0, j)),
        out_shape=jax.ShapeDtypeStruct((DEPTH, B, 3 * D_MODEL), F32),
        compiler_params=_params("arbitrary", "arbitrary"),
        name="ada",
    )(c, w_ada, b_ada.reshape(DEPTH, 1, 3 * D_MODEL))


def _rope_kernel(pos_ref, theta_ref, cos_ref, sin_ref):
    ang = pos_ref[0].astype(F32) * theta_ref[...]
    cos_ref[0] = jnp.cos(ang)
    sin_ref[0] = jnp.sin(ang)


def _rope_tables(positions):
    B, S = positions.shape
    half = RET_DIM // 2
    theta = ROPE_BASE ** (-jnp.arange(half, dtype=F32) / half)
    spec = pl.BlockSpec((1, S, half), lambda b: (b, 0, 0))
    return pl.pallas_call(
        _rope_kernel,
        grid=(B,),
        in_specs=[pl.BlockSpec((1, S, 1), lambda b: (b, 0, 0)),
                  pl.BlockSpec((1, half), lambda b: (0, 0))],
        out_specs=[spec, spec],
        out_shape=[jax.ShapeDtypeStruct((B, S, half), F32)] * 2,
        compiler_params=_params("arbitrary"),
        name="rope",
    )(positions.reshape(B, S, 1), theta.reshape(1, half))


def _row_order_of_column_block(cb):
    return 0 if cb < N_NATURAL_STEPS else 1 + (cb - N_NATURAL_STEPS) // STEPS_PER_GROUP


def _epilogue_of_column_block(cb):
    blk = cb * BLOCKS_PER_STEP
    if BLK_QR <= blk < BLK_KR:
        return 1.0
    if BLK_KR <= blk < BLK_VR:
        return RET_DIM ** -0.5
    if BLK_ZA <= blk < BLK_QR or BLK_ZR <= blk < BLK_GA:
        return "silu"
    if BLK_GA <= blk < _BLK_G1:
        return "sigmoid"
    return None


def _inproj_kernel(x_hbm, mod_ref, nw_ref, w_ref, cos_ref, sin_ref, o_ref,
                   h0_ref, h1_ref, h2_ref, hn_ref, hp_ref, x_ref, x_sem):
    b, j = pl.program_id(0), pl.program_id(1)
    S = x_ref.shape[0]
    rows = 128

    def x_copy(seq):
        return pltpu.make_async_copy(x_hbm.at[seq], x_ref, x_sem)

    @pl.when((j == 0) & (b == 0))
    def _():
        x_copy(b).start()

    @pl.when((j == 1) & (b + 1 < pl.num_programs(0)))
    def _():
        x_copy(b + 1).start()

    n_lane_blocks = D_MODEL // LANES
    h_refs = (h0_ref, h1_ref, h2_ref)
    half = RET_DIM // 2

    def project(h_ref, i, epilogue, mc):
        ridx = pl.ds(mc * ROW_CHUNK, ROW_CHUNK)
        res = jnp.dot(h_ref[ridx, :], w_ref[0, :, i * COL_BLOCK:(i + 1) * COL_BLOCK],
                      preferred_element_type=F32)
        blocks = [res[:, c * LANES:(c + 1) * LANES] for c in range(BLOCKS_PER_STEP)]
        if epilogue == "silu":
            halves = [0.5 * blk for blk in blocks]
            blocks = [hz + hz * jnp.tanh(hz) for hz in halves]
        elif epilogue == "sigmoid":
            blocks = [0.5 + 0.5 * jnp.tanh(0.5 * blk) for blk in blocks]
        elif epilogue is not None:
            cos, sin = cos_ref[0, ridx, :], sin_ref[0, ridx, :]
            if epilogue != 1.0:
                cos, sin = cos * epilogue, sin * epilogue
            for c in range(0, BLOCKS_PER_STEP, RET_DIM // LANES):
                t1, t2 = blocks[c], blocks[c + half // LANES]
                blocks[c] = t1 * cos - t2 * sin
                blocks[c + half // LANES] = t2 * cos + t1 * sin
        for c, blk in enumerate(blocks):
            o_ref[0, i * BLOCKS_PER_STEP + c, ridx, :] = blk.astype(BF16)

    def natural_slice(r0, wmul, shift):
        idx = pl.ds(r0, rows)
        xs = x_ref[idx, :]
        ms = jnp.mean(xs * xs, axis=-1, keepdims=True)
        hv = xs * lax.rsqrt(ms + EPS) * wmul + shift
        h0_ref[idx, :] = hv.astype(BF16)
        for cb in range(n_lane_blocks):
            hn_ref[cb, idx, :] = hv[:, cb * LANES:(cb + 1) * LANES]

    def hop_slice(hop, c):
        step = ATT_DILATIONS[hop + 1] // ATT_DILATIONS[hop]
        n_src = S // step
        src_ref = hp_ref if hop else hn_ref
        p0 = pl.multiple_of(c * rows, rows)
        start = (p0 % n_src) * step + p0 // n_src
        pieces = [src_ref[cb, pl.ds(start, rows, stride=step), :] for cb in range(n_lane_blocks)]
        h_refs[hop + 1][pl.ds(p0, rows), :] = jnp.concatenate(pieces, axis=-1).astype(BF16)
        if hop + 2 < len(ATT_DILATIONS):
            for cb in range(n_lane_blocks):
                hp_ref[cb, pl.ds(p0, rows), :] = pieces[cb]

    n_row_chunks = S // ROW_CHUNK
    slices_per_step = n_row_chunks
    hop_steps = S // rows // slices_per_step
    for hop in range(len(ATT_DILATIONS) - 1):
        first_use = (N_NATURAL_STEPS + hop * STEPS_PER_GROUP) // DOTS_PER_STEP
        assert 1 + (hop + 1) * hop_steps <= first_use

    def phase_of_step(s):
        if s == 0:
            return "natural"
        hop = (s - 1) // hop_steps
        return hop if hop + 1 < len(ATT_DILATIONS) else None

    steps_by_kind = {}
    for s in range(N_COL_STEPS // DOTS_PER_STEP):
        cbs = [s * DOTS_PER_STEP + i for i in range(DOTS_PER_STEP)]
        kind = tuple((_row_order_of_column_block(cb), _epilogue_of_column_block(cb))
                     for cb in cbs)
        steps_by_kind.setdefault((kind, phase_of_step(s)), []).append(s)
    for (kind, phase), steps in steps_by_kind.items():
        @pl.when(functools.reduce(jnp.logical_or, [j == s for s in steps]))
        def _(kind=kind, phase=phase):
            if phase == "natural":
                x_copy(b).wait()
                shift = mod_ref[0, :, 0:D_MODEL]
                wmul = nw_ref[0] * (1.0 + mod_ref[0, :, D_MODEL:2 * D_MODEL])
            for mc in range(n_row_chunks):
                if phase == "natural":
                    for r0 in range(mc * ROW_CHUNK, (mc + 1) * ROW_CHUNK, rows):
                        natural_slice(r0, wmul, shift)
                for i, (order, epilogue) in enumerate(kind):
                    project(h_refs[order], i, epilogue, mc)
                if phase not in ("natural", None):
                    first = 1 + phase * hop_steps
                    hop_slice(phase, (j - first) * slices_per_step + mc)


def _inproj(x, mod_l, norm_w, w_in, cos, sin, layer):
    B, S, D = x.shape
    tab = pl.BlockSpec((1, S, LANES), lambda b, j: (b, 0, 0))
    return pl.pallas_call(
        _inproj_kernel,
        grid=(B, N_COL_STEPS // DOTS_PER_STEP),
        in_specs=[
            pl.BlockSpec(memory_space=pl.ANY),
            pl.BlockSpec((1, 1, 3 * D), lambda b, j: (b, 0, 0)),
            pl.BlockSpec((1, 1, D), lambda b, j: (layer, 0, 0)),
            pl.BlockSpec((1, D, DOTS_PER_STEP * COL_BLOCK), lambda b, j: (layer, 0, j)),
            tab, tab,
        ],
        out_specs=pl.BlockSpec((1, DOTS_PER_STEP * BLOCKS_PER_STEP, S, LANES),
                               lambda b, j: (b, j, 0, 0)),
        out_shape=jax.ShapeDtypeStruct((B, N_BLOCKS, S, LANES), BF16),
        scratch_shapes=[pltpu.VMEM((S, D), BF16)] * N_ATT_GROUPS
                       + [pltpu.VMEM((D // LANES, S, LANES), F32)] * 2
                       + [pltpu.VMEM((S, D), F32), pltpu.SemaphoreType.DMA(())],
        compiler_params=_params("arbitrary", "arbitrary"),
        name="inproj",
    )(x, mod_l, norm_w, w_in, cos, sin)


def _attn_kernel(q0, k0, v0, q1, k1, v1, q2, k2, v2, z_ref, y_ref,
                 sb_sc, sd_sc, m_sc, acc_sc, l_sc):
    S = y_ref.shape[2]
    T = ATT_BLOCK
    c_exp2 = ATT_HEAD_DIM ** -0.5 * np.log2(np.e)
    row = lax.broadcasted_iota(jnp.int32, (T, 2 * T), 0)
    col = lax.broadcasted_iota(jnp.int32, (T, 2 * T), 1)
    mask_band = (col >= row) & (col <= row + T)
    mask_diag = (lax.broadcasted_iota(jnp.int32, (T, T), 1)
                 <= lax.broadcasted_iota(jnp.int32, (T, T), 0))
    groups = ((q0, k0, v0), (q1, k1, v1), (q2, k2, v2))
    d1, d2 = ATT_DILATIONS[1], ATT_DILATIONS[2]
    len1 = S // d1
    nb1 = len1 // T
    n_band0 = S // T - 1

    def scores(sl, g, q_start, band, blk, out_idx, m_idx):
        q_ref, k_ref, _ = groups[g]
        q = q_ref[0, sl, pl.ds(q_start, T), :]
        if band:
            k = k_ref[0, sl, pl.ds(q_start - T, 2 * T), :]
        else:
            k = k_ref[0, sl, pl.ds(q_start, T), :]
        s = lax.dot_general(q, k, (((1,), (1,)), ((), ())), preferred_element_type=F32) * c_exp2
        s = jnp.where(mask_band if band else mask_diag, s, NEG_INF)
        (sb_sc if band else sd_sc)[blk] = s
        m_sc[g, out_idx, :] = jnp.broadcast_to(jnp.max(s, axis=-1, keepdims=True), (T, LANES))

    def weighted_values(sl, g, q_start, band, blk, out_idx, m_idx):
        v_ref = groups[g][2]
        m = m_sc[min(g, 1), m_idx, :]
        if band:
            p = jnp.exp2(sb_sc[blk] - jnp.concatenate([m, m], axis=-1))
            v = v_ref[0, sl, pl.ds(q_start - T, 2 * T), :]
        else:
            p = jnp.exp2(sd_sc[blk] - m)
            v = v_ref[0, sl, pl.ds(q_start, T), :]
        v1 = jnp.concatenate([v, jnp.ones_like(v)], axis=-1)
        r = jnp.dot(p.astype(BF16), v1, preferred_element_type=F32)
        acc_sc[g, out_idx, :] = r[:, :LANES]
        l_sc[g, out_idx, :] = r[:, LANES:]

    blocks = []
    for i in range(S // T):
        blocks.append((0, i * T, i > 0, max(i - 1, 0), pl.ds(i * T, T), pl.ds(i * T, T)))
    for r in range(d1):
        for n in range(nb1):
            idx = pl.ds(r * len1 + n * T, T)
            blk = 1 + r if n == 0 else n_band0 + r * (nb1 - 1) + n - 1
            blocks.append((1, r * len1 + n * T, n > 0, blk, idx, idx))
    ratio = d2 // d1
    for r in range(d2):
        idx = pl.ds((r % d1) * len1 + r // d1, T, stride=ratio)
        blocks.append((2, r * T, False, 1 + d1 + r, idx, idx))

    def all_blocks(fn):
        for blk in blocks:
            fn(*blk)

    rows = 256

    def natural_rows(c):
        p0 = pl.multiple_of(c * rows, rows)
        return pl.ds(p0, rows), pl.ds((p0 % len1) * d1 + p0 // len1, rows, stride=d1)

    def shared_max(c, carry):
        idx, nat = natural_rows(c)
        m = jnp.maximum(jnp.maximum(m_sc[0, nat, :], m_sc[1, idx, :]), m_sc[2, idx, :])
        m_sc[0, nat, :] = m
        m_sc[1, idx, :] = m
        return carry

    def combine(c, carry):
        idx, nat = natural_rows(c)
        o = ((acc_sc[0, nat, :] + acc_sc[1, idx, :] + acc_sc[2, idx, :])
             / (l_sc[0, nat, :] + l_sc[1, idx, :] + l_sc[2, idx, :]))
        acc_sc[0, nat, :] = o
        return carry

    def finish(sl, c, carry):
        idx = pl.ds(pl.multiple_of(c * rows, rows), rows)
        gate = z_ref[0, sl, idx, :].astype(F32)
        y_ref[0, sl, idx, :] = (acc_sc[0, idx, :] * gate).astype(BF16)
        return carry

    for sl in range(ATT_SLOTS_PER_STEP):
        all_blocks(functools.partial(scores, sl))
        lax.fori_loop(0, S // rows, shared_max, 0, unroll=True)
        all_blocks(functools.partial(weighted_values, sl))
        lax.fori_loop(0, S // rows, combine, 0, unroll=True)
        lax.fori_loop(0, S // rows, functools.partial(finish, sl), 0)


def _attn(proj):
    B, _, S, _ = proj.shape
    sps = ATT_SLOTS_PER_STEP

    def spec(base):
        return pl.BlockSpec((1, sps, S, LANES), lambda b, s: (b, base // sps + s, 0, 0))

    in_specs = []
    for g in range(N_ATT_GROUPS):
        in_specs += [spec(BLK_Q[g]), spec(BLK_K[g]), spec(BLK_V[g])]
    in_specs.append(spec(BLK_ZA))
    n_diag = sum(ATT_DILATIONS)
    n_band = N_ATT_GROUPS * S // ATT_BLOCK - n_diag
    return pl.pallas_call(
        _attn_kernel,
        grid=(B, ATT_SLOTS // sps),
        in_specs=in_specs,
        out_specs=pl.BlockSpec((1, sps, S, LANES), lambda b, s: (b, s, 0, 0)),
        out_shape=jax.ShapeDtypeStruct((B, ATT_SLOTS, S, LANES), BF16),
        scratch_shapes=[pltpu.VMEM((n_band, ATT_BLOCK, 2 * ATT_BLOCK), F32),
                        pltpu.VMEM((n_diag, ATT_BLOCK, ATT_BLOCK), F32),
                        pltpu.VMEM((N_ATT_GROUPS, S, LANES), F32),
                        pltpu.VMEM((N_ATT_GROUPS, S, LANES), F32),
                        pltpu.VMEM((N_ATT_GROUPS, S, LANES), F32)],
        compiler_params=_params("arbitrary", "arbitrary"),
        name="attn",
    )(*([proj] * 10))


def _ret_decay_tables():
    H, C = RET_HEADS, RET_CHUNK
    log_g = np.log1p(-np.exp2(-5.0 - np.arange(H, dtype=np.float64)))
    idx = np.arange(C, dtype=np.float64)
    diff = idx[:, None] - idx[None, :]
    inner = np.where(diff >= 0, np.exp(log_g[:, None, None] * np.maximum(diff, 0.0)), 0.0)
    q_decay = np.exp(log_g[:, None] * (idx + 1.0))
    k_decay = np.exp(log_g[:, None] * (C - 1.0 - idx))
    chunk_decay = np.exp(log_g * C)
    f = lambda a: jnp.asarray(a, dtype=F32)
    return f(inner), f(q_decay[:, :, None]), f(k_decay[:, :, None]), f(chunk_decay)


def _ret_kernel(cd_ref, q_ref, k_ref, v_ref, z_ref, inner_ref, qd_ref, kd_ref,
                gnw_ref, y_ref, intra_sc, u_sc, st_sc):
    S = y_ref.shape[2]
    C = RET_CHUNK
    n_chunks = S // C
    half = RET_DIM // 2
    per_head = RET_DIM // LANES

    def head(ref, hh, idx):
        return jnp.concatenate([ref[0, hh * per_head + i, idx, :] for i in range(per_head)],
                               axis=-1)

    def state_free(hh):
        for n in range(n_chunks):
            idx = pl.ds(n * C, C)
            qc, kc, vc = head(q_ref, hh, idx), head(k_ref, hh, idx), head(v_ref, hh, idx)
            att = lax.dot_general(qc, kc, (((1,), (1,)), ((), ())), preferred_element_type=F32)
            att = (att * inner_ref[hh]).astype(BF16)
            intra_sc[hh % 2, idx, :] = jnp.dot(att, vc, preferred_element_type=F32)
            if n + 1 < n_chunks:
                kdt = (kc.astype(F32) * kd_ref[hh]).T.astype(BF16)
                u_sc[hh % 2, n] = jnp.dot(kdt, vc, preferred_element_type=F32)

    def recurrence(hh):
        cd = cd_ref[pl.program_id(1) * RET_HEADS_PER_STEP + hh]
        slab = 64
        for r0 in range(0, RET_DIM, slab):
            state = jnp.zeros((slab, RET_DIM), F32)
            for n in range(1, n_chunks):
                state = state * cd + u_sc[hh % 2, n - 1, r0:r0 + slab, :]
                st_sc[hh % 2, n, r0:r0 + slab, :] = state.astype(BF16)

    def finish(hh):
        gnw = gnw_ref[:, hh * RET_DIM:(hh + 1) * RET_DIM]
        for n in range(n_chunks):
            idx = pl.ds(n * C, C)
            out = intra_sc[hh % 2, idx, :]
            if n > 0:
                out = out + jnp.dot(head(q_ref, hh, idx), st_sc[hh % 2, n],
                                    preferred_element_type=F32) * qd_ref[hh]
            mu = jnp.mean(out, axis=-1, keepdims=True)
            cen = out - mu
            var = jnp.mean(cen * cen, axis=-1, keepdims=True)
            o = cen * lax.rsqrt(var + EPS) * gnw
            gate = head(z_ref, hh, idx).astype(F32)
            y = (o * gate).astype(BF16)
            y_ref[0, hh * per_head, idx, :] = y[:, :half]
            y_ref[0, hh * per_head + 1, idx, :] = y[:, half:]

    for hh in range(RET_HEADS_PER_STEP):
        state_free(hh)
        if hh > 0:
            finish(hh - 1)
        recurrence(hh)
    finish(RET_HEADS_PER_STEP - 1)


def _ret(proj, gn_w_l):
    B, _, S, _ = proj.shape
    inner, q_decay, k_decay, chunk_decay = _ret_decay_tables()
    blocks = RET_HEADS_PER_STEP * RET_DIM // LANES
    hps = RET_HEADS_PER_STEP

    def spec(base):
        return pl.BlockSpec((1, blocks, S, LANES), lambda b, h: (b, base // blocks + h, 0, 0))

    return pl.pallas_call(
        _ret_kernel,
        grid=(B, RET_HEADS // hps),
        in_specs=[
            pl.BlockSpec(memory_space=pltpu.SMEM),
            spec(BLK_QR), spec(BLK_KR), spec(BLK_VR), spec(BLK_ZR),
            pl.BlockSpec((hps, RET_CHUNK, RET_CHUNK), lambda b, h: (h, 0, 0)),
            pl.BlockSpec((hps, RET_CHUNK, 1), lambda b, h: (h, 0, 0)),
            pl.BlockSpec((hps, RET_CHUNK, 1), lambda b, h: (h, 0, 0)),
            pl.BlockSpec((1, hps * RET_DIM), lambda b, h: (0, h)),
        ],
        out_specs=pl.BlockSpec((1, blocks, S, LANES), lambda b, h: (b, h, 0, 0)),
        out_shape=jax.ShapeDtypeStruct((B, RET_HEADS * RET_DIM // LANES, S, LANES), BF16),
        scratch_shapes=[pltpu.VMEM((2, S, RET_DIM), F32),
                        pltpu.VMEM((2, S // RET_CHUNK, RET_DIM, RET_DIM), F32),
                        pltpu.VMEM((2, S // RET_CHUNK, RET_DIM, RET_DIM), BF16)],
        compiler_params=_params("arbitrary", "arbitrary"),
        name="retention",
    )(chunk_decay, proj, proj, proj, proj, inner, q_decay, k_decay,
      gn_w_l.reshape(1, RET_HEADS * RET_DIM))


def _outproj_kernel(ya_ref, yr_ref, ga_ref, gr_ref, x_ref, gate_ref, wpa_ref, wpr_ref, wo_ref,
                    fnw_ref, o_ref, *, final_norm):
    for r0 in range(0, x_ref.shape[1], ROW_CHUNK):
        ridx = pl.ds(r0, ROW_CHUNK)

        def cat(ref):
            return jnp.concatenate([ref[0, i, ridx, :] for i in range(ref.shape[1])], axis=-1)

        a = jnp.dot(cat(ya_ref), wpa_ref[0], preferred_element_type=F32)
        r = jnp.dot(cat(yr_ref), wpr_ref[0], preferred_element_type=F32)
        merged = cat(ga_ref).astype(F32) * a + cat(gr_ref).astype(F32) * r
        out = x_ref[0, ridx, :] + gate_ref[0] * jnp.dot(merged.astype(BF16), wo_ref[0],
                                                        preferred_element_type=F32)
        if final_norm:
            ms = jnp.mean(out * out, axis=-1, keepdims=True)
            out = out * lax.rsqrt(ms + EPS) * fnw_ref[...]
        o_ref[0, ridx, :] = out


def _outproj(x, ya, yr, proj, mod_l, wpa, wpr, wo, final_norm_w, layer, final_norm):
    B, S, D = x.shape
    tm = 1024
    n_g = D // LANES
    weight = lambda w: pl.BlockSpec((1,) + w.shape[1:], lambda b, i: (layer, 0, 0))
    return pl.pallas_call(
        functools.partial(_outproj_kernel, final_norm=final_norm),
        grid=(B, S // tm),
        in_specs=[
            pl.BlockSpec((1, ya.shape[1], tm, LANES), lambda b, i: (b, 0, i, 0)),
            pl.BlockSpec((1, yr.shape[1], tm, LANES), lambda b, i: (b, 0, i, 0)),
            pl.BlockSpec((1, n_g, tm, LANES), lambda b, i: (b, BLK_GA // n_g, i, 0)),
            pl.BlockSpec((1, n_g, tm, LANES), lambda b, i: (b, BLK_GR // n_g, i, 0)),
            pl.BlockSpec((1, tm, D), lambda b, i: (b, i, 0)),
            pl.BlockSpec((1, 1, D), lambda b, i: (b, 0, 2)),
            weight(wpa), weight(wpr), weight(wo),
            pl.BlockSpec((1, D), lambda b, i: (0, 0)),
        ],
        out_specs=pl.BlockSpec((1, tm, D), lambda b, i: (b, i, 0)),
        out_shape=jax.ShapeDtypeStruct((B, S, D), F32),
        compiler_params=_params("arbitrary", "arbitrary"),
        name="outproj",
    )(ya, yr, proj, proj, x, mod_l, wpa, wpr, wo, final_norm_w.reshape(1, D))


def kernel(x, c, positions, norm_w, w_ada, b_ada, w_in, ret_gn_w, w_proj_attn, w_proj_ret, w_out,
           final_norm_w):
    B = x.shape[0]
    mod = _ada(c, w_ada, b_ada).reshape(DEPTH, B, 1, 3 * D_MODEL)
    cos, sin = _rope_tables(positions)
    w_in_b = jnp.concatenate([w_in[:, :, s:s + n] for s, n in _column_segments()],
                             axis=-1).astype(BF16)
    wpa_b = w_proj_attn.astype(BF16)
    wpr_b = w_proj_ret.astype(BF16)
    wo_b = w_out.astype(BF16)
    norm_w3 = norm_w.reshape(DEPTH, 1, D_MODEL)
    for l in range(DEPTH):
        proj = _inproj(x, mod[l], norm_w3, w_in_b, cos, sin, layer=l)
        ya = _attn(proj)
        yr = _ret(proj, ret_gn_w[l])
        x = _outproj(x, ya, yr, proj, mod[l], wpa_b, wpr_b, wo_b, final_norm_w, layer=l,
                     final_norm=(l == DEPTH - 1))
    return x
```

```python
import functools

import numpy as np
import jax
import jax.numpy as jnp
from jax import lax
from jax.experimental import pallas as pl
from jax.experimental.pallas import tpu as pltpu

D_MODEL = 1024
SEQ = 2048
DEPTH = 4
ATT_DILATIONS = (1, 4, 16)
N_ATT_GROUPS = 3
ATT_SLOTS = 4
ATT_HEAD_DIM = 128
ATT_BLOCK = 128
ATT_SLOTS_PER_STEP = 2
RET_HEADS = 4
RET_DIM = 256
RET_CHUNK = 512
RET_HEADS_PER_STEP = 4
ROPE_BASE = 10000.0
EPS = 1e-6
NEG_INF = -1e30

LANES = 128
COL_BLOCK = 512
BLOCKS_PER_STEP = COL_BLOCK // LANES
DOTS_PER_STEP = 2
ROW_CHUNK = 512
ATT_QKV = N_ATT_GROUPS * ATT_SLOTS * ATT_HEAD_DIM
IN_WIDTH = 3 * ATT_QKV + ATT_SLOTS * ATT_HEAD_DIM + 4 * RET_HEADS * RET_DIM + 2 * D_MODEL
N_COL_STEPS = IN_WIDTH // COL_BLOCK
N_BLOCKS = IN_WIDTH // LANES


def _column_segments():
    grp = ATT_SLOTS * ATT_HEAD_DIM
    qkv = lambda t, g: (t * ATT_QKV + g * grp, grp)
    seg = [qkv(0, 0), qkv(1, 0), qkv(2, 0), (3 * ATT_QKV, IN_WIDTH - 3 * ATT_QKV)]
    for g in range(1, N_ATT_GROUPS):
        seg += [qkv(0, g), qkv(1, g), qkv(2, g)]
    return seg


_GRP_BLOCKS = ATT_SLOTS * ATT_HEAD_DIM // LANES
_WIDE_BLOCKS = RET_HEADS * RET_DIM // LANES
BLK_ZA = 3 * _GRP_BLOCKS
BLK_QR = BLK_ZA + _GRP_BLOCKS
BLK_KR = BLK_QR + _WIDE_BLOCKS
BLK_VR = BLK_KR + _WIDE_BLOCKS
BLK_ZR = BLK_VR + _WIDE_BLOCKS
BLK_GA = BLK_ZR + _WIDE_BLOCKS
BLK_GR = BLK_GA + D_MODEL // LANES
_BLK_G1 = BLK_GR + D_MODEL // LANES
BLK_Q = (0,) + tuple(_BLK_G1 + (g - 1) * 3 * _GRP_BLOCKS for g in range(1, N_ATT_GROUPS))
BLK_K = tuple(b + _GRP_BLOCKS for b in BLK_Q)
BLK_V = tuple(b + 2 * _GRP_BLOCKS for b in BLK_Q)
N_NATURAL_STEPS = _BLK_G1 // BLOCKS_PER_STEP
STEPS_PER_GROUP = 3 * _GRP_BLOCKS // BLOCKS_PER_STEP

VMEM_LIMIT = 60 * 1024 * 1024
F32 = jnp.float32
BF16 = jnp.bfloat16


def _params(*sem):
    return pltpu.CompilerParams(dimension_semantics=sem, vmem_limit_bytes=VMEM_LIMIT)


def _ada_kernel(c_ref, w_ref, b_ref, o_ref):
    c = c_ref[...]
    c_act = (c * jax.nn.sigmoid(c)).astype(BF16)
    acc = jnp.dot(c_act, w_ref[0].astype(BF16), preferred_element_type=F32)
    o_ref[0] = acc + b_ref[0]


def _ada(c, w_ada, b_ada):
    B = c.shape[0]
    n_col = 3 * D_MODEL // D_MODEL
    return pl.pallas_call(
        _ada_kernel,
        grid=(DEPTH, n_col),
        in_specs=[
            pl.BlockSpec((B, D_MODEL), lambda l, j: (0, 0)),
            pl.BlockSpec((1, D_MODEL, D_MODEL), lambda l, j: (l, 0, j)),
            pl.BlockSpec((1, 1, D_MODEL), lambda l, j: (l, 0, j)),
        ],
        out_specs=pl.BlockSpec((1, B, D_MODEL), lambda l, j: (l, 0, j)),
        out_shape=jax.ShapeDtypeStruct((DEPTH, B, 3 * D_MODEL), F32),
        compiler_params=_params("arbitrary", "arbitrary"),
        name="ada",
    )(c, w_ada, b_ada.reshape(DEPTH, 1, 3 * D_MODEL))


def _rope_kernel(pos_ref, theta_ref, cos_ref, sin_ref):
    ang = pos_ref[0].astype(F32) * theta_ref[...]
    cos_ref[0] = jnp.cos(ang)
    sin_ref[0] = jnp.sin(ang)


def _rope_tables(positions):
    B, S = positions.shape
    half = RET_DIM // 2
    theta = ROPE_BASE ** (-jnp.arange(half, dtype=F32) / half)
    spec = pl.BlockSpec((1, S, half), lambda b: (b, 0, 0))
    return pl.pallas_call(
        _rope_kernel,
        grid=(B,),
        in_specs=[pl.BlockSpec((1, S, 1), lambda b: (b, 0, 0)),
                  pl.BlockSpec((1, half), lambda b: (0, 0))],
        out_specs=[spec, spec],
        out_shape=[jax.ShapeDtypeStruct((B, S, half), F32)] * 2,
        compiler_params=_params("arbitrary"),
        name="rope",
    )(positions.reshape(B, S, 1), theta.reshape(1, half))


def _row_order_of_column_block(cb):
    return 0 if cb < N_NATURAL_STEPS else 1 + (cb - N_NATURAL_STEPS) // STEPS_PER_GROUP


def _epilogue_of_column_block(cb):
    blk = cb * BLOCKS_PER_STEP
    if BLK_QR <= blk < BLK_KR:
        return 1.0
    if BLK_KR <= blk < BLK_VR:
        return RET_DIM ** -0.5
    if BLK_ZA <= blk < BLK_QR or BLK_ZR <= blk < BLK_GA:
        return "silu"
    return None


def _inproj_kernel(x_hbm, mod_ref, nw_ref, w_ref, cos_ref, sin_ref, o_ref,
                   h0_ref, h1_ref, h2_ref, hn_ref, hp_ref, x_ref, x_sem):
    b, j = pl.program_id(0), pl.program_id(1)
    S = x_ref.shape[0]
    rows = 128

    def x_copy(seq):
        return pltpu.make_async_copy(x_hbm.at[seq], x_ref, x_sem)

    @pl.when((j == 0) & (b == 0))
    def _():
        x_copy(b).start()

    @pl.when((j == 1) & (b + 1 < pl.num_programs(0)))
    def _():
        x_copy(b + 1).start()

    n_lane_blocks = D_MODEL // LANES
    h_refs = (h0_ref, h1_ref, h2_ref)
    half = RET_DIM // 2

    def project(h_ref, i, epilogue, mc):
        ridx = pl.ds(mc * ROW_CHUNK, ROW_CHUNK)
        res = jnp.dot(h_ref[ridx, :], w_ref[0, :, i * COL_BLOCK:(i + 1) * COL_BLOCK],
                      preferred_element_type=F32)
        blocks = [res[:, c * LANES:(c + 1) * LANES] for c in range(BLOCKS_PER_STEP)]
        if epilogue == "silu":
            halves = [0.5 * blk for blk in blocks]
            blocks = [hz + hz * jnp.tanh(hz) for hz in halves]
        elif epilogue is not None:
            cos, sin = cos_ref[0, ridx, :], sin_ref[0, ridx, :]
            if epilogue != 1.0:
                cos, sin = cos * epilogue, sin * epilogue
            for c in range(0, BLOCKS_PER_STEP, RET_DIM // LANES):
                t1, t2 = blocks[c], blocks[c + half // LANES]
                blocks[c] = t1 * cos - t2 * sin
                blocks[c + half // LANES] = t2 * cos + t1 * sin
        for c, blk in enumerate(blocks):
            o_ref[0, i * BLOCKS_PER_STEP + c, ridx, :] = blk.astype(BF16)

    def natural_slice(r0, wmul, shift):
        idx = pl.ds(r0, rows)
        xs = x_ref[idx, :]
        ms = jnp.mean(xs * xs, axis=-1, keepdims=True)
        hv = xs * lax.rsqrt(ms + EPS) * wmul + shift
        h0_ref[idx, :] = hv.astype(BF16)
        for cb in range(n_lane_blocks):
            hn_ref[cb, idx, :] = hv[:, cb * LANES:(cb + 1) * LANES]

    def hop_slice(hop, c):
        step = ATT_DILATIONS[hop + 1] // ATT_DILATIONS[hop]
        n_src = S // step
        src_ref = hp_ref if hop else hn_ref
        p0 = pl.multiple_of(c * rows, rows)
        start = (p0 % n_src) * step + p0 // n_src
        pieces = [src_ref[cb, pl.ds(start, rows, stride=step), :] for cb in range(n_lane_blocks)]
        h_refs[hop + 1][pl.ds(p0, rows), :] = jnp.concatenate(pieces, axis=-1).astype(BF16)
        if hop + 2 < len(ATT_DILATIONS):
            for cb in range(n_lane_blocks):
                hp_ref[cb, pl.ds(p0, rows), :] = pieces[cb]

    n_row_chunks = S // ROW_CHUNK
    slices_per_step = n_row_chunks
    hop_steps = S // rows // slices_per_step
    for hop in range(len(ATT_DILATIONS) - 1):
        first_use = (N_NATURAL_STEPS + hop * STEPS_PER_GROUP) // DOTS_PER_STEP
        assert 1 + (hop + 1) * hop_steps <= first_use

    def phase_of_step(s):
        if s == 0:
            return "natural"
        hop = (s - 1) // hop_steps
        return hop if hop + 1 < len(ATT_DILATIONS) else None

    steps_by_kind = {}
    for s in range(N_COL_STEPS // DOTS_PER_STEP):
        cbs = [s * DOTS_PER_STEP + i for i in range(DOTS_PER_STEP)]
        kind = tuple((_row_order_of_column_block(cb), _epilogue_of_column_block(cb))
                     for cb in cbs)
        steps_by_kind.setdefault((kind, phase_of_step(s)), []).append(s)
    for (kind, phase), steps in steps_by_kind.items():
        @pl.when(functools.reduce(jnp.logical_or, [j == s for s in steps]))
        def _(kind=kind, phase=phase):
            if phase == "natural":
                x_copy(b).wait()
                shift = mod_ref[0, :, 0:D_MODEL]
                wmul = nw_ref[0] * (1.0 + mod_ref[0, :, D_MODEL:2 * D_MODEL])
            for mc in range(n_row_chunks):
                if phase == "natural":
                    for r0 in range(mc * ROW_CHUNK, (mc + 1) * ROW_CHUNK, rows):
                        natural_slice(r0, wmul, shift)
                for i, (order, epilogue) in enumerate(kind):
                    project(h_refs[order], i, epilogue, mc)
                if phase not in ("natural", None):
                    first = 1 + phase * hop_steps
                    hop_slice(phase, (j - first) * slices_per_step + mc)


def _inproj(x, mod_l, norm_w, w_in, cos, sin, layer):
    B, S, D = x.shape
    tab = pl.BlockSpec((1, S, LANES), lambda b, j: (b, 0, 0))
    return pl.pallas_call(
        _inproj_kernel,
        grid=(B, N_COL_STEPS // DOTS_PER_STEP),
        in_specs=[
            pl.BlockSpec(memory_space=pl.ANY),
            pl.BlockSpec((1, 1, 3 * D), lambda b, j: (b, 0, 0)),
            pl.BlockSpec((1, 1, D), lambda b, j: (layer, 0, 0)),
            pl.BlockSpec((1, D, DOTS_PER_STEP * COL_BLOCK), lambda b, j: (layer, 0, j)),
            tab, tab,
        ],
        out_specs=pl.BlockSpec((1, DOTS_PER_STEP * BLOCKS_PER_STEP, S, LANES),
                               lambda b, j: (b, j, 0, 0)),
        out_shape=jax.ShapeDtypeStruct((B, N_BLOCKS, S, LANES), BF16),
        scratch_shapes=[pltpu.VMEM((S, D), BF16)] * N_ATT_GROUPS
                       + [pltpu.VMEM((D // LANES, S, LANES), F32)] * 2
                       + [pltpu.VMEM((S, D), F32), pltpu.SemaphoreType.DMA(())],
        compiler_params=_params("arbitrary", "arbitrary"),
        name="inproj",
    )(x, mod_l, norm_w, w_in, cos, sin)


def _attn_kernel(q0, k0, v0, q1, k1, v1, q2, k2, v2, z_ref, y_ref,
                 sb_sc, sd_sc, m_sc, acc_sc, l_sc):
    S = y_ref.shape[2]
    T = ATT_BLOCK
    c_exp2 = ATT_HEAD_DIM ** -0.5 * np.log2(np.e)
    row = lax.broadcasted_iota(jnp.int32, (T, 2 * T), 0)
    col = lax.broadcasted_iota(jnp.int32, (T, 2 * T), 1)
    mask_band = (col >= row) & (col <= row + T)
    mask_diag = (lax.broadcasted_iota(jnp.int32, (T, T), 1)
                 <= lax.broadcasted_iota(jnp.int32, (T, T), 0))
    groups = ((q0, k0, v0), (q1, k1, v1), (q2, k2, v2))
    d1, d2 = ATT_DILATIONS[1], ATT_DILATIONS[2]
    len1 = S // d1
    nb1 = len1 // T
    n_band0 = S // T - 1

    def scores(sl, g, q_start, band, blk, out_idx, m_idx):
        q_ref, k_ref, _ = groups[g]
        q = q_ref[0, sl, pl.ds(q_start, T), :]
        if band:
            k = k_ref[0, sl, pl.ds(q_start - T, 2 * T), :]
        else:
            k = k_ref[0, sl, pl.ds(q_start, T), :]
        s = lax.dot_general(q, k, (((1,), (1,)), ((), ())), preferred_element_type=F32) * c_exp2
        s = jnp.where(mask_band if band else mask_diag, s, NEG_INF)
        (sb_sc if band else sd_sc)[blk] = s
        m_sc[g, out_idx, :] = jnp.broadcast_to(jnp.max(s, axis=-1, keepdims=True), (T, LANES))

    def weighted_values(sl, g, q_start, band, blk, out_idx, m_idx):
        v_ref = groups[g][2]
        m = m_sc[min(g, 1), m_idx, :]
        if band:
            p = jnp.exp2(sb_sc[blk] - jnp.concatenate([m, m], axis=-1))
            v = v_ref[0, sl, pl.ds(q_start - T, 2 * T), :]
        else:
            p = jnp.exp2(sd_sc[blk] - m)
            v = v_ref[0, sl, pl.ds(q_start, T), :]
        v1 = jnp.concatenate([v, jnp.ones_like(v)], axis=-1)
        r = jnp.dot(p.astype(BF16), v1, preferred_element_type=F32)
        acc_sc[g, out_idx, :] = r[:, :LANES]
        l_sc[g, out_idx, :] = r[:, LANES:]

    blocks = []
    for i in range(S // T):
        blocks.append((0, i * T, i > 0, max(i - 1, 0), pl.ds(i * T, T), pl.ds(i * T, T)))
    for r in range(d1):
        for n in range(nb1):
            idx = pl.ds(r * len1 + n * T, T)
            blk = 1 + r if n == 0 else n_band0 + r * (nb1 - 1) + n - 1
            blocks.append((1, r * len1 + n * T, n > 0, blk, idx, idx))
    ratio = d2 // d1
    for r in range(d2):
        idx = pl.ds((r % d1) * len1 + r // d1, T, stride=ratio)
        blocks.append((2, r * T, False, 1 + d1 + r, idx, idx))

    def all_blocks(fn):
        for blk in blocks:
            fn(*blk)

    rows = 256

    def natural_rows(c):
        p0 = pl.multiple_of(c * rows, rows)
        return pl.ds(p0, rows), pl.ds((p0 % len1) * d1 + p0 // len1, rows, stride=d1)

    def shared_max(c, carry):
        idx, nat = natural_rows(c)
        m = jnp.maximum(jnp.maximum(m_sc[0, nat, :], m_sc[1, idx, :]), m_sc[2, idx, :])
        m_sc[0, nat, :] = m
        m_sc[1, idx, :] = m
        return carry

    def combine(c, carry):
        idx, nat = natural_rows(c)
        o = ((acc_sc[0, nat, :] + acc_sc[1, idx, :] + acc_sc[2, idx, :])
             / (l_sc[0, nat, :] + l_sc[1, idx, :] + l_sc[2, idx, :]))
        acc_sc[0, nat, :] = o
        return carry

    def finish(sl, c, carry):
        idx = pl.ds(pl.multiple_of(c * rows, rows), rows)
        gate = z_ref[0, sl, idx, :].astype(F32)
        y_ref[0, sl, idx, :] = (acc_sc[0, idx, :] * gate).astype(BF16)
        return carry

    for sl in range(ATT_SLOTS_PER_STEP):
        all_blocks(functools.partial(scores, sl))
        lax.fori_loop(0, S // rows, shared_max, 0, unroll=True)
        all_blocks(functools.partial(weighted_values, sl))
        lax.fori_loop(0, S // rows, combine, 0, unroll=True)
        lax.fori_loop(0, S // rows, functools.partial(finish, sl), 0)


def _attn(proj):
    B, _, S, _ = proj.shape
    sps = ATT_SLOTS_PER_STEP

    def spec(base):
        return pl.BlockSpec((1, sps, S, LANES), lambda b, s: (b, base // sps + s, 0, 0))

    in_specs = []
    for g in range(N_ATT_GROUPS):
        in_specs += [spec(BLK_Q[g]), spec(BLK_K[g]), spec(BLK_V[g])]
    in_specs.append(spec(BLK_ZA))
    n_diag = sum(ATT_DILATIONS)
    n_band = N_ATT_GROUPS * S // ATT_BLOCK - n_diag
    return pl.pallas_call(
        _attn_kernel,
        grid=(B, ATT_SLOTS // sps),
        in_specs=in_specs,
        out_specs=pl.BlockSpec((1, sps, S, LANES), lambda b, s: (b, s, 0, 0)),
        out_shape=jax.ShapeDtypeStruct((B, ATT_SLOTS, S, LANES), BF16),
        scratch_shapes=[pltpu.VMEM((n_band, ATT_BLOCK, 2 * ATT_BLOCK), F32),
                        pltpu.VMEM((n_diag, ATT_BLOCK, ATT_BLOCK), F32),
                        pltpu.VMEM((N_ATT_GROUPS, S, LANES), F32),
                        pltpu.VMEM((N_ATT_GROUPS, S, LANES), F32),
                        pltpu.VMEM((N_ATT_GROUPS, S, LANES), F32)],
        compiler_params=_params("arbitrary", "arbitrary"),
        name="attn",
    )(*([proj] * 10))


def _ret_decay_tables():
    H, C = RET_HEADS, RET_CHUNK
    log_g = np.log1p(-np.exp2(-5.0 - np.arange(H, dtype=np.float64)))
    idx = np.arange(C, dtype=np.float64)
    diff = idx[:, None] - idx[None, :]
    inner = np.where(diff >= 0, np.exp(log_g[:, None, None] * np.maximum(diff, 0.0)), 0.0)
    q_decay = np.exp(log_g[:, None] * (idx + 1.0))
    k_decay = np.exp(log_g[:, None] * (C - 1.0 - idx))
    chunk_decay = np.exp(log_g * C)
    f = lambda a: jnp.asarray(a, dtype=F32)
    return f(inner), f(q_decay[:, :, None]), f(k_decay[:, :, None]), f(chunk_decay)


def _ret_kernel(cd_ref, q_ref, k_ref, v_ref, z_ref, inner_ref, qd_ref, kd_ref,
                gnw_ref, y_ref, intra_sc, u_sc, st_sc):
    S = y_ref.shape[2]
    C = RET_CHUNK
    n_chunks = S // C
    half = RET_DIM // 2
    per_head = RET_DIM // LANES

    def head(ref, hh, idx):
        return jnp.concatenate([ref[0, hh * per_head + i, idx, :] for i in range(per_head)],
                               axis=-1)

    def state_free(hh):
        for n in range(n_chunks):
            idx = pl.ds(n * C, C)
            qc, kc, vc = head(q_ref, hh, idx), head(k_ref, hh, idx), head(v_ref, hh, idx)
            att = lax.dot_general(qc, kc, (((1,), (1,)), ((), ())), preferred_element_type=F32)
            att = (att * inner_ref[hh]).astype(BF16)
            intra_sc[hh % 2, idx, :] = jnp.dot(att, vc, preferred_element_type=F32)
            if n + 1 < n_chunks:
                kdt = (kc.astype(F32) * kd_ref[hh]).T.astype(BF16)
                u_sc[hh % 2, n] = jnp.dot(kdt, vc, preferred_element_type=F32)

    def recurrence(hh):
        cd = cd_ref[pl.program_id(1) * RET_HEADS_PER_STEP + hh]
        slab = 64
        for r0 in range(0, RET_DIM, slab):
            state = jnp.zeros((slab, RET_DIM), F32)
            for n in range(1, n_chunks):
                state = state * cd + u_sc[hh % 2, n - 1, r0:r0 + slab, :]
                st_sc[hh % 2, n, r0:r0 + slab, :] = state.astype(BF16)

    def finish(hh):
        gnw = gnw_ref[:, hh * RET_DIM:(hh + 1) * RET_DIM]
        for n in range(n_chunks):
            idx = pl.ds(n * C, C)
            out = intra_sc[hh % 2, idx, :]
            if n > 0:
                out = out + jnp.dot(head(q_ref, hh, idx), st_sc[hh % 2, n],
                                    preferred_element_type=F32) * qd_ref[hh]
            mu = jnp.mean(out, axis=-1, keepdims=True)
            cen = out - mu
            var = jnp.mean(cen * cen, axis=-1, keepdims=True)
            o = cen * lax.rsqrt(var + EPS) * gnw
            gate = head(z_ref, hh, idx).astype(F32)
            y = (o * gate).astype(BF16)
            y_ref[0, hh * per_head, idx, :] = y[:, :half]
            y_ref[0, hh * per_head + 1, idx, :] = y[:, half:]

    for hh in range(RET_HEADS_PER_STEP):
        state_free(hh)
        if hh > 0:
            finish(hh - 1)
        recurrence(hh)
    finish(RET_HEADS_PER_STEP - 1)


def _ret(proj, gn_w_l):
    B, _, S, _ = proj.shape
    inner, q_decay, k_decay, chunk_decay = _ret_decay_tables()
    blocks = RET_HEADS_PER_STEP * RET_DIM // LANES
    hps = RET_HEADS_PER_STEP

    def spec(base):
        return pl.BlockSpec((1, blocks, S, LANES), lambda b, h: (b, base // blocks + h, 0, 0))

    return pl.pallas_call(
        _ret_kernel,
        grid=(B, RET_HEADS // hps),
        in_specs=[
            pl.BlockSpec(memory_space=pltpu.SMEM),
            spec(BLK_QR), spec(BLK_KR), spec(BLK_VR), spec(BLK_ZR),
            pl.BlockSpec((hps, RET_CHUNK, RET_CHUNK), lambda b, h: (h, 0, 0)),
            pl.BlockSpec((hps, RET_CHUNK, 1), lambda b, h: (h, 0, 0)),
            pl.BlockSpec((hps, RET_CHUNK, 1), lambda b, h: (h, 0, 0)),
            pl.BlockSpec((1, hps * RET_DIM), lambda b, h: (0, h)),
        ],
        out_specs=pl.BlockSpec((1, blocks, S, LANES), lambda b, h: (b, h, 0, 0)),
        out_shape=jax.ShapeDtypeStruct((B, RET_HEADS * RET_DIM // LANES, S, LANES), BF16),
        scratch_shapes=[pltpu.VMEM((2, S, RET_DIM), F32),
                        pltpu.VMEM((2, S // RET_CHUNK, RET_DIM, RET_DIM), F32),
                        pltpu.VMEM((2, S // RET_CHUNK, RET_DIM, RET_DIM), BF16)],
        compiler_params=_params("arbitrary", "arbitrary"),
        name="retention",
    )(chunk_decay, proj, proj, proj, proj, inner, q_decay, k_decay,
      gn_w_l.reshape(1, RET_HEADS * RET_DIM))


def _outproj_kernel(ya_ref, yr_ref, ga_ref, gr_ref, x_ref, gate_ref, wpa_ref, wpr_ref, wo_ref,
                    fnw_ref, o_ref, *, final_norm):
    for r0 in range(0, x_ref.shape[1], ROW_CHUNK):
        ridx = pl.ds(r0, ROW_CHUNK)

        def cat(ref):
            return jnp.concatenate([ref[0, i, ridx, :] for i in range(ref.shape[1])], axis=-1)

        a = jnp.dot(cat(ya_ref), wpa_ref[0], preferred_element_type=F32)
        r = jnp.dot(cat(yr_ref), wpr_ref[0], preferred_element_type=F32)
        merged = (jax.nn.sigmoid(cat(ga_ref).astype(F32)) * a
                  + jax.nn.sigmoid(cat(gr_ref).astype(F32)) * r)
        out = x_ref[0, ridx, :] + gate_ref[0] * jnp.dot(merged.astype(BF16), wo_ref[0],
                                                        preferred_element_type=F32)
        if final_norm:
            ms = jnp.mean(out * out, axis=-1, keepdims=True)
            out = out * lax.rsqrt(ms + EPS) * fnw_ref[...]
        o_ref[0, ridx, :] = out


def _outproj(x, ya, yr, proj, mod_l, wpa, wpr, wo, final_norm_w, layer, final_norm):
    B, S, D = x.shape
    tm = 1024
    n_g = D // LANES
    weight = lambda w: pl.BlockSpec((1,) + w.shape[1:], lambda b, i: (layer, 0, 0))
    return pl.pallas_call(
        functools.partial(_outproj_kernel, final_norm=final_norm),
        grid=(B, S // tm),
        in_specs=[
            pl.BlockSpec((1, ya.shape[1], tm, LANES), lambda b, i: (b, 0, i, 0)),
            pl.BlockSpec((1, yr.shape[1], tm, LANES), lambda b, i: (b, 0, i, 0)),
            pl.BlockSpec((1, n_g, tm, LANES), lambda b, i: (b, BLK_GA // n_g, i, 0)),
            pl.BlockSpec((1, n_g, tm, LANES), lambda b, i: (b, BLK_GR // n_g, i, 0)),
            pl.BlockSpec((1, tm, D), lambda b, i: (b, i, 0)),
            pl.BlockSpec((1, 1, D), lambda b, i: (b, 0, 2)),
            weight(wpa), weight(wpr), weight(wo),
            pl.BlockSpec((1, D), lambda b, i: (0, 0)),
        ],
        out_specs=pl.BlockSpec((1, tm, D), lambda b, i: (b, i, 0)),
        out_shape=jax.ShapeDtypeStruct((B, S, D), F32),
        compiler_params=_params("arbitrary", "arbitrary"),
        name="outproj",
    )(ya, yr, proj, proj, x, mod_l, wpa, wpr, wo, final_norm_w.reshape(1, D))


def kernel(x, c, positions, norm_w, w_ada, b_ada, w_in, ret_gn_w, w_proj_attn, w_proj_ret, w_out,
           final_norm_w):
    B = x.shape[0]
    mod = _ada(c, w_ada, b_ada).reshape(DEPTH, B, 1, 3 * D_MODEL)
    cos, sin = _rope_tables(positions)
    w_in_b = jnp.concatenate([w_in[:, :, s:s + n] for s, n in _column_segments()],
                             axis=-1).astype(BF16)
    wpa_b = w_proj_attn.astype(BF16)
    wpr_b = w_proj_ret.astype(BF16)
    wo_b = w_out.astype(BF16)
    norm_w3 = norm_w.reshape(DEPTH, 1, D_MODEL)
    for l in range(DEPTH):
        proj = _inproj(x, mod[l], norm_w3, w_in_b, cos, sin, layer=l)
        ya = _attn(proj)
        yr = _ret(proj, ret_gn_w[l])
        x = _outproj(x, ya, yr, proj, mod[l], wpa_b, wpr_b, wo_b, final_norm_w, layer=l,
                     final_norm=(l == DEPTH - 1))
    return x
```

```python
import functools

import numpy as np
import jax
import jax.numpy as jnp
from jax import lax
from jax.experimental import pallas as pl
from jax.experimental.pallas import tpu as pltpu

D_MODEL = 1024
SEQ = 2048
DEPTH = 4
ATT_DILATIONS = (1, 4, 16)
N_ATT_GROUPS = 3
ATT_SLOTS = 4
ATT_HEAD_DIM = 128
ATT_BLOCK = 128
ATT_SLOTS_PER_STEP = 2
RET_HEADS = 4
RET_DIM = 256
RET_CHUNK = 512
RET_HEADS_PER_STEP = 4
ROPE_BASE = 10000.0
EPS = 1e-6
NEG_INF = -1e30

LANES = 128
COL_BLOCK = 512
BLOCKS_PER_STEP = COL_BLOCK // LANES
DOTS_PER_STEP = 2
ROW_CHUNK = 512
ATT_QKV = N_ATT_GROUPS * ATT_SLOTS * ATT_HEAD_DIM
IN_WIDTH = 3 * ATT_QKV + ATT_SLOTS * ATT_HEAD_DIM + 4 * RET_HEADS * RET_DIM + 2 * D_MODEL
N_COL_STEPS = IN_WIDTH // COL_BLOCK
N_BLOCKS = IN_WIDTH // LANES


def _column_segments():
    grp = ATT_SLOTS * ATT_HEAD_DIM
    qkv = lambda t, g: (t * ATT_QKV + g * grp, grp)
    seg = [qkv(0, 0), qkv(1, 0), qkv(2, 0), (3 * ATT_QKV, IN_WIDTH - 3 * ATT_QKV)]
    for g in range(1, N_ATT_GROUPS):
        seg += [qkv(0, g), qkv(1, g), qkv(2, g)]
    return seg


_GRP_BLOCKS = ATT_SLOTS * ATT_HEAD_DIM // LANES
_WIDE_BLOCKS = RET_HEADS * RET_DIM // LANES
BLK_ZA = 3 * _GRP_BLOCKS
BLK_QR = BLK_ZA + _GRP_BLOCKS
BLK_KR = BLK_QR + _WIDE_BLOCKS
BLK_VR = BLK_KR + _WIDE_BLOCKS
BLK_ZR = BLK_VR + _WIDE_BLOCKS
BLK_GA = BLK_ZR + _WIDE_BLOCKS
BLK_GR = BLK_GA + D_MODEL // LANES
_BLK_G1 = BLK_GR + D_MODEL // LANES
BLK_Q = (0,) + tuple(_BLK_G1 + (g - 1) * 3 * _GRP_BLOCKS for g in range(1, N_ATT_GROUPS))
BLK_K = tuple(b + _GRP_BLOCKS for b in BLK_Q)
BLK_V = tuple(b + 2 * _GRP_BLOCKS for b in BLK_Q)
N_NATURAL_STEPS = _BLK_G1 // BLOCKS_PER_STEP
STEPS_PER_GROUP = 3 * _GRP_BLOCKS // BLOCKS_PER_STEP

VMEM_LIMIT = 60 * 1024 * 1024
F32 = jnp.float32
BF16 = jnp.bfloat16


def _params(*sem):
    return pltpu.CompilerParams(dimension_semantics=sem, vmem_limit_bytes=VMEM_LIMIT)


def _ada_kernel(c_ref, w_ref, b_ref, o_ref):
    c = c_ref[...]
    c_act = (c * jax.nn.sigmoid(c)).astype(BF16)
    acc = jnp.dot(c_act, w_ref[0].astype(BF16), preferred_element_type=F32)
    o_ref[0] = acc + b_ref[0]


def _ada(c, w_ada, b_ada):
    B = c.shape[0]
    n_col = 3 * D_MODEL // D_MODEL
    return pl.pallas_call(
        _ada_kernel,
        grid=(DEPTH, n_col),
        in_specs=[
            pl.BlockSpec((B, D_MODEL), lambda l, j: (0, 0)),
            pl.BlockSpec((1, D_MODEL, D_MODEL), lambda l, j: (l, 0, j)),
            pl.BlockSpec((1, 1, D_MODEL), lambda l, j: (l, 0, j)),
        ],
        out_specs=pl.BlockSpec((1, B, D_MODEL), lambda l, j: (l, 0, j)),
        out_shape=jax.ShapeDtypeStruct((DEPTH, B, 3 * D_MODEL), F32),
        compiler_params=_params("arbitrary", "arbitrary"),
        name="ada",
    )(c, w_ada, b_ada.reshape(DEPTH, 1, 3 * D_MODEL))


def _rope_kernel(pos_ref, theta_ref, cos_ref, sin_ref):
    ang = pos_ref[0].astype(F32) * theta_ref[...]
    cos_ref[0] = jnp.cos(ang)
    sin_ref[0] = jnp.sin(ang)


def _rope_tables(positions):
    B, S = positions.shape
    half = RET_DIM // 2
    theta = ROPE_BASE ** (-jnp.arange(half, dtype=F32) / half)
    spec = pl.BlockSpec((1, S, half), lambda b: (b, 0, 0))
    return pl.pallas_call(
        _rope_kernel,
        grid=(B,),
        in_specs=[pl.BlockSpec((1, S, 1), lambda b: (b, 0, 0)),
                  pl.BlockSpec((1, half), lambda b: (0, 0))],
        out_specs=[spec, spec],
        out_shape=[jax.ShapeDtypeStruct((B, S, half), F32)] * 2,
        compiler_params=_params("arbitrary"),
        name="rope",
    )(positions.reshape(B, S, 1), theta.reshape(1, half))


def _row_order_of_column_block(cb):
    return 0 if cb < N_NATURAL_STEPS else 1 + (cb - N_NATURAL_STEPS) // STEPS_PER_GROUP


def _epilogue_of_column_block(cb):
    blk = cb * BLOCKS_PER_STEP
    if BLK_QR <= blk < BLK_KR:
        return 1.0
    if BLK_KR <= blk < BLK_VR:
        return RET_DIM ** -0.5
    if BLK_ZA <= blk < BLK_QR or BLK_ZR <= blk < BLK_GA:
        return "silu"
    return None


def _inproj_kernel(x_hbm, mod_ref, nw_ref, w_ref, cos_ref, sin_ref, o_ref,
                   h0_ref, h1_ref, h2_ref, hn_ref, hp_ref, x_ref, x_sem, raw_ref):
    b, j = pl.program_id(0), pl.program_id(1)
    S = x_ref.shape[0]
    rows = 128

    def x_copy(seq):
        return pltpu.make_async_copy(x_hbm.at[seq], x_ref, x_sem)

    @pl.when((j == 0) & (b == 0))
    def _():
        x_copy(b).start()

    @pl.when((j == 1) & (b + 1 < pl.num_programs(0)))
    def _():
        x_copy(b + 1).start()

    n_lane_blocks = D_MODEL // LANES
    h_refs = (h0_ref, h1_ref, h2_ref)
    half = RET_DIM // 2

    def project(h_ref, i, epilogue, mc, slot):
        ridx = pl.ds(mc * ROW_CHUNK, ROW_CHUNK)
        res = jnp.dot(h_ref[ridx, :], w_ref[0, :, i * COL_BLOCK:(i + 1) * COL_BLOCK],
                      preferred_element_type=F32)
        if epilogue is None:
            for c in range(BLOCKS_PER_STEP):
                o_ref[0, i * BLOCKS_PER_STEP + c, ridx, :] = (
                    res[:, c * LANES:(c + 1) * LANES].astype(BF16))
            return None
        raw_ref[slot] = res
        return functools.partial(finish_piece, i, epilogue, mc, slot)

    def finish_piece(i, epilogue, mc, slot):
        ridx = pl.ds(mc * ROW_CHUNK, ROW_CHUNK)
        blocks = [raw_ref[slot, :, c * LANES:(c + 1) * LANES] for c in range(BLOCKS_PER_STEP)]
        if epilogue == "silu":
            halves = [0.5 * blk for blk in blocks]
            blocks = [hz + hz * jnp.tanh(hz) for hz in halves]
        elif epilogue is not None:
            cos, sin = cos_ref[0, ridx, :], sin_ref[0, ridx, :]
            if epilogue != 1.0:
                cos, sin = cos * epilogue, sin * epilogue
            for c in range(0, BLOCKS_PER_STEP, RET_DIM // LANES):
                t1, t2 = blocks[c], blocks[c + half // LANES]
                blocks[c] = t1 * cos - t2 * sin
                blocks[c + half // LANES] = t2 * cos + t1 * sin
        for c, blk in enumerate(blocks):
            o_ref[0, i * BLOCKS_PER_STEP + c, ridx, :] = blk.astype(BF16)

    def natural_slice(r0, wmul, shift):
        idx = pl.ds(r0, rows)
        xs = x_ref[idx, :]
        ms = jnp.mean(xs * xs, axis=-1, keepdims=True)
        hv = xs * lax.rsqrt(ms + EPS) * wmul + shift
        h0_ref[idx, :] = hv.astype(BF16)
        for cb in range(n_lane_blocks):
            hn_ref[cb, idx, :] = hv[:, cb * LANES:(cb + 1) * LANES]

    def hop_slice(hop, c):
        step = ATT_DILATIONS[hop + 1] // ATT_DILATIONS[hop]
        n_src = S // step
        src_ref = hp_ref if hop else hn_ref
        p0 = pl.multiple_of(c * rows, rows)
        start = (p0 % n_src) * step + p0 // n_src
        pieces = [src_ref[cb, pl.ds(start, rows, stride=step), :] for cb in range(n_lane_blocks)]
        h_refs[hop + 1][pl.ds(p0, rows), :] = jnp.concatenate(pieces, axis=-1).astype(BF16)
        if hop + 2 < len(ATT_DILATIONS):
            for cb in range(n_lane_blocks):
                hp_ref[cb, pl.ds(p0, rows), :] = pieces[cb]

    n_row_chunks = S // ROW_CHUNK
    slices_per_step = n_row_chunks
    hop_steps = S // rows // slices_per_step
    for hop in range(len(ATT_DILATIONS) - 1):
        first_use = (N_NATURAL_STEPS + hop * STEPS_PER_GROUP) // DOTS_PER_STEP
        assert 1 + (hop + 1) * hop_steps <= first_use

    def phase_of_step(s):
        if s == 0:
            return "natural"
        hop = (s - 1) // hop_steps
        return hop if hop + 1 < len(ATT_DILATIONS) else None

    steps_by_kind = {}
    for s in range(N_COL_STEPS // DOTS_PER_STEP):
        cbs = [s * DOTS_PER_STEP + i for i in range(DOTS_PER_STEP)]
        kind = tuple((_row_order_of_column_block(cb), _epilogue_of_column_block(cb))
                     for cb in cbs)
        steps_by_kind.setdefault((kind, phase_of_step(s)), []).append(s)
    for (kind, phase), steps in steps_by_kind.items():
        @pl.when(functools.reduce(jnp.logical_or, [j == s for s in steps]))
        def _(kind=kind, phase=phase):
            if phase == "natural":
                x_copy(b).wait()
                shift = mod_ref[0, :, 0:D_MODEL]
                wmul = nw_ref[0] * (1.0 + mod_ref[0, :, D_MODEL:2 * D_MODEL])
            pending, n_deferred = None, 0
            for mc in range(n_row_chunks):
                if phase == "natural":
                    for r0 in range(mc * ROW_CHUNK, (mc + 1) * ROW_CHUNK, rows):
                        natural_slice(r0, wmul, shift)
                for i, (order, epilogue) in enumerate(kind):
                    deferred = project(h_refs[order], i, epilogue, mc, n_deferred % 2)
                    if pending is not None:
                        pending()
                    pending = deferred
                    n_deferred += deferred is not None
                if phase not in ("natural", None):
                    first = 1 + phase * hop_steps
                    hop_slice(phase, (j - first) * slices_per_step + mc)
            if pending is not None:
                pending()


def _inproj(x, mod_l, norm_w, w_in, cos, sin, layer):
    B, S, D = x.shape
    tab = pl.BlockSpec((1, S, LANES), lambda b, j: (b, 0, 0))
    return pl.pallas_call(
        _inproj_kernel,
        grid=(B, N_COL_STEPS // DOTS_PER_STEP),
        in_specs=[
            pl.BlockSpec(memory_space=pl.ANY),
            pl.BlockSpec((1, 1, 3 * D), lambda b, j: (b, 0, 0)),
            pl.BlockSpec((1, 1, D), lambda b, j: (layer, 0, 0)),
            pl.BlockSpec((1, D, DOTS_PER_STEP * COL_BLOCK), lambda b, j: (layer, 0, j)),
            tab, tab,
        ],
        out_specs=pl.BlockSpec((1, DOTS_PER_STEP * BLOCKS_PER_STEP, S, LANES),
                               lambda b, j: (b, j, 0, 0)),
        out_shape=jax.ShapeDtypeStruct((B, N_BLOCKS, S, LANES), BF16),
        scratch_shapes=[pltpu.VMEM((S, D), BF16)] * N_ATT_GROUPS
                       + [pltpu.VMEM((D // LANES, S, LANES), F32)] * 2
                       + [pltpu.VMEM((S, D), F32), pltpu.SemaphoreType.DMA(()),
                          pltpu.VMEM((2, ROW_CHUNK, COL_BLOCK), F32)],
        compiler_params=_params("arbitrary", "arbitrary"),
        name="inproj",
    )(x, mod_l, norm_w, w_in, cos, sin)


def _attn_kernel(q0, k0, v0, q1, k1, v1, q2, k2, v2, z_ref, y_ref,
                 sb_sc, sd_sc, m_sc, acc_sc, l_sc):
    S = y_ref.shape[2]
    T = ATT_BLOCK
    c_exp2 = ATT_HEAD_DIM ** -0.5 * np.log2(np.e)
    row = lax.broadcasted_iota(jnp.int32, (T, 2 * T), 0)
    col = lax.broadcasted_iota(jnp.int32, (T, 2 * T), 1)
    mask_band = (col >= row) & (col <= row + T)
    mask_diag = (lax.broadcasted_iota(jnp.int32, (T, T), 1)
                 <= lax.broadcasted_iota(jnp.int32, (T, T), 0))
    groups = ((q0, k0, v0), (q1, k1, v1), (q2, k2, v2))
    d1, d2 = ATT_DILATIONS[1], ATT_DILATIONS[2]
    len1 = S // d1
    nb1 = len1 // T
    n_band0 = S // T - 1

    def scores(sl, g, q_start, band, blk, out_idx, m_idx):
        q_ref, k_ref, _ = groups[g]
        q = q_ref[0, sl, pl.ds(q_start, T), :]
        if band:
            k = k_ref[0, sl, pl.ds(q_start - T, 2 * T), :]
        else:
            k = k_ref[0, sl, pl.ds(q_start, T), :]
        s = lax.dot_general(q, k, (((1,), (1,)), ((), ())), preferred_element_type=F32) * c_exp2
        s = jnp.where(mask_band if band else mask_diag, s, NEG_INF)
        (sb_sc if band else sd_sc)[blk] = s
        m_sc[g, out_idx, :] = jnp.broadcast_to(jnp.max(s, axis=-1, keepdims=True), (T, LANES))

    def weighted_values(sl, g, q_start, band, blk, out_idx, m_idx):
        v_ref = groups[g][2]
        m = m_sc[min(g, 1), m_idx, :]
        if band:
            p = jnp.exp2(sb_sc[blk] - jnp.concatenate([m, m], axis=-1))
            v = v_ref[0, sl, pl.ds(q_start - T, 2 * T), :]
        else:
            p = jnp.exp2(sd_sc[blk] - m)
            v = v_ref[0, sl, pl.ds(q_start, T), :]
        v1 = jnp.concatenate([v, jnp.ones_like(v)], axis=-1)
        r = jnp.dot(p.astype(BF16), v1, preferred_element_type=F32)
        acc_sc[g, out_idx, :] = r[:, :LANES]
        l_sc[g, out_idx, :] = r[:, LANES:]

    blocks = []
    for i in range(S // T):
        blocks.append((0, i * T, i > 0, max(i - 1, 0), pl.ds(i * T, T), pl.ds(i * T, T)))
    for r in range(d1):
        for n in range(nb1):
            idx = pl.ds(r * len1 + n * T, T)
            blk = 1 + r if n == 0 else n_band0 + r * (nb1 - 1) + n - 1
            blocks.append((1, r * len1 + n * T, n > 0, blk, idx, idx))
    ratio = d2 // d1
    for r in range(d2):
        idx = pl.ds((r % d1) * len1 + r // d1, T, stride=ratio)
        blocks.append((2, r * T, False, 1 + d1 + r, idx, idx))

    def all_blocks(fn):
        for blk in blocks:
            fn(*blk)

    rows = 256

    def natural_rows(c):
        p0 = pl.multiple_of(c * rows, rows)
        return pl.ds(p0, rows), pl.ds((p0 % len1) * d1 + p0 // len1, rows, stride=d1)

    def shared_max(c, carry):
        idx, nat = natural_rows(c)
        m = jnp.maximum(jnp.maximum(m_sc[0, nat, :], m_sc[1, idx, :]), m_sc[2, idx, :])
        m_sc[0, nat, :] = m
        m_sc[1, idx, :] = m
        return carry

    def combine(c, carry):
        idx, nat = natural_rows(c)
        o = ((acc_sc[0, nat, :] + acc_sc[1, idx, :] + acc_sc[2, idx, :])
             / (l_sc[0, nat, :] + l_sc[1, idx, :] + l_sc[2, idx, :]))
        acc_sc[0, nat, :] = o
        return carry

    def finish(sl, c, carry):
        idx = pl.ds(pl.multiple_of(c * rows, rows), rows)
        gate = z_ref[0, sl, idx, :].astype(F32)
        y_ref[0, sl, idx, :] = (acc_sc[0, idx, :] * gate).astype(BF16)
        return carry

    for sl in range(ATT_SLOTS_PER_STEP):
        all_blocks(functools.partial(scores, sl))
        lax.fori_loop(0, S // rows, shared_max, 0, unroll=True)
        all_blocks(functools.partial(weighted_values, sl))
        lax.fori_loop(0, S // rows, combine, 0, unroll=True)
        lax.fori_loop(0, S // rows, functools.partial(finish, sl), 0)


def _attn(proj):
    B, _, S, _ = proj.shape
    sps = ATT_SLOTS_PER_STEP

    def spec(base):
        return pl.BlockSpec((1, sps, S, LANES), lambda b, s: (b, base // sps + s, 0, 0))

    in_specs = []
    for g in range(N_ATT_GROUPS):
        in_specs += [spec(BLK_Q[g]), spec(BLK_K[g]), spec(BLK_V[g])]
    in_specs.append(spec(BLK_ZA))
    n_diag = sum(ATT_DILATIONS)
    n_band = N_ATT_GROUPS * S // ATT_BLOCK - n_diag
    return pl.pallas_call(
        _attn_kernel,
        grid=(B, ATT_SLOTS // sps),
        in_specs=in_specs,
        out_specs=pl.BlockSpec((1, sps, S, LANES), lambda b, s: (b, s, 0, 0)),
        out_shape=jax.ShapeDtypeStruct((B, ATT_SLOTS, S, LANES), BF16),
        scratch_shapes=[pltpu.VMEM((n_band, ATT_BLOCK, 2 * ATT_BLOCK), F32),
                        pltpu.VMEM((n_diag, ATT_BLOCK, ATT_BLOCK), F32),
                        pltpu.VMEM((N_ATT_GROUPS, S, LANES), F32),
                        pltpu.VMEM((N_ATT_GROUPS, S, LANES), F32),
                        pltpu.VMEM((N_ATT_GROUPS, S, LANES), F32)],
        compiler_params=_params("arbitrary", "arbitrary"),
        name="attn",
    )(*([proj] * 10))


def _ret_decay_tables():
    H, C = RET_HEADS, RET_CHUNK
    log_g = np.log1p(-np.exp2(-5.0 - np.arange(H, dtype=np.float64)))
    idx = np.arange(C, dtype=np.float64)
    diff = idx[:, None] - idx[None, :]
    inner = np.where(diff >= 0, np.exp(log_g[:, None, None] * np.maximum(diff, 0.0)), 0.0)
    q_decay = np.exp(log_g[:, None] * (idx + 1.0))
    k_decay = np.exp(log_g[:, None] * (C - 1.0 - idx))
    chunk_decay = np.exp(log_g * C)
    f = lambda a: jnp.asarray(a, dtype=F32)
    return f(inner), f(q_decay[:, :, None]), f(k_decay[:, :, None]), f(chunk_decay)


def _ret_kernel(cd_ref, q_ref, k_ref, v_ref, z_ref, inner_ref, qd_ref, kd_ref,
                gnw_ref, y_ref, intra_sc, u_sc, st_sc):
    S = y_ref.shape[2]
    C = RET_CHUNK
    n_chunks = S // C
    half = RET_DIM // 2
    per_head = RET_DIM // LANES

    def head(ref, hh, idx):
        return jnp.concatenate([ref[0, hh * per_head + i, idx, :] for i in range(per_head)],
                               axis=-1)

    def state_free(hh):
        for n in range(n_chunks):
            idx = pl.ds(n * C, C)
            qc, kc, vc = head(q_ref, hh, idx), head(k_ref, hh, idx), head(v_ref, hh, idx)
            att = lax.dot_general(qc, kc, (((1,), (1,)), ((), ())), preferred_element_type=F32)
            att = (att * inner_ref[hh]).astype(BF16)
            intra_sc[hh % 2, idx, :] = jnp.dot(att, vc, preferred_element_type=F32)
            if n + 1 < n_chunks:
                kdt = (kc.astype(F32) * kd_ref[hh]).T.astype(BF16)
                u_sc[hh % 2, n] = jnp.dot(kdt, vc, preferred_element_type=F32)

    def recurrence(hh):
        cd = cd_ref[pl.program_id(1) * RET_HEADS_PER_STEP + hh]
        slab = 64
        for r0 in range(0, RET_DIM, slab):
            state = jnp.zeros((slab, RET_DIM), F32)
            for n in range(1, n_chunks):
                state = state * cd + u_sc[hh % 2, n - 1, r0:r0 + slab, :]
                st_sc[hh % 2, n, r0:r0 + slab, :] = state.astype(BF16)

    def finish(hh):
        gnw = gnw_ref[:, hh * RET_DIM:(hh + 1) * RET_DIM]
        for n in range(n_chunks):
            idx = pl.ds(n * C, C)
            out = intra_sc[hh % 2, idx, :]
            if n > 0:
                out = out + jnp.dot(head(q_ref, hh, idx), st_sc[hh % 2, n],
                                    preferred_element_type=F32) * qd_ref[hh]
            mu = jnp.mean(out, axis=-1, keepdims=True)
            cen = out - mu
            var = jnp.mean(cen * cen, axis=-1, keepdims=True)
            o = cen * lax.rsqrt(var + EPS) * gnw
            gate = head(z_ref, hh, idx).astype(F32)
            y = (o * gate).astype(BF16)
            y_ref[0, hh * per_head, idx, :] = y[:, :half]
            y_ref[0, hh * per_head + 1, idx, :] = y[:, half:]

    for hh in range(RET_HEADS_PER_STEP):
        state_free(hh)
        if hh > 0:
            finish(hh - 1)
        recurrence(hh)
    finish(RET_HEADS_PER_STEP - 1)


def _ret(proj, gn_w_l):
    B, _, S, _ = proj.shape
    inner, q_decay, k_decay, chunk_decay = _ret_decay_tables()
    blocks = RET_HEADS_PER_STEP * RET_DIM // LANES
    hps = RET_HEADS_PER_STEP

    def spec(base):
        return pl.BlockSpec((1, blocks, S, LANES), lambda b, h: (b, base // blocks + h, 0, 0))

    return pl.pallas_call(
        _ret_kernel,
        grid=(B, RET_HEADS // hps),
        in_specs=[
            pl.BlockSpec(memory_space=pltpu.SMEM),
            spec(BLK_QR), spec(BLK_KR), spec(BLK_VR), spec(BLK_ZR),
            pl.BlockSpec((hps, RET_CHUNK, RET_CHUNK), lambda b, h: (h, 0, 0)),
            pl.BlockSpec((hps, RET_CHUNK, 1), lambda b, h: (h, 0, 0)),
            pl.BlockSpec((hps, RET_CHUNK, 1), lambda b, h: (h, 0, 0)),
            pl.BlockSpec((1, hps * RET_DIM), lambda b, h: (0, h)),
        ],
        out_specs=pl.BlockSpec((1, blocks, S, LANES), lambda b, h: (b, h, 0, 0)),
        out_shape=jax.ShapeDtypeStruct((B, RET_HEADS * RET_DIM // LANES, S, LANES), BF16),
        scratch_shapes=[pltpu.VMEM((2, S, RET_DIM), F32),
                        pltpu.VMEM((2, S // RET_CHUNK, RET_DIM, RET_DIM), F32),
                        pltpu.VMEM((2, S // RET_CHUNK, RET_DIM, RET_DIM), BF16)],
        compiler_params=_params("arbitrary", "arbitrary"),
        name="retention",
    )(chunk_decay, proj, proj, proj, proj, inner, q_decay, k_decay,
      gn_w_l.reshape(1, RET_HEADS * RET_DIM))


def _outproj_kernel(ya_ref, yr_ref, ga_ref, gr_ref, x_ref, gate_ref, wpa_ref, wpr_ref, wo_ref,
                    fnw_ref, o_ref, *, final_norm):
    for r0 in range(0, x_ref.shape[1], ROW_CHUNK):
        ridx = pl.ds(r0, ROW_CHUNK)

        def cat(ref):
            return jnp.concatenate([ref[0, i, ridx, :] for i in range(ref.shape[1])], axis=-1)

        a = jnp.dot(cat(ya_ref), wpa_ref[0], preferred_element_type=F32)
        r = jnp.dot(cat(yr_ref), wpr_ref[0], preferred_element_type=F32)
        merged = (jax.nn.sigmoid(cat(ga_ref).astype(F32)) * a
                  + jax.nn.sigmoid(cat(gr_ref).astype(F32)) * r)
        out = x_ref[0, ridx, :] + gate_ref[0] * jnp.dot(merged.astype(BF16), wo_ref[0],
                                                        preferred_element_type=F32)
        if final_norm:
            ms = jnp.mean(out * out, axis=-1, keepdims=True)
            out = out * lax.rsqrt(ms + EPS) * fnw_ref[...]
        o_ref[0, ridx, :] = out


def _outproj(x, ya, yr, proj, mod_l, wpa, wpr, wo, final_norm_w, layer, final_norm):
    B, S, D = x.shape
    tm = 1024
    n_g = D // LANES
    weight = lambda w: pl.BlockSpec((1,) + w.shape[1:], lambda b, i: (layer, 0, 0))
    return pl.pallas_call(
        functools.partial(_outproj_kernel, final_norm=final_norm),
        grid=(B, S // tm),
        in_specs=[
            pl.BlockSpec((1, ya.shape[1], tm, LANES), lambda b, i: (b, 0, i, 0)),
            pl.BlockSpec((1, yr.shape[1], tm, LANES), lambda b, i: (b, 0, i, 0)),
            pl.BlockSpec((1, n_g, tm, LANES), lambda b, i: (b, BLK_GA // n_g, i, 0)),
            pl.BlockSpec((1, n_g, tm, LANES), lambda b, i: (b, BLK_GR // n_g, i, 0)),
            pl.BlockSpec((1, tm, D), lambda b, i: (b, i, 0)),
            pl.BlockSpec((1, 1, D), lambda b, i: (b, 0, 2)),
            weight(wpa), weight(wpr), weight(wo),
            pl.BlockSpec((1, D), lambda b, i: (0, 0)),
        ],
        out_specs=pl.BlockSpec((1, tm, D), lambda b, i: (b, i, 0)),
        out_shape=jax.ShapeDtypeStruct((B, S, D), F32),
        compiler_params=_params("arbitrary", "arbitrary"),
        name="outproj",
    )(ya, yr, proj, proj, x, mod_l, wpa, wpr, wo, final_norm_w.reshape(1, D))


def kernel(x, c, positions, norm_w, w_ada, b_ada, w_in, ret_gn_w, w_proj_attn, w_proj_ret, w_out,
           final_norm_w):
    B = x.shape[0]
    mod = _ada(c, w_ada, b_ada).reshape(DEPTH, B, 1, 3 * D_MODEL)
    cos, sin = _rope_tables(positions)
    w_in_b = jnp.concatenate([w_in[:, :, s:s + n] for s, n in _column_segments()],
                             axis=-1).astype(BF16)
    wpa_b = w_proj_attn.astype(BF16)
    wpr_b = w_proj_ret.astype(BF16)
    wo_b = w_out.astype(BF16)
    norm_w3 = norm_w.reshape(DEPTH, 1, D_MODEL)
    for l in range(DEPTH):
        proj = _inproj(x, mod[l], norm_w3, w_in_b, cos, sin, layer=l)
        ya = _attn(proj)
        yr = _ret(proj, ret_gn_w[l])
        x = _outproj(x, ya, yr, proj, mod[l], wpa_b, wpr_b, wo_b, final_norm_w, layer=l,
                     final_norm=(l == DEPTH - 1))
    return x
```

```python
import functools

import numpy as np
import jax
import jax.numpy as jnp
from jax import lax
from jax.experimental import pallas as pl
from jax.experimental.pallas import tpu as pltpu

D_MODEL = 1024
SEQ = 2048
DEPTH = 4
ATT_DILATIONS = (1, 4, 16)
N_ATT_GROUPS = 3
ATT_SLOTS = 4
ATT_HEAD_DIM = 128
ATT_BLOCK = 128
ATT_SLOTS_PER_STEP = 2
RET_HEADS = 4
RET_DIM = 256
RET_CHUNK = 512
RET_HEADS_PER_STEP = 4
ROPE_BASE = 10000.0
EPS = 1e-6
NEG_INF = -1e30

LANES = 128
COL_BLOCK = 512
BLOCKS_PER_STEP = COL_BLOCK // LANES
DOTS_PER_STEP = 2
ROW_CHUNK = 512
W_RING = 3
ATT_QKV = N_ATT_GROUPS * ATT_SLOTS * ATT_HEAD_DIM
IN_WIDTH = 3 * ATT_QKV + ATT_SLOTS * ATT_HEAD_DIM + 4 * RET_HEADS * RET_DIM + 2 * D_MODEL
N_COL_STEPS = IN_WIDTH // COL_BLOCK
N_BLOCKS = IN_WIDTH // LANES


def _column_segments():
    grp = ATT_SLOTS * ATT_HEAD_DIM
    qkv = lambda t, g: (t * ATT_QKV + g * grp, grp)
    seg = [qkv(0, 0), qkv(1, 0), qkv(2, 0), (3 * ATT_QKV, IN_WIDTH - 3 * ATT_QKV)]
    for g in range(1, N_ATT_GROUPS):
        seg += [qkv(0, g), qkv(1, g), qkv(2, g)]
    return seg


_GRP_BLOCKS = ATT_SLOTS * ATT_HEAD_DIM // LANES
_WIDE_BLOCKS = RET_HEADS * RET_DIM // LANES
BLK_ZA = 3 * _GRP_BLOCKS
BLK_QR = BLK_ZA + _GRP_BLOCKS
BLK_KR = BLK_QR + _WIDE_BLOCKS
BLK_VR = BLK_KR + _WIDE_BLOCKS
BLK_ZR = BLK_VR + _WIDE_BLOCKS
BLK_GA = BLK_ZR + _WIDE_BLOCKS
BLK_GR = BLK_GA + D_MODEL // LANES
_BLK_G1 = BLK_GR + D_MODEL // LANES
BLK_Q = (0,) + tuple(_BLK_G1 + (g - 1) * 3 * _GRP_BLOCKS for g in range(1, N_ATT_GROUPS))
BLK_K = tuple(b + _GRP_BLOCKS for b in BLK_Q)
BLK_V = tuple(b + 2 * _GRP_BLOCKS for b in BLK_Q)
N_NATURAL_STEPS = _BLK_G1 // BLOCKS_PER_STEP
STEPS_PER_GROUP = 3 * _GRP_BLOCKS // BLOCKS_PER_STEP

VMEM_LIMIT = 60 * 1024 * 1024
F32 = jnp.float32
BF16 = jnp.bfloat16


def _params(*sem):
    return pltpu.CompilerParams(dimension_semantics=sem, vmem_limit_bytes=VMEM_LIMIT)


def _ada_kernel(c_ref, w_ref, b_ref, o_ref):
    c = c_ref[...]
    c_act = (c * jax.nn.sigmoid(c)).astype(BF16)
    acc = jnp.dot(c_act, w_ref[0].astype(BF16), preferred_element_type=F32)
    o_ref[0] = acc + b_ref[0]


def _ada(c, w_ada, b_ada):
    B = c.shape[0]
    n_col = 3 * D_MODEL // D_MODEL
    return pl.pallas_call(
        _ada_kernel,
        grid=(DEPTH, n_col),
        in_specs=[
            pl.BlockSpec((B, D_MODEL), lambda l, j: (0, 0)),
            pl.BlockSpec((1, D_MODEL, D_MODEL), lambda l, j: (l, 0, j)),
            pl.BlockSpec((1, 1, D_MODEL), lambda l, j: (l, 0, j)),
        ],
        out_specs=pl.BlockSpec((1, B, D_MODEL), lambda l, j: (l, 0, j)),
        out_shape=jax.ShapeDtypeStruct((DEPTH, B, 3 * D_MODEL), F32),
        compiler_params=_params("arbitrary", "arbitrary"),
        name="ada",
    )(c, w_ada, b_ada.reshape(DEPTH, 1, 3 * D_MODEL))


def _rope_kernel(pos_ref, theta_ref, cos_ref, sin_ref):
    ang = pos_ref[0].astype(F32) * theta_ref[...]
    cos_ref[0] = jnp.cos(ang)
    sin_ref[0] = jnp.sin(ang)


def _rope_tables(positions):
    B, S = positions.shape
    half = RET_DIM // 2
    theta = ROPE_BASE ** (-jnp.arange(half, dtype=F32) / half)
    spec = pl.BlockSpec((1, S, half), lambda b: (b, 0, 0))
    return pl.pallas_call(
        _rope_kernel,
        grid=(B,),
        in_specs=[pl.BlockSpec((1, S, 1), lambda b: (b, 0, 0)),
                  pl.BlockSpec((1, half), lambda b: (0, 0))],
        out_specs=[spec, spec],
        out_shape=[jax.ShapeDtypeStruct((B, S, half), F32)] * 2,
        compiler_params=_params("arbitrary"),
        name="rope",
    )(positions.reshape(B, S, 1), theta.reshape(1, half))


def _row_order_of_column_block(cb):
    return 0 if cb < N_NATURAL_STEPS else 1 + (cb - N_NATURAL_STEPS) // STEPS_PER_GROUP


def _epilogue_of_column_block(cb):
    blk = cb * BLOCKS_PER_STEP
    if BLK_QR <= blk < BLK_KR:
        return 1.0
    if BLK_KR <= blk < BLK_VR:
        return RET_DIM ** -0.5
    if BLK_ZA <= blk < BLK_QR or BLK_ZR <= blk < BLK_GA:
        return "silu"
    return None


def _inproj_kernel(x_hbm, mod_ref, nw_ref, w_hbm, cos_ref, sin_ref, o_ref,
                   h0_ref, h1_ref, h2_ref, hn_ref, hp_ref, x_ref, x_sem, w_buf, w_sem, *, layer):
    b, j = pl.program_id(0), pl.program_id(1)
    S = x_ref.shape[0]
    rows = 128

    n_j = pl.num_programs(1)
    g = b * n_j + j
    n_g = pl.num_programs(0) * n_j
    step_cols = DOTS_PER_STEP * COL_BLOCK

    def w_copy(step):
        col = pl.multiple_of(lax.rem(step, n_j) * step_cols, step_cols)
        slot = lax.rem(step, W_RING)
        return pltpu.make_async_copy(w_hbm.at[layer, :, pl.ds(col, step_cols)],
                                     w_buf.at[slot], w_sem.at[slot])

    @pl.when(g == 0)
    def _():
        for ahead in range(W_RING - 1):
            w_copy(g + ahead).start()

    @pl.when(g + W_RING - 1 < n_g)
    def _():
        w_copy(g + W_RING - 1).start()

    w_copy(g).wait()
    w_ref = w_buf.at[lax.rem(g, W_RING)]

    def x_copy(seq):
        return pltpu.make_async_copy(x_hbm.at[seq], x_ref, x_sem)

    @pl.when((j == 0) & (b == 0))
    def _():
        x_copy(b).start()

    @pl.when((j == 1) & (b + 1 < pl.num_programs(0)))
    def _():
        x_copy(b + 1).start()

    n_lane_blocks = D_MODEL // LANES
    h_refs = (h0_ref, h1_ref, h2_ref)
    half = RET_DIM // 2

    def project(h_ref, i, epilogue, mc):
        ridx = pl.ds(mc * ROW_CHUNK, ROW_CHUNK)
        res = jnp.dot(h_ref[ridx, :], w_ref[:, i * COL_BLOCK:(i + 1) * COL_BLOCK],
                      preferred_element_type=F32)
        blocks = [res[:, c * LANES:(c + 1) * LANES] for c in range(BLOCKS_PER_STEP)]
        if epilogue == "silu":
            halves = [0.5 * blk for blk in blocks]
            blocks = [hz + hz * jnp.tanh(hz) for hz in halves]
        elif epilogue is not None:
            cos, sin = cos_ref[0, ridx, :], sin_ref[0, ridx, :]
            if epilogue != 1.0:
                cos, sin = cos * epilogue, sin * epilogue
            for c in range(0, BLOCKS_PER_STEP, RET_DIM // LANES):
                t1, t2 = blocks[c], blocks[c + half // LANES]
                blocks[c] = t1 * cos - t2 * sin
                blocks[c + half // LANES] = t2 * cos + t1 * sin
        for c, blk in enumerate(blocks):
            o_ref[0, i * BLOCKS_PER_STEP + c, ridx, :] = blk.astype(BF16)

    def natural_slice(r0, wmul, shift):
        idx = pl.ds(r0, rows)
        xs = x_ref[idx, :]
        ms = jnp.mean(xs * xs, axis=-1, keepdims=True)
        hv = xs * lax.rsqrt(ms + EPS) * wmul + shift
        h0_ref[idx, :] = hv.astype(BF16)
        for cb in range(n_lane_blocks):
            hn_ref[cb, idx, :] = hv[:, cb * LANES:(cb + 1) * LANES]

    def hop_slice(hop, c):
        step = ATT_DILATIONS[hop + 1] // ATT_DILATIONS[hop]
        n_src = S // step
        src_ref = hp_ref if hop else hn_ref
        p0 = pl.multiple_of(c * rows, rows)
        start = (p0 % n_src) * step + p0 // n_src
        pieces = [src_ref[cb, pl.ds(start, rows, stride=step), :] for cb in range(n_lane_blocks)]
        h_refs[hop + 1][pl.ds(p0, rows), :] = jnp.concatenate(pieces, axis=-1).astype(BF16)
        if hop + 2 < len(ATT_DILATIONS):
            for cb in range(n_lane_blocks):
                hp_ref[cb, pl.ds(p0, rows), :] = pieces[cb]

    n_row_chunks = S // ROW_CHUNK
    slices_per_step = n_row_chunks
    hop_steps = S // rows // slices_per_step
    for hop in range(len(ATT_DILATIONS) - 1):
        first_use = (N_NATURAL_STEPS + hop * STEPS_PER_GROUP) // DOTS_PER_STEP
        assert 1 + (hop + 1) * hop_steps <= first_use

    def phase_of_step(s):
        if s == 0:
            return "natural"
        hop = (s - 1) // hop_steps
        return hop if hop + 1 < len(ATT_DILATIONS) else None

    steps_by_kind = {}
    for s in range(N_COL_STEPS // DOTS_PER_STEP):
        cbs = [s * DOTS_PER_STEP + i for i in range(DOTS_PER_STEP)]
        kind = tuple((_row_order_of_column_block(cb), _epilogue_of_column_block(cb))
                     for cb in cbs)
        steps_by_kind.setdefault((kind, phase_of_step(s)), []).append(s)
    for (kind, phase), steps in steps_by_kind.items():
        @pl.when(functools.reduce(jnp.logical_or, [j == s for s in steps]))
        def _(kind=kind, phase=phase):
            if phase == "natural":
                x_copy(b).wait()
                shift = mod_ref[0, :, 0:D_MODEL]
                wmul = nw_ref[0] * (1.0 + mod_ref[0, :, D_MODEL:2 * D_MODEL])
            for mc in range(n_row_chunks):
                if phase == "natural":
                    for r0 in range(mc * ROW_CHUNK, (mc + 1) * ROW_CHUNK, rows):
                        natural_slice(r0, wmul, shift)
                for i, (order, epilogue) in enumerate(kind):
                    project(h_refs[order], i, epilogue, mc)
                if phase not in ("natural", None):
                    first = 1 + phase * hop_steps
                    hop_slice(phase, (j - first) * slices_per_step + mc)


def _inproj(x, mod_l, norm_w, w_in, cos, sin, layer):
    B, S, D = x.shape
    tab = pl.BlockSpec((1, S, LANES), lambda b, j: (b, 0, 0))
    return pl.pallas_call(
        functools.partial(_inproj_kernel, layer=layer),
        grid=(B, N_COL_STEPS // DOTS_PER_STEP),
        in_specs=[
            pl.BlockSpec(memory_space=pl.ANY),
            pl.BlockSpec((1, 1, 3 * D), lambda b, j: (b, 0, 0)),
            pl.BlockSpec((1, 1, D), lambda b, j: (layer, 0, 0)),
            pl.BlockSpec(memory_space=pl.ANY),
            tab, tab,
        ],
        out_specs=pl.BlockSpec((1, DOTS_PER_STEP * BLOCKS_PER_STEP, S, LANES),
                               lambda b, j: (b, j, 0, 0)),
        out_shape=jax.ShapeDtypeStruct((B, N_BLOCKS, S, LANES), BF16),
        scratch_shapes=[pltpu.VMEM((S, D), BF16)] * N_ATT_GROUPS
                       + [pltpu.VMEM((D // LANES, S, LANES), F32)] * 2
                       + [pltpu.VMEM((S, D), F32), pltpu.SemaphoreType.DMA(()),
                          pltpu.VMEM((W_RING, D, DOTS_PER_STEP * COL_BLOCK), BF16),
                          pltpu.SemaphoreType.DMA((W_RING,))],
        compiler_params=_params("arbitrary", "arbitrary"),
        name="inproj",
    )(x, mod_l, norm_w, w_in, cos, sin)


def _attn_kernel(q0, k0, v0, q1, k1, v1, q2, k2, v2, z_ref, y_ref,
                 sb_sc, sd_sc, m_sc, acc_sc, l_sc):
    S = y_ref.shape[2]
    T = ATT_BLOCK
    c_exp2 = ATT_HEAD_DIM ** -0.5 * np.log2(np.e)
    row = lax.broadcasted_iota(jnp.int32, (T, 2 * T), 0)
    col = lax.broadcasted_iota(jnp.int32, (T, 2 * T), 1)
    mask_band = (col >= row) & (col <= row + T)
    mask_diag = (lax.broadcasted_iota(jnp.int32, (T, T), 1)
                 <= lax.broadcasted_iota(jnp.int32, (T, T), 0))
    groups = ((q0, k0, v0), (q1, k1, v1), (q2, k2, v2))
    d1, d2 = ATT_DILATIONS[1], ATT_DILATIONS[2]
    len1 = S // d1
    nb1 = len1 // T
    n_band0 = S // T - 1

    def scores(sl, g, q_start, band, blk, out_idx, m_idx):
        q_ref, k_ref, _ = groups[g]
        q = q_ref[0, sl, pl.ds(q_start, T), :]
        if band:
            k = k_ref[0, sl, pl.ds(q_start - T, 2 * T), :]
        else:
            k = k_ref[0, sl, pl.ds(q_start, T), :]
        s = lax.dot_general(q, k, (((1,), (1,)), ((), ())), preferred_element_type=F32) * c_exp2
        s = jnp.where(mask_band if band else mask_diag, s, NEG_INF)
        (sb_sc if band else sd_sc)[blk] = s
        m_sc[g, out_idx, :] = jnp.broadcast_to(jnp.max(s, axis=-1, keepdims=True), (T, LANES))

    def weighted_values(sl, g, q_start, band, blk, out_idx, m_idx):
        v_ref = groups[g][2]
        m = m_sc[min(g, 1), m_idx, :]
        if band:
            p = jnp.exp2(sb_sc[blk] - jnp.concatenate([m, m], axis=-1))
            v = v_ref[0, sl, pl.ds(q_start - T, 2 * T), :]
        else:
            p = jnp.exp2(sd_sc[blk] - m)
            v = v_ref[0, sl, pl.ds(q_start, T), :]
        v1 = jnp.concatenate([v, jnp.ones_like(v)], axis=-1)
        r = jnp.dot(p.astype(BF16), v1, preferred_element_type=F32)
        acc_sc[g, out_idx, :] = r[:, :LANES]
        l_sc[g, out_idx, :] = r[:, LANES:]

    blocks = []
    for i in range(S // T):
        blocks.append((0, i * T, i > 0, max(i - 1, 0), pl.ds(i * T, T), pl.ds(i * T, T)))
    for r in range(d1):
        for n in range(nb1):
            idx = pl.ds(r * len1 + n * T, T)
            blk = 1 + r if n == 0 else n_band0 + r * (nb1 - 1) + n - 1
            blocks.append((1, r * len1 + n * T, n > 0, blk, idx, idx))
    ratio = d2 // d1
    for r in range(d2):
        idx = pl.ds((r % d1) * len1 + r // d1, T, stride=ratio)
        blocks.append((2, r * T, False, 1 + d1 + r, idx, idx))

    def all_blocks(fn):
        for blk in blocks:
            fn(*blk)

    rows = 256

    def natural_rows(c):
        p0 = pl.multiple_of(c * rows, rows)
        return pl.ds(p0, rows), pl.ds((p0 % len1) * d1 + p0 // len1, rows, stride=d1)

    def shared_max(c, carry):
        idx, nat = natural_rows(c)
        m = jnp.maximum(jnp.maximum(m_sc[0, nat, :], m_sc[1, idx, :]), m_sc[2, idx, :])
        m_sc[0, nat, :] = m
        m_sc[1, idx, :] = m
        return carry

    def combine(c, carry):
        idx, nat = natural_rows(c)
        o = ((acc_sc[0, nat, :] + acc_sc[1, idx, :] + acc_sc[2, idx, :])
             / (l_sc[0, nat, :] + l_sc[1, idx, :] + l_sc[2, idx, :]))
        acc_sc[0, nat, :] = o
        return carry

    def finish(sl, c, carry):
        idx = pl.ds(pl.multiple_of(c * rows, rows), rows)
        gate = z_ref[0, sl, idx, :].astype(F32)
        y_ref[0, sl, idx, :] = (acc_sc[0, idx, :] * gate).astype(BF16)
        return carry

    for sl in range(ATT_SLOTS_PER_STEP):
        all_blocks(functools.partial(scores, sl))
        lax.fori_loop(0, S // rows, shared_max, 0, unroll=True)
        all_blocks(functools.partial(weighted_values, sl))
        lax.fori_loop(0, S // rows, combine, 0, unroll=True)
        lax.fori_loop(0, S // rows, functools.partial(finish, sl), 0)


def _attn(proj):
    B, _, S, _ = proj.shape
    sps = ATT_SLOTS_PER_STEP

    def spec(base):
        return pl.BlockSpec((1, sps, S, LANES), lambda b, s: (b, base // sps + s, 0, 0))

    in_specs = []
    for g in range(N_ATT_GROUPS):
        in_specs += [spec(BLK_Q[g]), spec(BLK_K[g]), spec(BLK_V[g])]
    in_specs.append(spec(BLK_ZA))
    n_diag = sum(ATT_DILATIONS)
    n_band = N_ATT_GROUPS * S // ATT_BLOCK - n_diag
    return pl.pallas_call(
        _attn_kernel,
        grid=(B, ATT_SLOTS // sps),
        in_specs=in_specs,
        out_specs=pl.BlockSpec((1, sps, S, LANES), lambda b, s: (b, s, 0, 0)),
        out_shape=jax.ShapeDtypeStruct((B, ATT_SLOTS, S, LANES), BF16),
        scratch_shapes=[pltpu.VMEM((n_band, ATT_BLOCK, 2 * ATT_BLOCK), F32),
                        pltpu.VMEM((n_diag, ATT_BLOCK, ATT_BLOCK), F32),
                        pltpu.VMEM((N_ATT_GROUPS, S, LANES), F32),
                        pltpu.VMEM((N_ATT_GROUPS, S, LANES), F32),
                        pltpu.VMEM((N_ATT_GROUPS, S, LANES), F32)],
        compiler_params=_params("arbitrary", "arbitrary"),
        name="attn",
    )(*([proj] * 10))


def _ret_decay_tables():
    H, C = RET_HEADS, RET_CHUNK
    log_g = np.log1p(-np.exp2(-5.0 - np.arange(H, dtype=np.float64)))
    idx = np.arange(C, dtype=np.float64)
    diff = idx[:, None] - idx[None, :]
    inner = np.where(diff >= 0, np.exp(log_g[:, None, None] * np.maximum(diff, 0.0)), 0.0)
    q_decay = np.exp(log_g[:, None] * (idx + 1.0))
    k_decay = np.exp(log_g[:, None] * (C - 1.0 - idx))
    chunk_decay = np.exp(log_g * C)
    f = lambda a: jnp.asarray(a, dtype=F32)
    return f(inner), f(q_decay[:, :, None]), f(k_decay[:, :, None]), f(chunk_decay)


def _ret_kernel(cd_ref, q_ref, k_ref, v_ref, z_ref, inner_ref, qd_ref, kd_ref,
                gnw_ref, y_ref, intra_sc, u_sc, st_sc):
    S = y_ref.shape[2]
    C = RET_CHUNK
    n_chunks = S // C
    half = RET_DIM // 2
    per_head = RET_DIM // LANES

    def head(ref, hh, idx):
        return jnp.concatenate([ref[0, hh * per_head + i, idx, :] for i in range(per_head)],
                               axis=-1)

    def state_free(hh):
        for n in range(n_chunks):
            idx = pl.ds(n * C, C)
            qc, kc, vc = head(q_ref, hh, idx), head(k_ref, hh, idx), head(v_ref, hh, idx)
            att = lax.dot_general(qc, kc, (((1,), (1,)), ((), ())), preferred_element_type=F32)
            att = (att * inner_ref[hh]).astype(BF16)
            intra_sc[hh % 2, idx, :] = jnp.dot(att, vc, preferred_element_type=F32)
            if n + 1 < n_chunks:
                kdt = (kc.astype(F32) * kd_ref[hh]).T.astype(BF16)
                u_sc[hh % 2, n] = jnp.dot(kdt, vc, preferred_element_type=F32)

    def recurrence(hh):
        cd = cd_ref[pl.program_id(1) * RET_HEADS_PER_STEP + hh]
        slab = 64
        for r0 in range(0, RET_DIM, slab):
            state = jnp.zeros((slab, RET_DIM), F32)
            for n in range(1, n_chunks):
                state = state * cd + u_sc[hh % 2, n - 1, r0:r0 + slab, :]
                st_sc[hh % 2, n, r0:r0 + slab, :] = state.astype(BF16)

    def finish(hh):
        gnw = gnw_ref[:, hh * RET_DIM:(hh + 1) * RET_DIM]
        for n in range(n_chunks):
            idx = pl.ds(n * C, C)
            out = intra_sc[hh % 2, idx, :]
            if n > 0:
                out = out + jnp.dot(head(q_ref, hh, idx), st_sc[hh % 2, n],
                                    preferred_element_type=F32) * qd_ref[hh]
            mu = jnp.mean(out, axis=-1, keepdims=True)
            cen = out - mu
            var = jnp.mean(cen * cen, axis=-1, keepdims=True)
            o = cen * lax.rsqrt(var + EPS) * gnw
            gate = head(z_ref, hh, idx).astype(F32)
            y = (o * gate).astype(BF16)
            y_ref[0, hh * per_head, idx, :] = y[:, :half]
            y_ref[0, hh * per_head + 1, idx, :] = y[:, half:]

    for hh in range(RET_HEADS_PER_STEP):
        state_free(hh)
        if hh > 0:
            finish(hh - 1)
        recurrence(hh)
    finish(RET_HEADS_PER_STEP - 1)


def _ret(proj, gn_w_l):
    B, _, S, _ = proj.shape
    inner, q_decay, k_decay, chunk_decay = _ret_decay_tables()
    blocks = RET_HEADS_PER_STEP * RET_DIM // LANES
    hps = RET_HEADS_PER_STEP

    def spec(base):
        return pl.BlockSpec((1, blocks, S, LANES), lambda b, h: (b, base // blocks + h, 0, 0))

    return pl.pallas_call(
        _ret_kernel,
        grid=(B, RET_HEADS // hps),
        in_specs=[
            pl.BlockSpec(memory_space=pltpu.SMEM),
            spec(BLK_QR), spec(BLK_KR), spec(BLK_VR), spec(BLK_ZR),
            pl.BlockSpec((hps, RET_CHUNK, RET_CHUNK), lambda b, h: (h, 0, 0)),
            pl.BlockSpec((hps, RET_CHUNK, 1), lambda b, h: (h, 0, 0)),
            pl.BlockSpec((hps, RET_CHUNK, 1), lambda b, h: (h, 0, 0)),
            pl.BlockSpec((1, hps * RET_DIM), lambda b, h: (0, h)),
        ],
        out_specs=pl.BlockSpec((1, blocks, S, LANES), lambda b, h: (b, h, 0, 0)),
        out_shape=jax.ShapeDtypeStruct((B, RET_HEADS * RET_DIM // LANES, S, LANES), BF16),
        scratch_shapes=[pltpu.VMEM((2, S, RET_DIM), F32),
                        pltpu.VMEM((2, S // RET_CHUNK, RET_DIM, RET_DIM), F32),
                        pltpu.VMEM((2, S // RET_CHUNK, RET_DIM, RET_DIM), BF16)],
        compiler_params=_params("arbitrary", "arbitrary"),
        name="retention",
    )(chunk_decay, proj, proj, proj, proj, inner, q_decay, k_decay,
      gn_w_l.reshape(1, RET_HEADS * RET_DIM))


def _outproj_kernel(ya_ref, yr_ref, ga_ref, gr_ref, x_ref, gate_ref, wpa_ref, wpr_ref, wo_ref,
                    fnw_ref, o_ref, *, final_norm):
    for r0 in range(0, x_ref.shape[1], ROW_CHUNK):
        ridx = pl.ds(r0, ROW_CHUNK)

        def cat(ref):
            return jnp.concatenate([ref[0, i, ridx, :] for i in range(ref.shape[1])], axis=-1)

        a = jnp.dot(cat(ya_ref), wpa_ref[0], preferred_element_type=F32)
        r = jnp.dot(cat(yr_ref), wpr_ref[0], preferred_element_type=F32)
        merged = (jax.nn.sigmoid(cat(ga_ref).astype(F32)) * a
                  + jax.nn.sigmoid(cat(gr_ref).astype(F32)) * r)
        out = x_ref[0, ridx, :] + gate_ref[0] * jnp.dot(merged.astype(BF16), wo_ref[0],
                                                        preferred_element_type=F32)
        if final_norm:
            ms = jnp.mean(out * out, axis=-1, keepdims=True)
            out = out * lax.rsqrt(ms + EPS) * fnw_ref[...]
        o_ref[0, ridx, :] = out


def _outproj(x, ya, yr, proj, mod_l, wpa, wpr, wo, final_norm_w, layer, final_norm):
    B, S, D = x.shape
    tm = 1024
    n_g = D // LANES
    weight = lambda w: pl.BlockSpec((1,) + w.shape[1:], lambda b, i: (layer, 0, 0))
    return pl.pallas_call(
        functools.partial(_outproj_kernel, final_norm=final_norm),
        grid=(B, S // tm),
        in_specs=[
            pl.BlockSpec((1, ya.shape[1], tm, LANES), lambda b, i: (b, 0, i, 0)),
            pl.BlockSpec((1, yr.shape[1], tm, LANES), lambda b, i: (b, 0, i, 0)),
            pl.BlockSpec((1, n_g, tm, LANES), lambda b, i: (b, BLK_GA // n_g, i, 0)),
            pl.BlockSpec((1, n_g, tm, LANES), lambda b, i: (b, BLK_GR // n_g, i, 0)),
            pl.BlockSpec((1, tm, D), lambda b, i: (b, i, 0)),
            pl.BlockSpec((1, 1, D), lambda b, i: (b, 0, 2)),
            weight(wpa), weight(wpr), weight(wo),
            pl.BlockSpec((1, D), lambda b, i: (0, 0)),
        ],
        out_specs=pl.BlockSpec((1, tm, D), lambda b, i: (b, i, 0)),
        out_shape=jax.ShapeDtypeStruct((B, S, D), F32),
        compiler_params=_params("arbitrary", "arbitrary"),
        name="outproj",
    )(ya, yr, proj, proj, x, mod_l, wpa, wpr, wo, final_norm_w.reshape(1, D))


def kernel(x, c, positions, norm_w, w_ada, b_ada, w_in, ret_gn_w, w_proj_attn, w_proj_ret, w_out,
           final_norm_w):
    B = x.shape[0]
    mod = _ada(c, w_ada, b_ada).reshape(DEPTH, B, 1, 3 * D_MODEL)
    cos, sin = _rope_tables(positions)
    w_in_b = jnp.concatenate([w_in[:, :, s:s + n] for s, n in _column_segments()],
                             axis=-1).astype(BF16)
    wpa_b = w_proj_attn.astype(BF16)
    wpr_b = w_proj_ret.astype(BF16)
    wo_b = w_out.astype(BF16)
    norm_w3 = norm_w.reshape(DEPTH, 1, D_MODEL)
    for l in range(DEPTH):
        proj = _inproj(x, mod[l], norm_w3, w_in_b, cos, sin, layer=l)
        ya = _attn(proj)
        yr = _ret(proj, ret_gn_w[l])
        x = _outproj(x, ya, yr, proj, mod[l], wpa_b, wpr_b, wo_b, final_norm_w, layer=l,
                     final_norm=(l == DEPTH - 1))
    return x
```

```python
import functools

import numpy as np
import jax
import jax.numpy as jnp
from jax import lax
from jax.experimental import pallas as pl
from jax.experimental.pallas import tpu as pltpu

D_MODEL = 1024
SEQ = 2048
DEPTH = 4
ATT_DILATIONS = (1, 4, 16)
N_ATT_GROUPS = 3
ATT_SLOTS = 4
ATT_HEAD_DIM = 128
ATT_BLOCK = 128
ATT_SLOTS_PER_STEP = 2
RET_HEADS = 4
RET_DIM = 256
RET_CHUNK = 512
RET_HEADS_PER_STEP = 4
ROPE_BASE = 10000.0
EPS = 1e-6
NEG_INF = -1e30

LANES = 128
COL_BLOCK = 512
BLOCKS_PER_STEP = COL_BLOCK // LANES
DOTS_PER_STEP = 2
ROW_CHUNK = 512
ATT_QKV = N_ATT_GROUPS * ATT_SLOTS * ATT_HEAD_DIM
IN_WIDTH = 3 * ATT_QKV + ATT_SLOTS * ATT_HEAD_DIM + 4 * RET_HEADS * RET_DIM + 2 * D_MODEL
N_COL_STEPS = IN_WIDTH // COL_BLOCK
N_BLOCKS = IN_WIDTH // LANES


def _column_segments():
    grp = ATT_SLOTS * ATT_HEAD_DIM
    qkv = lambda t, g: (t * ATT_QKV + g * grp, grp)
    seg = [qkv(0, 0), qkv(1, 0), qkv(2, 0), (3 * ATT_QKV, IN_WIDTH - 3 * ATT_QKV)]
    for g in range(1, N_ATT_GROUPS):
        seg += [qkv(0, g), qkv(1, g), qkv(2, g)]
    return seg


_GRP_BLOCKS = ATT_SLOTS * ATT_HEAD_DIM // LANES
_WIDE_BLOCKS = RET_HEADS * RET_DIM // LANES
BLK_ZA = 3 * _GRP_BLOCKS
BLK_QR = BLK_ZA + _GRP_BLOCKS
BLK_KR = BLK_QR + _WIDE_BLOCKS
BLK_VR = BLK_KR + _WIDE_BLOCKS
BLK_ZR = BLK_VR + _WIDE_BLOCKS
BLK_GA = BLK_ZR + _WIDE_BLOCKS
BLK_GR = BLK_GA + D_MODEL // LANES
_BLK_G1 = BLK_GR + D_MODEL // LANES
BLK_Q = (0,) + tuple(_BLK_G1 + (g - 1) * 3 * _GRP_BLOCKS for g in range(1, N_ATT_GROUPS))
BLK_K = tuple(b + _GRP_BLOCKS for b in BLK_Q)
BLK_V = tuple(b + 2 * _GRP_BLOCKS for b in BLK_Q)
N_NATURAL_STEPS = _BLK_G1 // BLOCKS_PER_STEP
STEPS_PER_GROUP = 3 * _GRP_BLOCKS // BLOCKS_PER_STEP

VMEM_LIMIT = 60 * 1024 * 1024
F32 = jnp.float32
BF16 = jnp.bfloat16


def _params(*sem):
    return pltpu.CompilerParams(dimension_semantics=sem, vmem_limit_bytes=VMEM_LIMIT)


def _ada_kernel(c_ref, w_ref, b_ref, o_ref):
    c = c_ref[...]
    c_act = (c * jax.nn.sigmoid(c)).astype(BF16)
    acc = jnp.dot(c_act, w_ref[0].astype(BF16), preferred_element_type=F32)
    o_ref[0] = acc + b_ref[0]


def _ada(c, w_ada, b_ada):
    B = c.shape[0]
    n_col = 3 * D_MODEL // D_MODEL
    return pl.pallas_call(
        _ada_kernel,
        grid=(DEPTH, n_col),
        in_specs=[
            pl.BlockSpec((B, D_MODEL), lambda l, j: (0, 0)),
            pl.BlockSpec((1, D_MODEL, D_MODEL), lambda l, j: (l, 0, j)),
            pl.BlockSpec((1, 1, D_MODEL), lambda l, j: (l, 0, j)),
        ],
        out_specs=pl.BlockSpec((1, B, D_MODEL), lambda l, j: (l, 0, j)),
        out_shape=jax.ShapeDtypeStruct((DEPTH, B, 3 * D_MODEL), F32),
        compiler_params=_params("arbitrary", "arbitrary"),
        name="ada",
    )(c, w_ada, b_ada.reshape(DEPTH, 1, 3 * D_MODEL))


def _rope_kernel(pos_ref, theta_ref, cos_ref, sin_ref):
    ang = pos_ref[0].astype(F32) * theta_ref[...]
    cos_ref[0] = jnp.cos(ang)
    sin_ref[0] = jnp.sin(ang)


def _rope_tables(positions):
    B, S = positions.shape
    half = RET_DIM // 2
    theta = ROPE_BASE ** (-jnp.arange(half, dtype=F32) / half)
    spec = pl.BlockSpec((1, S, half), lambda b: (b, 0, 0))
    return pl.pallas_call(
        _rope_kernel,
        grid=(B,),
        in_specs=[pl.BlockSpec((1, S, 1), lambda b: (b, 0, 0)),
                  pl.BlockSpec((1, half), lambda b: (0, 0))],
        out_specs=[spec, spec],
        out_shape=[jax.ShapeDtypeStruct((B, S, half), F32)] * 2,
        compiler_params=_params("arbitrary"),
        name="rope",
    )(positions.reshape(B, S, 1), theta.reshape(1, half))


def _row_order_of_column_block(cb):
    return 0 if cb < N_NATURAL_STEPS else 1 + (cb - N_NATURAL_STEPS) // STEPS_PER_GROUP


def _epilogue_of_column_block(cb):
    blk = cb * BLOCKS_PER_STEP
    if BLK_QR <= blk < BLK_KR:
        return 1.0
    if BLK_KR <= blk < BLK_VR:
        return RET_DIM ** -0.5
    if BLK_ZA <= blk < BLK_QR or BLK_ZR <= blk < BLK_GA:
        return "silu"
    return None


def _inproj_kernel(x_hbm, mod_ref, nw_ref, w_ref, cos_ref, sin_ref, o_ref,
                   h0_ref, h1_ref, h2_ref, hn_ref, hp_ref, x_ref, x_sem):
    b, j = pl.program_id(0), pl.program_id(1)
    S = x_ref.shape[0]
    rows = 128

    def x_copy(seq):
        return pltpu.make_async_copy(x_hbm.at[seq], x_ref, x_sem)

    @pl.when((j == 0) & (b == 0))
    def _():
        x_copy(b).start()

    @pl.when((j == 1) & (b + 1 < pl.num_programs(0)))
    def _():
        x_copy(b + 1).start()

    n_lane_blocks = D_MODEL // LANES
    h_refs = (h0_ref, h1_ref, h2_ref)
    half = RET_DIM // 2

    def project(h_ref, i, epilogue, mc):
        ridx = pl.ds(mc * ROW_CHUNK, ROW_CHUNK)
        res = jnp.dot(h_ref[ridx, :], w_ref[0, :, i * COL_BLOCK:(i + 1) * COL_BLOCK],
                      preferred_element_type=F32)
        blocks = [res[:, c * LANES:(c + 1) * LANES] for c in range(BLOCKS_PER_STEP)]
        if epilogue == "silu":
            halves = [0.5 * blk for blk in blocks]
            blocks = [hz + hz * jnp.tanh(hz) for hz in halves]
        elif epilogue is not None:
            cos, sin = cos_ref[0, ridx, :], sin_ref[0, ridx, :]
            if epilogue != 1.0:
                cos, sin = cos * epilogue, sin * epilogue
            for c in range(0, BLOCKS_PER_STEP, RET_DIM // LANES):
                t1, t2 = blocks[c], blocks[c + half // LANES]
                blocks[c] = t1 * cos - t2 * sin
                blocks[c + half // LANES] = t2 * cos + t1 * sin
        for c, blk in enumerate(blocks):
            o_ref[0, i * BLOCKS_PER_STEP + c, ridx, :] = blk.astype(BF16)

    def natural_slice(r0, wmul, shift):
        idx = pl.ds(r0, rows)
        xs = x_ref[idx, :]
        ms = jnp.mean(xs * xs, axis=-1, keepdims=True)
        hv = xs * lax.rsqrt(ms + EPS) * wmul + shift
        h0_ref[idx, :] = hv.astype(BF16)
        for cb in range(n_lane_blocks):
            hn_ref[cb, idx, :] = hv[:, cb * LANES:(cb + 1) * LANES]

    def hop_slice(hop, c):
        step = ATT_DILATIONS[hop + 1] // ATT_DILATIONS[hop]
        n_src = S // step
        src_ref = hp_ref if hop else hn_ref
        p0 = pl.multiple_of(c * rows, rows)
        start = (p0 % n_src) * step + p0 // n_src
        pieces = [src_ref[cb, pl.ds(start, rows, stride=step), :] for cb in range(n_lane_blocks)]
        h_refs[hop + 1][pl.ds(p0, rows), :] = jnp.concatenate(pieces, axis=-1).astype(BF16)
        if hop + 2 < len(ATT_DILATIONS):
            for cb in range(n_lane_blocks):
                hp_ref[cb, pl.ds(p0, rows), :] = pieces[cb]

    n_row_chunks = S // ROW_CHUNK
    slices_per_step = n_row_chunks
    hop_steps = S // rows // slices_per_step
    for hop in range(len(ATT_DILATIONS) - 1):
        first_use = (N_NATURAL_STEPS + hop * STEPS_PER_GROUP) // DOTS_PER_STEP
        assert 1 + (hop + 1) * hop_steps <= first_use

    def phase_of_step(s):
        if s == 0:
            return "natural"
        hop = (s - 1) // hop_steps
        return hop if hop + 1 < len(ATT_DILATIONS) else None

    steps_by_kind = {}
    for s in range(N_COL_STEPS // DOTS_PER_STEP):
        cbs = [s * DOTS_PER_STEP + i for i in range(DOTS_PER_STEP)]
        kind = tuple((_row_order_of_column_block(cb), _epilogue_of_column_block(cb))
                     for cb in cbs)
        steps_by_kind.setdefault((kind, phase_of_step(s)), []).append(s)
    for (kind, phase), steps in steps_by_kind.items():
        @pl.when(functools.reduce(jnp.logical_or, [j == s for s in steps]))
        def _(kind=kind, phase=phase):
            if phase == "natural":
                x_copy(b).wait()
                shift = mod_ref[0, :, 0:D_MODEL]
                wmul = nw_ref[0] * (1.0 + mod_ref[0, :, D_MODEL:2 * D_MODEL])
            for mc in range(n_row_chunks):
                if phase == "natural":
                    for r0 in range(mc * ROW_CHUNK, (mc + 1) * ROW_CHUNK, rows):
                        natural_slice(r0, wmul, shift)
                for i, (order, epilogue) in enumerate(kind):
                    project(h_refs[order], i, epilogue, mc)
                if phase not in ("natural", None):
                    first = 1 + phase * hop_steps
                    hop_slice(phase, (j - first) * slices_per_step + mc)


def _inproj(x, mod_l, norm_w, w_in, cos, sin, layer):
    B, S, D = x.shape
    tab = pl.BlockSpec((1, S, LANES), lambda b, j: (b, 0, 0))
    return pl.pallas_call(
        _inproj_kernel,
        grid=(B, N_COL_STEPS // DOTS_PER_STEP),
        in_specs=[
            pl.BlockSpec(memory_space=pl.ANY),
            pl.BlockSpec((1, 1, 3 * D), lambda b, j: (b, 0, 0)),
            pl.BlockSpec((1, 1, D), lambda b, j: (layer, 0, 0)),
            pl.BlockSpec((1, D, DOTS_PER_STEP * COL_BLOCK), lambda b, j: (layer, 0, j)),
            tab, tab,
        ],
        out_specs=pl.BlockSpec((1, DOTS_PER_STEP * BLOCKS_PER_STEP, S, LANES),
                               lambda b, j: (b, j, 0, 0)),
        out_shape=jax.ShapeDtypeStruct((B, N_BLOCKS, S, LANES), BF16),
        scratch_shapes=[pltpu.VMEM((S, D), BF16)] * N_ATT_GROUPS
                       + [pltpu.VMEM((D // LANES, S, LANES), F32)] * 2
                       + [pltpu.VMEM((S, D), F32), pltpu.SemaphoreType.DMA(())],
        compiler_params=_params("arbitrary", "arbitrary"),
        name="inproj",
    )(x, mod_l, norm_w, w_in, cos, sin)


def _attn_kernel(q0, k0, v0, q1, k1, v1, q2, k2, v2, z_ref, y_ref,
                 sb_sc, sd_sc, m_sc, acc_sc, l_sc):
    S = y_ref.shape[2]
    T = ATT_BLOCK
    c_exp2 = ATT_HEAD_DIM ** -0.5 * np.log2(np.e)
    row = lax.broadcasted_iota(jnp.int32, (T, 2 * T), 0)
    col = lax.broadcasted_iota(jnp.int32, (T, 2 * T), 1)
    mask_band = (col >= row) & (col <= row + T)
    mask_diag = (lax.broadcasted_iota(jnp.int32, (T, T), 1)
                 <= lax.broadcasted_iota(jnp.int32, (T, T), 0))
    groups = ((q0, k0, v0), (q1, k1, v1), (q2, k2, v2))
    d1, d2 = ATT_DILATIONS[1], ATT_DILATIONS[2]
    len1 = S // d1
    nb1 = len1 // T
    n_band0 = S // T - 1

    def scores(sl, g, q_start, band, blk, out_idx, m_idx):
        q_ref, k_ref, _ = groups[g]
        q = q_ref[0, sl, pl.ds(q_start, T), :]
        if band:
            k = k_ref[0, sl, pl.ds(q_start - T, 2 * T), :]
        else:
            k = k_ref[0, sl, pl.ds(q_start, T), :]
        s = lax.dot_general(q, k, (((1,), (1,)), ((), ())), preferred_element_type=F32) * c_exp2
        s = jnp.where(mask_band if band else mask_diag, s, NEG_INF)
        (sb_sc if band else sd_sc)[blk] = s
        m_sc[g, out_idx, :] = jnp.broadcast_to(jnp.max(s, axis=-1, keepdims=True), (T, LANES))

    def weighted_values(sl, g, q_start, band, blk, out_idx, m_idx):
        v_ref = groups[g][2]
        m = m_sc[min(g, 1), m_idx, :]
        if band:
            p = jnp.exp2(sb_sc[blk] - jnp.concatenate([m, m], axis=-1))
            v = v_ref[0, sl, pl.ds(q_start - T, 2 * T), :]
        else:
            p = jnp.exp2(sd_sc[blk] - m)
            v = v_ref[0, sl, pl.ds(q_start, T), :]
        v1 = jnp.concatenate([v, jnp.ones_like(v)], axis=-1)
        r = jnp.dot(p.astype(BF16), v1, preferred_element_type=F32)
        acc_sc[g, out_idx, :] = r[:, :LANES]
        l_sc[g, out_idx, :] = r[:, LANES:]

    blocks = []
    for i in range(S // T):
        blocks.append((0, i * T, i > 0, max(i - 1, 0), pl.ds(i * T, T), pl.ds(i * T, T)))
    for r in range(d1):
        for n in range(nb1):
            idx = pl.ds(r * len1 + n * T, T)
            blk = 1 + r if n == 0 else n_band0 + r * (nb1 - 1) + n - 1
            blocks.append((1, r * len1 + n * T, n > 0, blk, idx, idx))
    ratio = d2 // d1
    for r in range(d2):
        idx = pl.ds((r % d1) * len1 + r // d1, T, stride=ratio)
        blocks.append((2, r * T, False, 1 + d1 + r, idx, idx))

    def all_blocks(fn):
        for blk in blocks:
            fn(*blk)

    rows = 256

    def natural_rows(c):
        p0 = pl.multiple_of(c * rows, rows)
        return pl.ds(p0, rows), pl.ds((p0 % len1) * d1 + p0 // len1, rows, stride=d1)

    def shared_max(c, carry):
        idx, nat = natural_rows(c)
        m = jnp.maximum(jnp.maximum(m_sc[0, nat, :], m_sc[1, idx, :]), m_sc[2, idx, :])
        m_sc[0, nat, :] = m
        m_sc[1, idx, :] = m
        return carry

    def combine(c, carry):
        idx, nat = natural_rows(c)
        o = ((acc_sc[0, nat, :] + acc_sc[1, idx, :] + acc_sc[2, idx, :])
             / (l_sc[0, nat, :] + l_sc[1, idx, :] + l_sc[2, idx, :]))
        acc_sc[0, nat, :] = o
        return carry

    def finish(sl, c, carry):
        idx = pl.ds(pl.multiple_of(c * rows, rows), rows)
        y_ref[0, sl, idx, :] = acc_sc[0, idx, :].astype(BF16) * z_ref[0, sl, idx, :]
        return carry

    for sl in range(ATT_SLOTS_PER_STEP):
        all_blocks(functools.partial(scores, sl))
        lax.fori_loop(0, S // rows, shared_max, 0, unroll=True)
        all_blocks(functools.partial(weighted_values, sl))
        lax.fori_loop(0, S // rows, combine, 0, unroll=True)
        lax.fori_loop(0, S // rows, functools.partial(finish, sl), 0)


def _attn(proj):
    B, _, S, _ = proj.shape
    sps = ATT_SLOTS_PER_STEP

    def spec(base):
        return pl.BlockSpec((1, sps, S, LANES), lambda b, s: (b, base // sps + s, 0, 0))

    in_specs = []
    for g in range(N_ATT_GROUPS):
        in_specs += [spec(BLK_Q[g]), spec(BLK_K[g]), spec(BLK_V[g])]
    in_specs.append(spec(BLK_ZA))
    n_diag = sum(ATT_DILATIONS)
    n_band = N_ATT_GROUPS * S // ATT_BLOCK - n_diag
    return pl.pallas_call(
        _attn_kernel,
        grid=(B, ATT_SLOTS // sps),
        in_specs=in_specs,
        out_specs=pl.BlockSpec((1, sps, S, LANES), lambda b, s: (b, s, 0, 0)),
        out_shape=jax.ShapeDtypeStruct((B, ATT_SLOTS, S, LANES), BF16),
        scratch_shapes=[pltpu.VMEM((n_band, ATT_BLOCK, 2 * ATT_BLOCK), F32),
                        pltpu.VMEM((n_diag, ATT_BLOCK, ATT_BLOCK), F32),
                        pltpu.VMEM((N_ATT_GROUPS, S, LANES), F32),
                        pltpu.VMEM((N_ATT_GROUPS, S, LANES), F32),
                        pltpu.VMEM((N_ATT_GROUPS, S, LANES), F32)],
        compiler_params=_params("arbitrary", "arbitrary"),
        name="attn",
    )(*([proj] * 10))


def _ret_decay_tables():
    H, C = RET_HEADS, RET_CHUNK
    log_g = np.log1p(-np.exp2(-5.0 - np.arange(H, dtype=np.float64)))
    idx = np.arange(C, dtype=np.float64)
    diff = idx[:, None] - idx[None, :]
    inner = np.where(diff >= 0, np.exp(log_g[:, None, None] * np.maximum(diff, 0.0)), 0.0)
    q_decay = np.exp(log_g[:, None] * (idx + 1.0))
    k_decay = np.exp(log_g[:, None] * (C - 1.0 - idx))
    chunk_decay = np.exp(log_g * C)
    f = lambda a: jnp.asarray(a, dtype=F32)
    return f(inner), f(q_decay[:, :, None]), f(k_decay[:, :, None]), f(chunk_decay)


def _ret_kernel(cd_ref, q_ref, k_ref, v_ref, z_ref, inner_ref, qd_ref, kd_ref,
                gnw_ref, y_ref, intra_sc, u_sc, st_sc):
    S = y_ref.shape[2]
    C = RET_CHUNK
    n_chunks = S // C
    half = RET_DIM // 2
    per_head = RET_DIM // LANES

    def head(ref, hh, idx):
        return jnp.concatenate([ref[0, hh * per_head + i, idx, :] for i in range(per_head)],
                               axis=-1)

    def state_free(hh):
        for n in range(n_chunks):
            idx = pl.ds(n * C, C)
            qc, kc, vc = head(q_ref, hh, idx), head(k_ref, hh, idx), head(v_ref, hh, idx)
            att = lax.dot_general(qc, kc, (((1,), (1,)), ((), ())), preferred_element_type=F32)
            att = (att * inner_ref[hh]).astype(BF16)
            intra_sc[hh % 2, idx, :] = jnp.dot(att, vc, preferred_element_type=F32)
            if n + 1 < n_chunks:
                kdt = (kc.astype(F32) * kd_ref[hh]).T.astype(BF16)
                u_sc[hh % 2, n] = jnp.dot(kdt, vc, preferred_element_type=F32)

    def recurrence(hh):
        cd = cd_ref[pl.program_id(1) * RET_HEADS_PER_STEP + hh]
        slab = 64
        for r0 in range(0, RET_DIM, slab):
            state = jnp.zeros((slab, RET_DIM), F32)
            for n in range(1, n_chunks):
                state = state * cd + u_sc[hh % 2, n - 1, r0:r0 + slab, :]
                st_sc[hh % 2, n, r0:r0 + slab, :] = state.astype(BF16)

    def finish(hh):
        gnw = gnw_ref[:, hh * RET_DIM:(hh + 1) * RET_DIM]
        for n in range(n_chunks):
            idx = pl.ds(n * C, C)
            out = intra_sc[hh % 2, idx, :]
            if n > 0:
                out = out + jnp.dot(head(q_ref, hh, idx), st_sc[hh % 2, n],
                                    preferred_element_type=F32) * qd_ref[hh]
            mu = jnp.mean(out, axis=-1, keepdims=True)
            cen = out - mu
            var = jnp.mean(cen * cen, axis=-1, keepdims=True)
            o = cen * lax.rsqrt(var + EPS) * gnw
            y = o.astype(BF16) * head(z_ref, hh, idx)
            y_ref[0, hh * per_head, idx, :] = y[:, :half]
            y_ref[0, hh * per_head + 1, idx, :] = y[:, half:]

    for hh in range(RET_HEADS_PER_STEP):
        state_free(hh)
        if hh > 0:
            finish(hh - 1)
        recurrence(hh)
    finish(RET_HEADS_PER_STEP - 1)


def _ret(proj, gn_w_l):
    B, _, S, _ = proj.shape
    inner, q_decay, k_decay, chunk_decay = _ret_decay_tables()
    blocks = RET_HEADS_PER_STEP * RET_DIM // LANES
    hps = RET_HEADS_PER_STEP

    def spec(base):
        return pl.BlockSpec((1, blocks, S, LANES), lambda b, h: (b, base // blocks + h, 0, 0))

    return pl.pallas_call(
        _ret_kernel,
        grid=(B, RET_HEADS // hps),
        in_specs=[
            pl.BlockSpec(memory_space=pltpu.SMEM),
            spec(BLK_QR), spec(BLK_KR), spec(BLK_VR), spec(BLK_ZR),
            pl.BlockSpec((hps, RET_CHUNK, RET_CHUNK), lambda b, h: (h, 0, 0)),
            pl.BlockSpec((hps, RET_CHUNK, 1), lambda b, h: (h, 0, 0)),
            pl.BlockSpec((hps, RET_CHUNK, 1), lambda b, h: (h, 0, 0)),
            pl.BlockSpec((1, hps * RET_DIM), lambda b, h: (0, h)),
        ],
        out_specs=pl.BlockSpec((1, blocks, S, LANES), lambda b, h: (b, h, 0, 0)),
        out_shape=jax.ShapeDtypeStruct((B, RET_HEADS * RET_DIM // LANES, S, LANES), BF16),
        scratch_shapes=[pltpu.VMEM((2, S, RET_DIM), F32),
                        pltpu.VMEM((2, S // RET_CHUNK, RET_DIM, RET_DIM), F32),
                        pltpu.VMEM((2, S // RET_CHUNK, RET_DIM, RET_DIM), BF16)],
        compiler_params=_params("arbitrary", "arbitrary"),
        name="retention",
    )(chunk_decay, proj, proj, proj, proj, inner, q_decay, k_decay,
      gn_w_l.reshape(1, RET_HEADS * RET_DIM))


def _outproj_kernel(ya_ref, yr_ref, ga_ref, gr_ref, x_ref, gate_ref, wpa_ref, wpr_ref, wo_ref,
                    fnw_ref, o_ref, *, final_norm):
    for r0 in range(0, x_ref.shape[1], ROW_CHUNK):
        ridx = pl.ds(r0, ROW_CHUNK)

        def cat(ref):
            return jnp.concatenate([ref[0, i, ridx, :] for i in range(ref.shape[1])], axis=-1)

        a = jnp.dot(cat(ya_ref), wpa_ref[0], preferred_element_type=F32)
        r = jnp.dot(cat(yr_ref), wpr_ref[0], preferred_element_type=F32)
        merged = (jax.nn.sigmoid(cat(ga_ref).astype(F32)) * a
                  + jax.nn.sigmoid(cat(gr_ref).astype(F32)) * r)
        out = x_ref[0, ridx, :] + gate_ref[0] * jnp.dot(merged.astype(BF16), wo_ref[0],
                                                        preferred_element_type=F32)
        if final_norm:
            ms = jnp.mean(out * out, axis=-1, keepdims=True)
            out = out * lax.rsqrt(ms + EPS) * fnw_ref[...]
        o_ref[0, ridx, :] = out


def _outproj(x, ya, yr, proj, mod_l, wpa, wpr, wo, final_norm_w, layer, final_norm):
    B, S, D = x.shape
    tm = 1024
    n_g = D // LANES
    weight = lambda w: pl.BlockSpec((1,) + w.shape[1:], lambda b, i: (layer, 0, 0))
    return pl.pallas_call(
        functools.partial(_outproj_kernel, final_norm=final_norm),
        grid=(B, S // tm),
        in_specs=[
            pl.BlockSpec((1, ya.shape[1], tm, LANES), lambda b, i: (b, 0, i, 0)),
            pl.BlockSpec((1, yr.shape[1], tm, LANES), lambda b, i: (b, 0, i, 0)),
            pl.BlockSpec((1, n_g, tm, LANES), lambda b, i: (b, BLK_GA // n_g, i, 0)),
            pl.BlockSpec((1, n_g, tm, LANES), lambda b, i: (b, BLK_GR // n_g, i, 0)),
            pl.BlockSpec((1, tm, D), lambda b, i: (b, i, 0)),
            pl.BlockSpec((1, 1, D), lambda b, i: (b, 0, 2)),
            weight(wpa), weight(wpr), weight(wo),
            pl.BlockSpec((1, D), lambda b, i: (0, 0)),
        ],
        out_specs=pl.BlockSpec((1, tm, D), lambda b, i: (b, i, 0)),
        out_shape=jax.ShapeDtypeStruct((B, S, D), F32),
        compiler_params=_params("arbitrary", "arbitrary"),
        name="outproj",
    )(ya, yr, proj, proj, x, mod_l, wpa, wpr, wo, final_norm_w.reshape(1, D))


def kernel(x, c, positions, norm_w, w_ada, b_ada, w_in, ret_gn_w, w_proj_attn, w_proj_ret, w_out,
           final_norm_w):
    B = x.shape[0]
    mod = _ada(c, w_ada, b_ada).reshape(DEPTH, B, 1, 3 * D_MODEL)
    cos, sin = _rope_tables(positions)
    w_in_b = jnp.concatenate([w_in[:, :, s:s + n] for s, n in _column_segments()],
                             axis=-1).astype(BF16)
    wpa_b = w_proj_attn.astype(BF16)
    wpr_b = w_proj_ret.astype(BF16)
    wo_b = w_out.astype(BF16)
    norm_w3 = norm_w.reshape(DEPTH, 1, D_MODEL)
    for l in range(DEPTH):
        proj = _inproj(x, mod[l], norm_w3, w_in_b, cos, sin, layer=l)
        ya = _attn(proj)
        yr = _ret(proj, ret_gn_w[l])
        x = _outproj(x, ya, yr, proj, mod[l], wpa_b, wpr_b, wo_b, final_norm_w, layer=l,
                     final_norm=(l == DEPTH - 1))
    return x
```
